```python
import math
import jax, jax.numpy as jnp
from jax import lax
import numpy as np

D_MODEL = 1024
BATCH = 8
SEQ = 2048
DEPTH = 2

GRID_W = 64
CTX_LEN = 256
MIX_W = 512
IN_W = 5 * MIX_W
A_CONV = 3
B_GROUPS = 8
CHUNK = 128
C_CONV = 31
D_HEADS = 4
D_HEAD_DIM = 64
ROPE_THETA = 10000.0
N_ADA = 6
PEER_HEADS = 8
PEER_KEYS = 128
PEER_EXPERTS = PEER_KEYS * PEER_KEYS
PEER_QDIM = 256
PEER_TOPK = 16
PEER_BLOCK = 128
EPS = 1e-6
N_EVEN = (DEPTH + 1) // 2
N_ODD = DEPTH // 2

kernel_name = "hybrid_diffusion_conv_gmlp_conformer_diffattn_peer"


def rms_norm(x, g):
    xf = x.astype(jnp.float32)
    y = xf * lax.rsqrt(jnp.mean(xf * xf, axis=-1, keepdims=True) + EPS)
    return (y * g.astype(jnp.float32)).astype(x.dtype)


def modulate(h, shift, scale):
    return h * (1.0 + scale) + shift


def depthwise_conv(x, w):
    k = w.shape[0]
    return lax.conv_general_dilated(
        x, w[:, None, :].astype(x.dtype), window_strides=(1,),
        padding=((k // 2, k // 2),), dimension_numbers=('NWC', 'WIO', 'NWC'),
        feature_group_count=x.shape[-1])


def axial_rope_tables(rows):
    r = jnp.repeat(jnp.arange(rows, dtype=jnp.float32), GRID_W)
    col = jnp.tile(jnp.arange(GRID_W, dtype=jnp.float32), rows)
    n_freq = D_HEAD_DIM // 4
    inv = ROPE_THETA ** (-jnp.arange(n_freq, dtype=jnp.float32) / n_freq)
    ang = jnp.concatenate([r[:, None] * inv, col[:, None] * inv], axis=-1)
    return jnp.cos(ang), jnp.sin(ang)


def apply_rope(t, cos, sin):
    tf = t.astype(jnp.float32)
    t1, t2 = tf[..., 0::2], tf[..., 1::2]
    cs, sn = cos[None, :, None, None, :], sin[None, :, None, None, :]
    out = jnp.stack([t1 * cs - t2 * sn, t1 * sn + t2 * cs], axis=-1).reshape(t.shape)
    return out.astype(t.dtype)


def short_conv_mixer(hx, gate_c, gate_b, conv_w):
    return gate_b * depthwise_conv(gate_c * hx, conv_w)


def chunk_mlp_mixer(u, v, vnorm_g, w_s, b_s):
    bsz, L, _ = u.shape
    u = jax.nn.gelu(u)
    v = rms_norm(jax.nn.gelu(v), vnorm_g)
    v = v.reshape(bsz, L // CHUNK, CHUNK, B_GROUPS, MIX_W // B_GROUPS)
    s = jnp.einsum('gpq,bnqgc->bnpgc', w_s, v) + b_s.T[:, :, None]
    return u * s.reshape(bsz, L, MIX_W)


def even_mixer(h, w_in, w_out, conv_w, vnorm_g, w_s, b_s):
    a_h, a_c, a_b, b_u, b_v = jnp.split(h @ w_in, 5, axis=-1)
    ya = short_conv_mixer(a_h, a_c, a_b, conv_w)
    yb = chunk_mlp_mixer(b_u, b_v, vnorm_g, w_s, b_s)
    return jnp.concatenate([ya, yb], axis=-1) @ w_out


def conformer_conv(a, gate, conv_w, conv_b, norm_g):
    y = a * jax.nn.sigmoid(gate)
    y = depthwise_conv(y, conv_w) + conv_b
    return jax.nn.silu(rms_norm(y, norm_g))


def diff_attend(q, k, v, lam):
    s = jnp.einsum('bqhmd,bkhmd->bhmqk', q, k).astype(jnp.float32) * (D_HEAD_DIM ** -0.5)
    p = jax.nn.softmax(s, axis=-1)
    a = p[:, :, 0] - lam * p[:, :, 1]
    return jnp.einsum('bhqk,bkhe->bqhe', a.astype(v.dtype), v)


def diff_attend_blocked(q, k, v, lam):
    bsz, L = q.shape[:2]
    qb = q.reshape(bsz, L // CHUNK, CHUNK, *q.shape[2:]).swapaxes(0, 1)
    out = lax.map(lambda qi: diff_attend(qi, k, v, lam), qb)
    return out.swapaxes(0, 1).reshape(bsz, L, *out.shape[3:])


def odd_mixer(hx, hc, w_in, w_out, conv_w, conv_b, cnorm_g, qk_g, lam_p, subln_g,
              lam_init, cos, sin, need_ctx_out):
    bsz, L, _ = hx.shape

    def heads(t):
        return t.reshape(*t.shape[:2], D_HEADS, 2, D_HEAD_DIM)

    def values(t):
        return t.reshape(*t.shape[:2], D_HEADS, 2 * D_HEAD_DIM)

    lam_f = lam_p.astype(jnp.float32)
    lam = (jnp.exp(jnp.sum(lam_f[0] * lam_f[1])) - jnp.exp(jnp.sum(lam_f[2] * lam_f[3]))
           + lam_init)

    def finish(o):
        return (rms_norm(o, subln_g) * (1.0 - lam_init)).reshape(*o.shape[:2], MIX_W)

    ca, cg, qx, kx, vx = jnp.split(hx @ w_in, 5, axis=-1)
    qx = apply_rope(rms_norm(heads(qx), qk_g[0]), cos, sin)
    kx = apply_rope(rms_norm(heads(kx), qk_g[1]), cos, sin)
    if need_ctx_out:
        cca, ccg, qc, kc, vc = jnp.split(hc @ w_in, 5, axis=-1)
    else:
        kc, vc = jnp.split(hc @ w_in[:, 3 * MIX_W:], 2, axis=-1)
    kc = rms_norm(heads(kc), qk_g[1])
    vc = values(vc)
    k_all = jnp.concatenate([kc, kx], axis=1)
    v_all = jnp.concatenate([vc, values(vx)], axis=1)
    yx = jnp.concatenate([conformer_conv(ca, cg, conv_w, conv_b, cnorm_g),
                          finish(diff_attend_blocked(qx, k_all, v_all, lam))], axis=-1) @ w_out
    if not need_ctx_out:
        return yx, None
    qc = rms_norm(heads(qc), qk_g[0])
    yc = jnp.concatenate([conformer_conv(cca, ccg, conv_w, conv_b, cnorm_g),
                          finish(diff_attend(qc, kc, vc, lam))], axis=-1) @ w_out
    return yx, yc


def peer(h, wq, subkeys, u_tab, v_tab):
    shp = h.shape
    t = h.reshape(-1, D_MODEL)
    q = (t @ wq).reshape(-1, PEER_HEADS, 2, PEER_QDIM // 2)
    s = jnp.einsum('thpd,hpkd->thpk', q, subkeys).astype(jnp.float32)
    sv, si = lax.top_k(s, PEER_TOPK)
    cand = sv[:, :, 0, :, None] + sv[:, :, 1, None, :]
    cv, ci = lax.top_k(cand.reshape(*cand.shape[:2], PEER_TOPK * PEER_TOPK), PEER_TOPK)
    i1 = jnp.take_along_axis(si[:, :, 0], ci // PEER_TOPK, axis=-1)
    i2 = jnp.take_along_axis(si[:, :, 1], ci % PEER_TOPK, axis=-1)
    idx = i1 * PEER_KEYS + i2
    g = jax.nn.softmax(cv, axis=-1)
    nblk = t.shape[0] // PEER_BLOCK

    def block(args):
        tb, ib, gb = args
        act = jax.nn.gelu(jnp.einsum('td,thkd->thk', tb, u_tab[ib]).astype(jnp.float32))
        return jnp.einsum('thk,thkd->td', (gb * act).astype(tb.dtype), v_tab[ib])

    out = lax.map(block, (t.reshape(nblk, PEER_BLOCK, D_MODEL),
                          idx.reshape(nblk, PEER_BLOCK, PEER_HEADS, PEER_TOPK),
                          g.reshape(nblk, PEER_BLOCK, PEER_HEADS, PEER_TOPK)))
    return out.reshape(shp)


def setup_inputs(seed: int = 0) -> dict:
    key = jax.random.key(seed)
    ks = jax.random.split(key, 24)

    def nrm(k, shape, s):
        return jax.random.normal(k, shape, jnp.float32) * s

    return {
        "x": nrm(ks[0], (BATCH, SEQ, D_MODEL), 1.0),
        "c": nrm(ks[1], (BATCH, D_MODEL), 1.0),
        "ctx": nrm(ks[2], (BATCH, CTX_LEN, D_MODEL), 1.0),
        "c_ctx": nrm(ks[3], (D_MODEL,), 1.0),
        "w_ada": nrm(ks[4], (DEPTH, D_MODEL, N_ADA * D_MODEL), 0.5 * D_MODEL ** -0.5),
        "b_ada": nrm(ks[5], (DEPTH, N_ADA * D_MODEL), 0.01),
        "norm1_g": 1.0 + nrm(ks[6], (DEPTH, D_MODEL), 0.02),
        "norm2_g": 1.0 + nrm(ks[7], (DEPTH, D_MODEL), 0.02),
        "w_in": nrm(ks[8], (DEPTH, D_MODEL, IN_W), D_MODEL ** -0.5),
        "w_out": nrm(ks[9], (DEPTH, 2 * MIX_W, D_MODEL), (2 * MIX_W) ** -0.5),
        "a_conv_w": nrm(ks[10], (N_EVEN, A_CONV, MIX_W), A_CONV ** -0.5),
        "b_vnorm_g": 1.0 + nrm(ks[11], (N_EVEN, MIX_W), 0.02),
        "b_spatial_w": nrm(ks[12], (N_EVEN, B_GROUPS, CHUNK, CHUNK), 0.5 * CHUNK ** -0.5),
        "b_spatial_b": 1.0 + nrm(ks[13], (N_EVEN, B_GROUPS, CHUNK), 0.01),
        "c_conv_w": nrm(ks[14], (N_ODD, C_CONV, MIX_W), C_CONV ** -0.5),
        "c_conv_b": nrm(ks[15], (N_ODD, MIX_W), 0.01),
        "c_norm_g": 1.0 + nrm(ks[16], (N_ODD, MIX_W), 0.02),
        "d_qk_norm_g": 1.0 + nrm(ks[17], (N_ODD, 2, D_HEAD_DIM), 0.02),
        "d_lambda": nrm(ks[18], (N_ODD, 4, D_HEAD_DIM), 0.1),
        "d_subln_g": 1.0 + nrm(ks[19], (N_ODD, 2 * D_HEAD_DIM), 0.02),
        "peer_wq": nrm(ks[20], (DEPTH, D_MODEL, PEER_HEADS * PEER_QDIM), D_MODEL ** -0.5),
        "peer_subkeys": nrm(ks[21], (DEPTH, PEER_HEADS, 2, PEER_KEYS, PEER_QDIM // 2),
                            (PEER_QDIM // 2) ** -0.5),
        "peer_u": nrm(ks[22], (DEPTH, PEER_EXPERTS, D_MODEL), D_MODEL ** -0.5),
        "peer_v": nrm(ks[23], (DEPTH, PEER_EXPERTS, D_MODEL), 1.0),
    }


def reference(x, c, ctx, c_ctx, w_ada, b_ada, norm1_g, norm2_g, w_in, w_out,
              a_conv_w, b_vnorm_g, b_spatial_w, b_spatial_b,
              c_conv_w, c_conv_b, c_norm_g, d_qk_norm_g, d_lambda, d_subln_g,
              peer_wq, peer_subkeys, peer_u, peer_v):
    bsz, L, _ = x.shape
    rows = L // GRID_W
    cos, sin = axial_rope_tables(rows)
    for l in range(DEPTH):
        last = l == DEPTH - 1
        j = l // 2
        ada_x = (jax.nn.silu(c) @ w_ada[l] + b_ada[l]).reshape(bsz, 1, N_ADA, D_MODEL)
        hx = modulate(rms_norm(x, norm1_g[l]), ada_x[:, :, 0], ada_x[:, :, 1])
        ctx_read = (not last) or (l % 2 == 1)
        if ctx_read:
            ada_c = (jax.nn.silu(c_ctx) @ w_ada[l] + b_ada[l]).reshape(1, 1, N_ADA, D_MODEL)
            hc = modulate(rms_norm(ctx, norm1_g[l]), ada_c[:, :, 0], ada_c[:, :, 1])
        if l % 2 == 0:
            yx = even_mixer(hx, w_in[l], w_out[l], a_conv_w[j], b_vnorm_g[j],
                            b_spatial_w[j], b_spatial_b[j])
            yc = None if last else even_mixer(hc, w_in[l], w_out[l], a_conv_w[j], b_vnorm_g[j],
                                              b_spatial_w[j], b_spatial_b[j])
        else:
            lam_init = 0.8 - 0.6 * math.exp(-0.3 * l)
            yx, yc = odd_mixer(hx, hc, w_in[l], w_out[l], c_conv_w[j], c_conv_b[j], c_norm_g[j],
                               d_qk_norm_g[j], d_lambda[j], d_subln_g[j], lam_init, cos, sin,
                               not last)
        x = x + ada_x[:, :, 2] * yx
        h2 = modulate(rms_norm(x, norm2_g[l]), ada_x[:, :, 3], ada_x[:, :, 4])
        x = x + ada_x[:, :, 5] * peer(h2, peer_wq[l], peer_subkeys[l], peer_u[l], peer_v[l])
        if not last:
            ctx = ctx + ada_c[:, :, 2] * yc
            h2c = modulate(rms_norm(ctx, norm2_g[l]), ada_c[:, :, 3], ada_c[:, :, 4])
            ctx = ctx + ada_c[:, :, 5] * peer(h2c, peer_wq[l], peer_subkeys[l], peer_u[l], peer_v[l])
    return x
```

```python
import functools
import math

import jax
import jax.numpy as jnp
from jax import lax
from jax.experimental import pallas as pl
from jax.experimental.pallas import tpu as pltpu

F32 = jnp.float32
BF16 = jnp.bfloat16

D_MODEL = 1024
MIX_W = 512
IN_W = 5 * MIX_W
GRID_W = 64
CHUNK = 128
B_GROUPS = 8
A_CONV = 3
C_CONV = 31
D_HEADS = 4
D_HEAD_DIM = 64
ROPE_THETA = 10000.0
N_ADA = 6
PEER_HEADS = 8
PEER_KEYS = 128
PEER_TOPK = 16
EPS = 1e-6

ADA_ROWS = 16
CONV_HALO = 16
VMEM_LIMIT = 56 * 1024 * 1024

_STAIR = ((1, 8), (2, 5), (3, 4), (4, 3), (5, 2), (6, 2), (7, 2))
_N_CAND = 16 + 8 * len(_STAIR) + 8


def _gelu(x):
    return 0.5 * x * (1.0 + jnp.tanh(0.7978845608028654 * (x + 0.044715 * (x * x * x))))


def _sigmoid(x):
    return 1.0 / (1.0 + jnp.exp(-x))


def _rms(x, g):
    return x * lax.rsqrt(jnp.mean(x * x, axis=-1, keepdims=True) + EPS) * g


def _norm_mod(x, g, shift, scale):
    return _rms(x, g) * (1.0 + scale) + shift


def _split(x):
    hi = x.astype(BF16)
    lo = (x - hi.astype(F32)).astype(BF16)
    return hi, lo


def _dot(a, b):
    return jnp.dot(a, b, preferred_element_type=F32)


def _ada_kernel(c_ref, w_ref, b_ref, o_ref):
    cc = c_ref[...]
    s = cc * _sigmoid(cc)
    o_ref[0] = jnp.dot(s, w_ref[0], precision=lax.Precision.HIGHEST,
                       preferred_element_type=F32) + b_ref[0]


def _ada(cc, w_ada, b_ada):
    depth, _, n = w_ada.shape
    tn = 1536
    return pl.pallas_call(
        _ada_kernel,
        grid=(depth, n // tn),
        in_specs=[pl.BlockSpec((ADA_ROWS, D_MODEL), lambda l, j: (0, 0)),
                  pl.BlockSpec((1, D_MODEL, tn), lambda l, j: (l, 0, j)),
                  pl.BlockSpec((1, 1, tn), lambda l, j: (l, 0, j))],
        out_specs=pl.BlockSpec((1, ADA_ROWS, tn), lambda l, j: (l, 0, j)),
        out_shape=jax.ShapeDtypeStruct((depth, ADA_ROWS, n), F32),
        compiler_params=pltpu.CompilerParams(vmem_limit_bytes=VMEM_LIMIT),
        name="ada",
    )(cc, w_ada, b_ada.reshape(depth, 1, n))


def _even_kernel(x_ref, xp_ref, xn_ref, ada_ref, g1_ref, win_ref, wout_ref, cw_ref, vg_ref,
                 ws_ref, bsb_ref, o_ref, *, n_tiles):
    t = pl.program_id(1)
    tm = x_ref.shape[1]
    shift, scale, gate = ada_ref[0, 0:1, :], ada_ref[0, 1:2, :], ada_ref[0, 2:3, :]
    g1 = g1_ref[...]

    x = x_ref[0]
    h = _norm_mod(x, g1, shift, scale).astype(BF16)
    proj = _dot(h, win_ref[...])
    a_h, a_c, a_b = proj[:, 0:MIX_W], proj[:, MIX_W:2 * MIX_W], proj[:, 2 * MIX_W:3 * MIX_W]
    b_u, b_v = proj[:, 3 * MIX_W:4 * MIX_W], proj[:, 4 * MIX_W:5 * MIX_W]

    z = a_c * a_h

    def edge_z(ref, row):
        hh = _norm_mod(ref[0], g1, shift, scale).astype(BF16)
        pp = _dot(hh, win_ref[:, 0:2 * MIX_W])
        return (pp[:, 0:MIX_W] * pp[:, MIX_W:2 * MIX_W])[row:row + 1, :]

    zp = jnp.where(t > 0, edge_z(xp_ref, 7), 0.0)
    zn = jnp.where(t < n_tiles - 1, edge_z(xn_ref, 0), 0.0)
    row = lax.broadcasted_iota(jnp.int32, (tm, MIX_W), 0)
    z_m1 = jnp.where(row == 0, zp, pltpu.roll(z, 1, axis=0))
    z_p1 = jnp.where(row == tm - 1, zn, pltpu.roll(z, tm - 1, axis=0))
    ya = a_b * (cw_ref[0:1, :] * z_m1 + cw_ref[1:2, :] * z + cw_ref[2:3, :] * z_p1)

    u = _gelu(b_u)
    v = _rms(_gelu(b_v), vg_ref[...]).astype(BF16)
    lane = lax.broadcasted_iota(jnp.int32, (CHUNK, CHUNK), 1)
    group_w = MIX_W // B_GROUPS
    rows = []
    for n in range(tm // CHUNK):
        cols = []
        for j in range(MIX_W // CHUNK):
            vv = v[n * CHUNK:(n + 1) * CHUNK, j * CHUNK:(j + 1) * CHUNK]
            s0 = _dot(ws_ref[2 * j], vv)
            s1 = _dot(ws_ref[2 * j + 1], vv)
            cols.append(jnp.where(lane < group_w, s0, s1))
        rows.append(jnp.concatenate(cols, axis=1) + bsb_ref[...])
    s = jnp.concatenate(rows, axis=0)
    yb = u * s

    y = jnp.concatenate([ya, yb], axis=1).astype(BF16)
    o_ref[0] = x + gate * _dot(y, wout_ref[...])


def _even_mixer(x, ada, g1, win, wout, cw, vg, ws, bsb, *, tm):
    bsz, L, _ = x.shape
    n_tiles = L // tm
    hb = tm // 8
    const = lambda *shape: pl.BlockSpec(shape, lambda b, t: (0,) * len(shape))
    return pl.pallas_call(
        functools.partial(_even_kernel, n_tiles=n_tiles),
        grid=(bsz, n_tiles),
        in_specs=[pl.BlockSpec((1, tm, D_MODEL), lambda b, t: (b, t, 0)),
                  pl.BlockSpec((1, 8, D_MODEL), lambda b, t: (b, jnp.maximum(t * hb - 1, 0), 0)),
                  pl.BlockSpec((1, 8, D_MODEL),
                               lambda b, t: (b, jnp.minimum((t + 1) * hb, L // 8 - 1), 0)),
                  pl.BlockSpec((1, N_ADA, D_MODEL), lambda b, t: (b, 0, 0)),
                  const(1, D_MODEL), const(D_MODEL, IN_W), const(2 * MIX_W, D_MODEL),
                  const(A_CONV, MIX_W), const(1, MIX_W), const(B_GROUPS, CHUNK, CHUNK),
                  const(CHUNK, MIX_W)],
        out_specs=pl.BlockSpec((1, tm, D_MODEL), lambda b, t: (b, t, 0)),
        out_shape=jax.ShapeDtypeStruct(x.shape, F32),
        compiler_params=pltpu.CompilerParams(
            dimension_semantics=("parallel", "parallel"), vmem_limit_bytes=VMEM_LIMIT),
        name="even_mixer",
    )(x, x, x, ada, g1, win, wout, cw, vg, ws, bsb)


def _head_norm(t, gain, seg_ref):
    hi, lo = _split(t * t)
    ss = _dot(hi, seg_ref[...]) + _dot(lo, seg_ref[...])
    return t * lax.rsqrt(ss * (1.0 / D_HEAD_DIM) + EPS) * gain


def _odd_in_kernel(x_ref, xp_ref, xn_ref, ada_ref, g1_ref, win_ref, cw_ref, cb_ref, cg_ref,
                   qg_ref, kg_ref, cos_ref, sin_ref, seg_ref,
                   conf_ref, q_ref, kt_ref, v_ref, ybuf_ref, *, n_tiles):
    t = pl.program_id(1)
    tm = x_ref.shape[1]
    shift, scale = ada_ref[0, 0:1, :], ada_ref[0, 1:2, :]
    g1 = g1_ref[...]

    h = _norm_mod(x_ref[0], g1, shift, scale).astype(BF16)
    proj = _dot(h, win_ref[...])

    def glu_rows(ref):
        hh = _norm_mod(ref[0], g1, shift, scale).astype(BF16)
        pp = _dot(hh, win_ref[:, 0:2 * MIX_W])
        return pp[:, 0:MIX_W] * _sigmoid(pp[:, MIX_W:2 * MIX_W])

    ybuf_ref[0:CONV_HALO, :] = jnp.where(t > 0, glu_rows(xp_ref), 0.0)
    ybuf_ref[CONV_HALO:CONV_HALO + tm, :] = proj[:, 0:MIX_W] * _sigmoid(proj[:, MIX_W:2 * MIX_W])
    ybuf_ref[CONV_HALO + tm:2 * CONV_HALO + tm, :] = jnp.where(t < n_tiles - 1, glu_rows(xn_ref), 0.0)
    base = CONV_HALO - C_CONV // 2
    acc = jnp.zeros((tm, MIX_W), F32) + cb_ref[...]
    for k in range(C_CONV):
        acc = acc + cw_ref[k:k + 1, :] * ybuf_ref[base + k:base + k + tm, :]
    yn = _rms(acc, cg_ref[...])
    conf_ref[0] = (yn * _sigmoid(yn)).astype(BF16)

    lane = lax.broadcasted_iota(jnp.int32, (tm, MIX_W), 1)
    even_lane = (lane % 2) == 0

    def rope(tn):
        partner = jnp.where(even_lane, pltpu.roll(tn, MIX_W - 1, axis=1), pltpu.roll(tn, 1, axis=1))
        return tn * cos_ref[...] + partner * sin_ref[...]

    q = rope(_head_norm(proj[:, 2 * MIX_W:3 * MIX_W], qg_ref[...], seg_ref))
    k = rope(_head_norm(proj[:, 3 * MIX_W:4 * MIX_W], kg_ref[...], seg_ref))
    q_ref[0] = (q * (D_HEAD_DIM ** -0.5)).astype(BF16)
    kt_ref[0] = k.T.astype(BF16)
    v_ref[0] = proj[:, 4 * MIX_W:5 * MIX_W].astype(BF16)


def _odd_in(x, ada, g1, win, cw, cb, cg, qg, kg, cos_t, sin_t, seg, *, tm):
    bsz, L, _ = x.shape
    n_tiles = L // tm
    hb = tm // CONV_HALO
    const = lambda *shape: pl.BlockSpec(shape, lambda b, t: (0,) * len(shape))
    tok = lambda w: pl.BlockSpec((1, tm, w), lambda b, t: (b, t, 0))
    return pl.pallas_call(
        functools.partial(_odd_in_kernel, n_tiles=n_tiles),
        grid=(bsz, n_tiles),
        in_specs=[tok(D_MODEL),
                  pl.BlockSpec((1, CONV_HALO, D_MODEL),
                               lambda b, t: (b, jnp.maximum(t * hb - 1, 0), 0)),
                  pl.BlockSpec((1, CONV_HALO, D_MODEL),
                               lambda b, t: (b, jnp.minimum((t + 1) * hb, L // CONV_HALO - 1), 0)),
                  pl.BlockSpec((1, N_ADA, D_MODEL), lambda b, t: (b, 0, 0)),
                  const(1, D_MODEL), const(D_MODEL, IN_W), const(C_CONV, MIX_W),
                  const(1, MIX_W), const(1, MIX_W), const(1, MIX_W), const(1, MIX_W),
                  pl.BlockSpec((tm, MIX_W), lambda b, t: (t, 0)),
                  pl.BlockSpec((tm, MIX_W), lambda b, t: (t, 0)),
                  const(MIX_W, MIX_W)],
        out_specs=[tok(MIX_W), tok(MIX_W),
                   pl.BlockSpec((1, MIX_W, tm), lambda b, t: (b, 0, t)),
                   tok(MIX_W)],
        out_shape=[jax.ShapeDtypeStruct((bsz, L, MIX_W), BF16),
                   jax.ShapeDtypeStruct((bsz, L, MIX_W), BF16),
                   jax.ShapeDtypeStruct((bsz, MIX_W, L), BF16),
                   jax.ShapeDtypeStruct((bsz, L, MIX_W), BF16)],
        scratch_shapes=[pltpu.VMEM((tm + 2 * CONV_HALO, MIX_W), F32)],
        compiler_params=pltpu.CompilerParams(
            dimension_semantics=("parallel", "parallel"), vmem_limit_bytes=VMEM_LIMIT),
        name="odd_in",
    )(x, x, x, ada, g1, win, cw, cb, cg, qg, kg, cos_t, sin_t, seg)


def _ctx_kv_kernel(x_ref, ada_ref, g1_ref, wkv_ref, kg_ref, seg_ref, kt_ref, v_ref):
    shift, scale = ada_ref[0, 0:1, :], ada_ref[0, 1:2, :]
    h = _norm_mod(x_ref[0], g1_ref[...], shift, scale).astype(BF16)
    proj = _dot(h, wkv_ref[...])
    k = _head_norm(proj[:, 0:MIX_W], kg_ref[...], seg_ref)
    kt_ref[0] = k.T.astype(BF16)
    v_ref[0] = proj[:, MIX_W:2 * MIX_W].astype(BF16)


def _ctx_kv(ctx, ada, g1, wkv, kg, seg):
    bsz, L, _ = ctx.shape
    const = lambda *shape: pl.BlockSpec(shape, lambda b: (0,) * len(shape))
    return pl.pallas_call(
        _ctx_kv_kernel,
        grid=(bsz,),
        in_specs=[pl.BlockSpec((1, L, D_MODEL), lambda b: (b, 0, 0)),
                  pl.BlockSpec((1, N_ADA, D_MODEL), lambda b: (b, 0, 0)),
                  const(1, D_MODEL), const(D_MODEL, 2 * MIX_W), const(1, MIX_W),
                  const(MIX_W, MIX_W)],
        out_specs=[pl.BlockSpec((1, MIX_W, L), lambda b: (b, 0, 0)),
                   pl.BlockSpec((1, L, MIX_W), lambda b: (b, 0, 0))],
        out_shape=[jax.ShapeDtypeStruct((bsz, MIX_W, L), BF16),
                   jax.ShapeDtypeStruct((bsz, L, MIX_W), BF16)],
        compiler_params=pltpu.CompilerParams(
            dimension_semantics=("parallel",), vmem_limit_bytes=VMEM_LIMIT),
        name="ctx_kv",
    )(ctx, ada, g1, wkv, kg, seg)


def _attn_kernel(q_ref, ktc_ref, ktx_ref, vc_ref, vx_ref, conf_ref, x_ref, ada_ref, wout_ref,
                 lam_ref, sg_ref, o_ref, *, lam_init):
    tq = q_ref.shape[1]
    vw = 2 * D_HEAD_DIM
    lam_p = lam_ref[...]
    lam = (jnp.exp(jnp.sum(lam_p[0:1] * lam_p[1:2], axis=-1, keepdims=True))
           - jnp.exp(jnp.sum(lam_p[2:3] * lam_p[3:4], axis=-1, keepdims=True)) + lam_init)
    first_map = lax.broadcasted_iota(jnp.int32, (tq, vw), 1) < D_HEAD_DIM
    outs = []
    for hd in range(D_HEADS):
        sl = slice(hd * vw, (hd + 1) * vw)
        qh = q_ref[0, :, sl]
        probs = []
        for m in range(2):
            qm = jnp.where(first_map if m == 0 else jnp.logical_not(first_map), qh, jnp.zeros_like(qh))
            sc = _dot(qm, ktc_ref[0, sl, :])
            sx = _dot(qm, ktx_ref[0, sl, :])
            mx = jnp.maximum(jnp.max(sc, axis=-1, keepdims=True), jnp.max(sx, axis=-1, keepdims=True))
            pc, px = jnp.exp(sc - mx), jnp.exp(sx - mx)
            inv = 1.0 / (jnp.sum(pc, axis=-1, keepdims=True) + jnp.sum(px, axis=-1, keepdims=True))
            probs.append((pc * inv, px * inv))
        ac = (probs[0][0] - lam * probs[1][0]).astype(BF16)
        ax = (probs[0][1] - lam * probs[1][1]).astype(BF16)
        o = _dot(ac, vc_ref[0, :, sl]) + _dot(ax, vx_ref[0, :, sl])
        outs.append(_rms(o, sg_ref[...]) * (1.0 - lam_init))
    y = jnp.concatenate([conf_ref[0]] + [o.astype(BF16) for o in outs], axis=1)
    o_ref[0] = x_ref[0] + ada_ref[0, 2:3, :] * _dot(y, wout_ref[...])


def _attn(q, ktc, ktx, vc, vx, conf, x, ada, wout, lam_p, sg, *, lam_init, tq):
    bsz, L, _ = x.shape
    lc = vc.shape[1]
    const = lambda *shape: pl.BlockSpec(shape, lambda b, t: (0,) * len(shape))
    tok = lambda w: pl.BlockSpec((1, tq, w), lambda b, t: (b, t, 0))
    return pl.pallas_call(
        functools.partial(_attn_kernel, lam_init=lam_init),
        grid=(bsz, L // tq),
        in_specs=[tok(MIX_W),
                  pl.BlockSpec((1, MIX_W, lc), lambda b, t: (b, 0, 0)),
                  pl.BlockSpec((1, MIX_W, L), lambda b, t: (b, 0, 0)),
                  pl.BlockSpec((1, lc, MIX_W), lambda b, t: (b, 0, 0)),
                  pl.BlockSpec((1, L, MIX_W), lambda b, t: (b, 0, 0)),
                  tok(MIX_W), tok(D_MODEL),
                  pl.BlockSpec((1, N_ADA, D_MODEL), lambda b, t: (b, 0, 0)),
                  const(2 * MIX_W, D_MODEL), const(4, D_HEAD_DIM), const(1, 2 * D_HEAD_DIM)],
        out_specs=tok(D_MODEL),
        out_shape=jax.ShapeDtypeStruct(x.shape, F32),
        compiler_params=pltpu.CompilerParams(
            dimension_semantics=("parallel", "parallel"), vmem_limit_bytes=VMEM_LIMIT),
        name="diff_attn",
    )(q, ktc, ktx, vc, vx, conf, x, ada, wout, lam_p, sg)


def _top16(s, iota):
    work = s
    rank = jnp.full(s.shape, float(PEER_TOPK), F32)
    vals = []
    for r in range(PEER_TOPK):
        m = jnp.max(work, axis=0, keepdims=True)
        first = jnp.min(jnp.where(work == m, iota, 1e9), axis=0, keepdims=True)
        sel = iota == first
        vals.append(m)
        rank = jnp.where(sel, float(r), rank)
        work = jnp.where(sel, -jnp.inf, work)
    return vals, rank


def _peer_query_kernel(x_ref, ada_ref, g2_ref, wqh_ref, wql_ref, skh_ref, skl_ref,
                       h2t_ref, r2_ref, a2_ref, n1_ref, b1_ref, qt_ref):
    tb = x_ref.shape[0]
    shift, scale = ada_ref[0, 3:4, :], ada_ref[0, 4:5, :]
    h2 = _norm_mod(x_ref[...], g2_ref[...], shift, scale)
    h2t = h2.T
    hh, hl = _split(h2t)
    h2t_ref[...] = hh
    qt_ref[...] = _dot(wqh_ref[...], hh) + (_dot(wqh_ref[...], hl) + _dot(wql_ref[...], hh))

    iota_k = lax.broadcasted_iota(jnp.int32, (PEER_KEYS, tb), 0).astype(F32)
    iota_c = lax.broadcasted_iota(jnp.int32, (_N_CAND, tb), 0)
    valid = iota_c < 24
    for g, (_, nv) in enumerate(_STAIR[1:], start=1):
        valid = jnp.logical_or(valid, jnp.logical_and(iota_c >= 16 + 8 * g, iota_c < 16 + 8 * g + nv))
    valid = jnp.logical_or(valid, iota_c >= _N_CAND - 8)
    iota_c = iota_c.astype(F32)

    def head(hd, carry):
        def scores(p):
            qp = qt_ref[pl.ds(pl.multiple_of((2 * hd + p) * PEER_KEYS, PEER_KEYS), PEER_KEYS), :]
            qh, ql = _split(qp)
            kh, kl = skh_ref[2 * hd + p], skl_ref[2 * hd + p]
            return _dot(kh, qh) + (_dot(kh, ql) + _dot(kl, qh))

        s1, s2 = scores(0), scores(1)
        v1, rank1 = _top16(s1, iota_k)
        v2, rank2 = _top16(s2, iota_k)
        top1 = jnp.concatenate(v1, axis=0)
        top2 = jnp.concatenate(v2, axis=0)
        groups = [top2 + v1[0]] + [top2[0:8] + v1[k1] for k1, _ in _STAIR] + [top1[8:16] + v2[0]]
        cand = jnp.where(valid, jnp.concatenate(groups, axis=0), -jnp.inf)
        taken = jnp.zeros((_N_CAND, tb), F32)
        z = jnp.zeros((1, tb), F32)
        best = None
        for r in range(PEER_TOPK):
            m = jnp.max(cand, axis=0, keepdims=True)
            first = jnp.min(jnp.where(cand == m, iota_c, 1e9), axis=0, keepdims=True)
            sel = iota_c == first
            best = m if best is None else best
            z = z + jnp.exp(m - best)
            taken = jnp.where(sel, 1.0, taken)
            cand = jnp.where(sel, -jnp.inf, cand)
        counts = [jnp.sum(taken[0:16], axis=0, keepdims=True)]
        counts += [jnp.sum(taken[16 + 8 * g:24 + 8 * g], axis=0, keepdims=True)
                   for g in range(len(_STAIR))]
        counts += [taken[_N_CAND - 8 + i:_N_CAND - 7 + i] for i in range(8)]
        n1 = jnp.zeros((PEER_KEYS, tb), F32)
        for k1 in range(PEER_TOPK):
            n1 = jnp.where(rank1 == float(k1), counts[k1], n1)
        r2_ref[hd] = rank2
        a2_ref[hd] = jnp.exp(s2 - v2[0])
        n1_ref[hd] = n1
        b1_ref[hd] = jnp.exp(s1 - v1[0]) * (1.0 / z)
        return carry

    lax.fori_loop(0, PEER_HEADS, head, 0)


def _peer_query(x, ada, g2, wqh, wql, skh, skl, *, tb, rows_per_ada):
    T = x.shape[0]
    qd = wqh.shape[0]
    const = lambda *shape: pl.BlockSpec(shape, lambda j: (0,) * len(shape))
    tab = pl.BlockSpec((PEER_HEADS, PEER_KEYS, tb), lambda j: (0, 0, j))
    tab_shape = jax.ShapeDtypeStruct((PEER_HEADS, PEER_KEYS, T), F32)
    return pl.pallas_call(
        _peer_query_kernel,
        grid=(T // tb,),
        in_specs=[pl.BlockSpec((tb, D_MODEL), lambda j: (j, 0)),
                  pl.BlockSpec((1, N_ADA, D_MODEL), lambda j: ((j * tb) // rows_per_ada, 0, 0)),
                  const(1, D_MODEL), const(qd, D_MODEL), const(qd, D_MODEL),
                  const(2 * PEER_HEADS, PEER_KEYS, PEER_KEYS),
                  const(2 * PEER_HEADS, PEER_KEYS, PEER_KEYS)],
        out_specs=[pl.BlockSpec((D_MODEL, tb), lambda j: (0, j)), tab, tab, tab, tab],
        out_shape=[jax.ShapeDtypeStruct((D_MODEL, T), BF16), tab_shape, tab_shape, tab_shape,
                   tab_shape],
        scratch_shapes=[pltpu.VMEM((qd, tb), F32)],
        compiler_params=pltpu.CompilerParams(
            dimension_semantics=("parallel",), vmem_limit_bytes=VMEM_LIMIT),
        name="peer_query",
    )(x, ada, g2, wqh, wql, skh, skl)


def _peer_dense_kernel(h2t_ref, u_ref, vt_ref, r2_ref, a2_ref, n1_ref, b1_ref, x_ref, ada_ref,
                       o_ref, acc_ref, p_ref, *, n_et):
    i = pl.program_id(1)
    te, tb = p_ref.shape
    lanes = 128

    @pl.when(i == 0)
    def _():
        acc_ref[...] = jnp.zeros_like(acc_ref)

    act = _gelu(_dot(u_ref[...], h2t_ref[...]))
    for j in range(te // PEER_KEYS):
        for c in range(tb // lanes):
            cs = slice(c * lanes, (c + 1) * lanes)
            g = jnp.zeros((PEER_KEYS, lanes), F32)
            for hd in range(PEER_HEADS):
                cnt = n1_ref[hd, j:j + 1, cs]
                wgt = b1_ref[hd, j:j + 1, cs]
                g = g + jnp.where(r2_ref[hd, :, cs] < cnt, a2_ref[hd, :, cs], 0.0) * wgt
            rs = slice(j * PEER_KEYS, (j + 1) * PEER_KEYS)
            p_ref[rs, cs] = (g * act[rs, cs]).astype(BF16)
    acc_ref[...] += _dot(vt_ref[...], p_ref[...])

    @pl.when(i == n_et - 1)
    def _():
        o_ref[...] = x_ref[...] + ada_ref[0, 5:6, :] * acc_ref[...].T


def _peer_dense(x, h2t, tabs, u, vt, ada, *, tb, te, rows_per_ada):
    T = x.shape[0]
    n_exp = u.shape[0]
    n_et = n_exp // te
    tab = pl.BlockSpec((PEER_HEADS, PEER_KEYS, tb), lambda j, i: (0, 0, j))
    tab1 = pl.BlockSpec((PEER_HEADS, te // PEER_KEYS, tb), lambda j, i: (0, i, j))
    return pl.pallas_call(
        functools.partial(_peer_dense_kernel, n_et=n_et),
        grid=(T // tb, n_et),
        in_specs=[pl.BlockSpec((D_MODEL, tb), lambda j, i: (0, j)),
                  pl.BlockSpec((te, D_MODEL), lambda j, i: (i, 0)),
                  pl.BlockSpec((D_MODEL, te), lambda j, i: (0, i)),
                  tab, tab, tab1, tab1,
                  pl.BlockSpec((tb, D_MODEL), lambda j, i: (j, 0)),
                  pl.BlockSpec((1, N_ADA, D_MODEL), lambda j, i: ((j * tb) // rows_per_ada, 0, 0))],
        out_specs=pl.BlockSpec((tb, D_MODEL), lambda j, i: (j, 0)),
        out_shape=jax.ShapeDtypeStruct(x.shape, F32),
        scratch_shapes=[pltpu.VMEM((D_MODEL, tb), F32), pltpu.VMEM((te, tb), BF16)],
        compiler_params=pltpu.CompilerParams(
            dimension_semantics=("parallel", "arbitrary"), vmem_limit_bytes=VMEM_LIMIT),
        name="peer_dense",
    )(h2t, u, vt, *tabs, x, ada)


def _peer(x, ada, g2, wq, subkeys, u_tab, v_tab, *, rows_per_ada):
    wqh, wql = _split(wq.T)
    skh, skl = _split(subkeys.reshape(2 * PEER_HEADS, PEER_KEYS, -1))
    h2t, *tabs = _peer_query(x, ada, g2, wqh, wql, skh, skl, tb=256, rows_per_ada=rows_per_ada)
    return _peer_dense(x, h2t, tabs, u_tab.astype(BF16), v_tab.T.astype(BF16), ada,
                       tb=512, te=1024, rows_per_ada=rows_per_ada)


def _rope_tables(L):
    pos = jnp.arange(L, dtype=jnp.int32)
    r = (pos // GRID_W).astype(F32)
    col = (pos % GRID_W).astype(F32)
    n_freq = D_HEAD_DIM // 4
    inv = ROPE_THETA ** (-jnp.arange(n_freq, dtype=F32) / n_freq)
    ang = jnp.concatenate([r[:, None] * inv, col[:, None] * inv], axis=-1)
    cos = jnp.repeat(jnp.cos(ang), 2, axis=-1)
    sin = jnp.repeat(jnp.sin(ang), 2, axis=-1) * jnp.tile(jnp.array([-1.0, 1.0], F32), D_HEAD_DIM // 2)
    reps = MIX_W // D_HEAD_DIM
    return jnp.tile(cos, (1, reps)), jnp.tile(sin, (1, reps))


def kernel(x, c, ctx, c_ctx, w_ada, b_ada, norm1_g, norm2_g, w_in, w_out, a_conv_w, b_vnorm_g,
           b_spatial_w, b_spatial_b, c_conv_w, c_conv_b, c_norm_g, d_qk_norm_g, d_lambda,
           d_subln_g, peer_wq, peer_subkeys, peer_u, peer_v):
    bsz, L, _ = x.shape
    lc = ctx.shape[1]
    depth = w_ada.shape[0]
    assert depth == 2 and bsz + 1 <= ADA_ROWS

    cc = jnp.zeros((ADA_ROWS, D_MODEL), F32).at[:bsz].set(c).at[bsz].set(c_ctx)
    ada = _ada(cc, w_ada, b_ada).reshape(depth, ADA_ROWS, N_ADA, D_MODEL)
    row = lambda v: v.reshape(1, -1)

    ada_x, ada_c = ada[0, :bsz], ada[0, bsz:bsz + 1]
    bsb = jnp.repeat(b_spatial_b[0].T, MIX_W // B_GROUPS, axis=1)
    even = functools.partial(
        _even_mixer, g1=row(norm1_g[0]), win=w_in[0].astype(BF16), wout=w_out[0].astype(BF16),
        cw=a_conv_w[0], vg=row(b_vnorm_g[0]), ws=b_spatial_w[0].astype(BF16), bsb=bsb)
    peer0 = functools.partial(_peer, g2=row(norm2_g[0]), wq=peer_wq[0], subkeys=peer_subkeys[0],
                              u_tab=peer_u[0], v_tab=peer_v[0])
    x = even(x, ada_x, tm=512)
    x = peer0(x.reshape(bsz * L, D_MODEL), ada_x, rows_per_ada=L).reshape(bsz, L, D_MODEL)
    ctx = even(ctx, jnp.broadcast_to(ada_c, (bsz, N_ADA, D_MODEL)), tm=lc)
    ctx = peer0(ctx.reshape(bsz * lc, D_MODEL), ada_c, rows_per_ada=bsz * lc).reshape(bsz, lc, D_MODEL)

    ada_x, ada_c = ada[1, :bsz], ada[1, bsz:bsz + 1]
    lam_init = 0.8 - 0.6 * math.exp(-0.3 * 1)
    win = w_in[1].astype(BF16)
    cos_t, sin_t = _rope_tables(L)
    seg_id = jnp.arange(MIX_W, dtype=jnp.int32) // D_HEAD_DIM
    seg = (seg_id[:, None] == seg_id[None, :]).astype(BF16)
    reps = MIX_W // D_HEAD_DIM
    qg, kg = row(jnp.tile(d_qk_norm_g[0, 0], reps)), row(jnp.tile(d_qk_norm_g[0, 1], reps))
    conf, q, ktx, vx = _odd_in(x, ada_x, row(norm1_g[1]), win, c_conv_w[0], row(c_conv_b[0]),
                               row(c_norm_g[0]), qg, kg, cos_t, sin_t, seg, tm=512)
    ktc, vc = _ctx_kv(ctx, jnp.broadcast_to(ada_c, (bsz, N_ADA, D_MODEL)), row(norm1_g[1]),
                      win[:, 3 * MIX_W:], kg, seg)
    x = _attn(q, ktc, ktx, vc, vx, conf, x, ada_x, w_out[1].astype(BF16), d_lambda[0],
              row(d_subln_g[0]), lam_init=lam_init, tq=256)
    x = _peer(x.reshape(bsz * L, D_MODEL), ada_x, row(norm2_g[1]), peer_wq[1], peer_subkeys[1],
              peer_u[1], peer_v[1], rows_per_ada=L)
    return x.reshape(bsz, L, D_MODEL)
```

```python
import functools
import math

import jax
import jax.numpy as jnp
from jax import lax
from jax.experimental import pallas as pl
from jax.experimental.pallas import tpu as pltpu

F32 = jnp.float32
BF16 = jnp.bfloat16

D_MODEL = 1024
MIX_W = 512
IN_W = 5 * MIX_W
GRID_W = 64
CHUNK = 128
B_GROUPS = 8
A_CONV = 3
C_CONV = 31
D_HEADS = 4
D_HEAD_DIM = 64
ROPE_THETA = 10000.0
N_ADA = 6
PEER_HEADS = 8
PEER_KEYS = 128
PEER_TOPK = 16
EPS = 1e-6

SUBLANES, LANES = 8, 128
BF16_ROWS = 2 * SUBLANES

ADA_ROWS = 16
CONV_HALO = 16
VMEM_LIMIT = 56 * 1024 * 1024

_STAIR = ((1, 8), (2, 5), (3, 4), (4, 3), (5, 2), (6, 2), (7, 2))
_N_CAND = 16 + 8 * len(_STAIR) + 8


_GELU_C0 = -2.0 * math.sqrt(2.0 / math.pi) * math.log2(math.e)
_GELU_C1 = 0.044715 * _GELU_C0


def _gelu(x):
    return x / (1.0 + jnp.exp2(x * (_GELU_C0 + _GELU_C1 * (x * x))))


def _sigmoid(x):
    return 1.0 / (1.0 + jnp.exp(-x))


def _rms(x, g):
    return x * lax.rsqrt(jnp.mean(x * x, axis=-1, keepdims=True) + EPS) * g


def _norm_mod(x, g, shift, scale):
    return _rms(x, g) * (1.0 + scale) + shift


def _split(x):
    hi = x.astype(BF16)
    lo = (x - hi.astype(F32)).astype(BF16)
    return hi, lo


def _dot(a, b):
    return jnp.dot(a, b, preferred_element_type=F32)


def _ada_kernel(c_ref, w_ref, b_ref, o_ref):
    cc = c_ref[...]
    s = cc * _sigmoid(cc)
    o_ref[0] = jnp.dot(s, w_ref[0], precision=lax.Precision.HIGHEST,
                       preferred_element_type=F32) + b_ref[0]


def _ada(cc, w_ada, b_ada):
    depth, _, n = w_ada.shape
    tn = 1536
    return pl.pallas_call(
        _ada_kernel,
        grid=(depth, n // tn),
        in_specs=[pl.BlockSpec((ADA_ROWS, D_MODEL), lambda l, j: (0, 0)),
                  pl.BlockSpec((1, D_MODEL, tn), lambda l, j: (l, 0, j)),
                  pl.BlockSpec((1, 1, tn), lambda l, j: (l, 0, j))],
        out_specs=pl.BlockSpec((1, ADA_ROWS, tn), lambda l, j: (l, 0, j)),
        out_shape=jax.ShapeDtypeStruct((depth, ADA_ROWS, n), F32),
        compiler_params=pltpu.CompilerParams(vmem_limit_bytes=VMEM_LIMIT),
        name="ada",
    )(cc, w_ada, b_ada.reshape(depth, 1, n))


def _even_kernel(x_ref, xp_ref, xn_ref, ada_ref, g1_ref, win_ref, wout_ref, cw_ref, vg_ref,
                 ws_ref, bsb_ref, o_ref, *, n_tiles):
    t = pl.program_id(1)
    tm = x_ref.shape[1]
    shift, scale, gate = ada_ref[0, 0:1, :], ada_ref[0, 1:2, :], ada_ref[0, 2:3, :]
    g1 = g1_ref[...]

    x = x_ref[0]
    h = _norm_mod(x, g1, shift, scale).astype(BF16)
    proj = _dot(h, win_ref[...])
    a_h, a_c, a_b = proj[:, 0:MIX_W], proj[:, MIX_W:2 * MIX_W], proj[:, 2 * MIX_W:3 * MIX_W]
    b_u, b_v = proj[:, 3 * MIX_W:4 * MIX_W], proj[:, 4 * MIX_W:5 * MIX_W]

    z = a_c * a_h

    def edge_z(ref, row):
        hh = _norm_mod(ref[0], g1, shift, scale).astype(BF16)
        pp = _dot(hh, win_ref[:, 0:2 * MIX_W])
        return (pp[:, 0:MIX_W] * pp[:, MIX_W:2 * MIX_W])[row:row + 1, :]

    zp = jnp.where(t > 0, edge_z(xp_ref, 7), 0.0)
    zn = jnp.where(t < n_tiles - 1, edge_z(xn_ref, 0), 0.0)
    row = lax.broadcasted_iota(jnp.int32, (tm, MIX_W), 0)
    z_m1 = jnp.where(row == 0, zp, pltpu.roll(z, 1, axis=0))
    z_p1 = jnp.where(row == tm - 1, zn, pltpu.roll(z, tm - 1, axis=0))
    ya = a_b * (cw_ref[0:1, :] * z_m1 + cw_ref[1:2, :] * z + cw_ref[2:3, :] * z_p1)

    u = _gelu(b_u)
    v = _rms(_gelu(b_v), vg_ref[...]).astype(BF16)
    lane = lax.broadcasted_iota(jnp.int32, (CHUNK, CHUNK), 1)
    group_w = MIX_W // B_GROUPS
    rows = []
    for n in range(tm // CHUNK):
        cols = []
        for j in range(MIX_W // CHUNK):
            vv = v[n * CHUNK:(n + 1) * CHUNK, j * CHUNK:(j + 1) * CHUNK]
            s0 = _dot(ws_ref[2 * j], vv)
            s1 = _dot(ws_ref[2 * j + 1], vv)
            cols.append(jnp.where(lane < group_w, s0, s1))
        rows.append(jnp.concatenate(cols, axis=1) + bsb_ref[...])
    s = jnp.concatenate(rows, axis=0)
    yb = u * s

    y = jnp.concatenate([ya, yb], axis=1).astype(BF16)
    o_ref[0] = x + gate * _dot(y, wout_ref[...])


def _even_mixer(x, ada, g1, win, wout, cw, vg, ws, bsb, *, tm):
    bsz, L, _ = x.shape
    n_tiles = L // tm
    hb = tm // 8
    const = lambda *shape: pl.BlockSpec(shape, lambda b, t: (0,) * len(shape))
    return pl.pallas_call(
        functools.partial(_even_kernel, n_tiles=n_tiles),
        grid=(bsz, n_tiles),
        in_specs=[pl.BlockSpec((1, tm, D_MODEL), lambda b, t: (b, t, 0)),
                  pl.BlockSpec((1, 8, D_MODEL), lambda b, t: (b, jnp.maximum(t * hb - 1, 0), 0)),
                  pl.BlockSpec((1, 8, D_MODEL),
                               lambda b, t: (b, jnp.minimum((t + 1) * hb, L // 8 - 1), 0)),
                  pl.BlockSpec((1, N_ADA, D_MODEL), lambda b, t: (b, 0, 0)),
                  const(1, D_MODEL), const(D_MODEL, IN_W), const(2 * MIX_W, D_MODEL),
                  const(A_CONV, MIX_W), const(1, MIX_W), const(B_GROUPS, CHUNK, CHUNK),
                  const(CHUNK, MIX_W)],
        out_specs=pl.BlockSpec((1, tm, D_MODEL), lambda b, t: (b, t, 0)),
        out_shape=jax.ShapeDtypeStruct(x.shape, F32),
        compiler_params=pltpu.CompilerParams(
            dimension_semantics=("parallel", "parallel"), vmem_limit_bytes=VMEM_LIMIT),
        name="even_mixer",
    )(x, x, x, ada, g1, win, wout, cw, vg, ws, bsb)


def _head_norm(t, gain, seg_ref):
    hi, lo = _split(t * t)
    ss = _dot(hi, seg_ref[...]) + _dot(lo, seg_ref[...])
    return t * lax.rsqrt(ss * (1.0 / D_HEAD_DIM) + EPS) * gain


def _odd_in_kernel(x_ref, xp_ref, xn_ref, ada_ref, g1_ref, win_ref, cw_ref, cb_ref, cg_ref,
                   qg_ref, kg_ref, cos_ref, sin_ref, seg_ref,
                   conf_ref, q_ref, kt_ref, v_ref, ybuf_ref, *, n_tiles):
    t = pl.program_id(1)
    tm = x_ref.shape[1]
    shift, scale = ada_ref[0, 0:1, :], ada_ref[0, 1:2, :]
    g1 = g1_ref[...]

    h = _norm_mod(x_ref[0], g1, shift, scale).astype(BF16)
    proj = _dot(h, win_ref[...])

    def glu_rows(ref):
        hh = _norm_mod(ref[0], g1, shift, scale).astype(BF16)
        pp = _dot(hh, win_ref[:, 0:2 * MIX_W])
        return pp[:, 0:MIX_W] * _sigmoid(pp[:, MIX_W:2 * MIX_W])

    ybuf_ref[0:CONV_HALO, :] = jnp.where(t > 0, glu_rows(xp_ref), 0.0)
    ybuf_ref[CONV_HALO:CONV_HALO + tm, :] = proj[:, 0:MIX_W] * _sigmoid(proj[:, MIX_W:2 * MIX_W])
    ybuf_ref[CONV_HALO + tm:2 * CONV_HALO + tm, :] = jnp.where(t < n_tiles - 1, glu_rows(xn_ref), 0.0)
    base = CONV_HALO - C_CONV // 2
    acc = jnp.zeros((tm, MIX_W), F32) + cb_ref[...]
    for k in range(C_CONV):
        acc = acc + cw_ref[k:k + 1, :] * ybuf_ref[base + k:base + k + tm, :]
    yn = _rms(acc, cg_ref[...])
    conf_ref[0] = (yn * _sigmoid(yn)).astype(BF16)

    lane = lax.broadcasted_iota(jnp.int32, (tm, MIX_W), 1)
    even_lane = (lane % 2) == 0

    def rope(tn):
        partner = jnp.where(even_lane, pltpu.roll(tn, MIX_W - 1, axis=1), pltpu.roll(tn, 1, axis=1))
        return tn * cos_ref[...] + partner * sin_ref[...]

    q = rope(_head_norm(proj[:, 2 * MIX_W:3 * MIX_W], qg_ref[...], seg_ref))
    k = rope(_head_norm(proj[:, 3 * MIX_W:4 * MIX_W], kg_ref[...], seg_ref))
    q_ref[0] = (q * (D_HEAD_DIM ** -0.5)).astype(BF16)
    kt_ref[0] = k.T.astype(BF16)
    v_ref[0] = proj[:, 4 * MIX_W:5 * MIX_W].astype(BF16)


def _odd_in(x, ada, g1, win, cw, cb, cg, qg, kg, cos_t, sin_t, seg, *, tm):
    bsz, L, _ = x.shape
    n_tiles = L // tm
    hb = tm // CONV_HALO
    const = lambda *shape: pl.BlockSpec(shape, lambda b, t: (0,) * len(shape))
    tok = lambda w: pl.BlockSpec((1, tm, w), lambda b, t: (b, t, 0))
    return pl.pallas_call(
        functools.partial(_odd_in_kernel, n_tiles=n_tiles),
        grid=(bsz, n_tiles),
        in_specs=[tok(D_MODEL),
                  pl.BlockSpec((1, CONV_HALO, D_MODEL),
                               lambda b, t: (b, jnp.maximum(t * hb - 1, 0), 0)),
                  pl.BlockSpec((1, CONV_HALO, D_MODEL),
                               lambda b, t: (b, jnp.minimum((t + 1) * hb, L // CONV_HALO - 1), 0)),
                  pl.BlockSpec((1, N_ADA, D_MODEL), lambda b, t: (b, 0, 0)),
                  const(1, D_MODEL), const(D_MODEL, IN_W), const(C_CONV, MIX_W),
                  const(1, MIX_W), const(1, MIX_W), const(1, MIX_W), const(1, MIX_W),
                  pl.BlockSpec((tm, MIX_W), lambda b, t: (t, 0)),
                  pl.BlockSpec((tm, MIX_W), lambda b, t: (t, 0)),
                  const(MIX_W, MIX_W)],
        out_specs=[tok(MIX_W), tok(MIX_W),
                   pl.BlockSpec((1, MIX_W, tm), lambda b, t: (b, 0, t)),
                   tok(MIX_W)],
        out_shape=[jax.ShapeDtypeStruct((bsz, L, MIX_W), BF16),
                   jax.ShapeDtypeStruct((bsz, L, MIX_W), BF16),
                   jax.ShapeDtypeStruct((bsz, MIX_W, L), BF16),
                   jax.ShapeDtypeStruct((bsz, L, MIX_W), BF16)],
        scratch_shapes=[pltpu.VMEM((tm + 2 * CONV_HALO, MIX_W), F32)],
        compiler_params=pltpu.CompilerParams(
            dimension_semantics=("parallel", "parallel"), vmem_limit_bytes=VMEM_LIMIT),
        name="odd_in",
    )(x, x, x, ada, g1, win, cw, cb, cg, qg, kg, cos_t, sin_t, seg)


def _ctx_kv_kernel(x_ref, ada_ref, g1_ref, wkv_ref, kg_ref, seg_ref, kt_ref, v_ref):
    shift, scale = ada_ref[0, 0:1, :], ada_ref[0, 1:2, :]
    h = _norm_mod(x_ref[0], g1_ref[...], shift, scale).astype(BF16)
    proj = _dot(h, wkv_ref[...])
    k = _head_norm(proj[:, 0:MIX_W], kg_ref[...], seg_ref)
    kt_ref[0] = k.T.astype(BF16)
    v_ref[0] = proj[:, MIX_W:2 * MIX_W].astype(BF16)


def _ctx_kv(ctx, ada, g1, wkv, kg, seg):
    bsz, L, _ = ctx.shape
    const = lambda *shape: pl.BlockSpec(shape, lambda b: (0,) * len(shape))
    return pl.pallas_call(
        _ctx_kv_kernel,
        grid=(bsz,),
        in_specs=[pl.BlockSpec((1, L, D_MODEL), lambda b: (b, 0, 0)),
                  pl.BlockSpec((1, N_ADA, D_MODEL), lambda b: (b, 0, 0)),
                  const(1, D_MODEL), const(D_MODEL, 2 * MIX_W), const(1, MIX_W),
                  const(MIX_W, MIX_W)],
        out_specs=[pl.BlockSpec((1, MIX_W, L), lambda b: (b, 0, 0)),
                   pl.BlockSpec((1, L, MIX_W), lambda b: (b, 0, 0))],
        out_shape=[jax.ShapeDtypeStruct((bsz, MIX_W, L), BF16),
                   jax.ShapeDtypeStruct((bsz, L, MIX_W), BF16)],
        compiler_params=pltpu.CompilerParams(
            dimension_semantics=("parallel",), vmem_limit_bytes=VMEM_LIMIT),
        name="ctx_kv",
    )(ctx, ada, g1, wkv, kg, seg)


def _attn_kernel(q_ref, ktc_ref, ktx_ref, vc_ref, vx_ref, conf_ref, x_ref, ada_ref, wout_ref,
                 lam_ref, sg_ref, o_ref, *, lam_init):
    tq = q_ref.shape[1]
    vw = 2 * D_HEAD_DIM
    lam_p = lam_ref[...]
    lam = (jnp.exp(jnp.sum(lam_p[0:1] * lam_p[1:2], axis=-1, keepdims=True))
           - jnp.exp(jnp.sum(lam_p[2:3] * lam_p[3:4], axis=-1, keepdims=True)) + lam_init)
    first_map = lax.broadcasted_iota(jnp.int32, (tq, vw), 1) < D_HEAD_DIM
    outs = []
    for hd in range(D_HEADS):
        sl = slice(hd * vw, (hd + 1) * vw)
        qh = q_ref[0, :, sl]
        probs = []
        for m in range(2):
            qm = jnp.where(first_map if m == 0 else jnp.logical_not(first_map), qh, jnp.zeros_like(qh))
            sc = _dot(qm, ktc_ref[0, sl, :])
            sx = _dot(qm, ktx_ref[0, sl, :])
            mx = jnp.maximum(jnp.max(sc, axis=-1, keepdims=True), jnp.max(sx, axis=-1, keepdims=True))
            pc, px = jnp.exp(sc - mx), jnp.exp(sx - mx)
            inv = 1.0 / (jnp.sum(pc, axis=-1, keepdims=True) + jnp.sum(px, axis=-1, keepdims=True))
            probs.append((pc * inv, px * inv))
        ac = (probs[0][0] - lam * probs[1][0]).astype(BF16)
        ax = (probs[0][1] - lam * probs[1][1]).astype(BF16)
        o = _dot(ac, vc_ref[0, :, sl]) + _dot(ax, vx_ref[0, :, sl])
        outs.append(_rms(o, sg_ref[...]) * (1.0 - lam_init))
    y = jnp.concatenate([conf_ref[0]] + [o.astype(BF16) for o in outs], axis=1)
    o_ref[0] = x_ref[0] + ada_ref[0, 2:3, :] * _dot(y, wout_ref[...])


def _attn(q, ktc, ktx, vc, vx, conf, x, ada, wout, lam_p, sg, *, lam_init, tq):
    bsz, L, _ = x.shape
    lc = vc.shape[1]
    const = lambda *shape: pl.BlockSpec(shape, lambda b, t: (0,) * len(shape))
    tok = lambda w: pl.BlockSpec((1, tq, w), lambda b, t: (b, t, 0))
    return pl.pallas_call(
        functools.partial(_attn_kernel, lam_init=lam_init),
        grid=(bsz, L // tq),
        in_specs=[tok(MIX_W),
                  pl.BlockSpec((1, MIX_W, lc), lambda b, t: (b, 0, 0)),
                  pl.BlockSpec((1, MIX_W, L), lambda b, t: (b, 0, 0)),
                  pl.BlockSpec((1, lc, MIX_W), lambda b, t: (b, 0, 0)),
                  pl.BlockSpec((1, L, MIX_W), lambda b, t: (b, 0, 0)),
                  tok(MIX_W), tok(D_MODEL),
                  pl.BlockSpec((1, N_ADA, D_MODEL), lambda b, t: (b, 0, 0)),
                  const(2 * MIX_W, D_MODEL), const(4, D_HEAD_DIM), const(1, 2 * D_HEAD_DIM)],
        out_specs=tok(D_MODEL),
        out_shape=jax.ShapeDtypeStruct(x.shape, F32),
        compiler_params=pltpu.CompilerParams(
            dimension_semantics=("parallel", "parallel"), vmem_limit_bytes=VMEM_LIMIT),
        name="diff_attn",
    )(q, ktc, ktx, vc, vx, conf, x, ada, wout, lam_p, sg)


def _top16(s, iota):
    work = s
    rank = jnp.full(s.shape, float(PEER_TOPK), F32)
    vals = []
    for r in range(PEER_TOPK):
        m = jnp.max(work, axis=0, keepdims=True)
        first = jnp.min(jnp.where(work == m, iota, 1e9), axis=0, keepdims=True)
        sel = iota == first
        vals.append(m)
        rank = jnp.where(sel, float(r), rank)
        work = jnp.where(sel, -jnp.inf, work)
    return vals, rank


def _dup_bf16(v):
    bits = pltpu.bitcast(v.astype(BF16).astype(F32), jnp.uint32)
    return bits | (bits >> 16)


def _pack_bf16(v):
    n = v.shape[0] // 2
    bits = pltpu.bitcast(v.astype(BF16).astype(F32), jnp.uint32)
    return (bits[:n] >> 16) | bits[n:]


def _peer_query_kernel(x_ref, ada_ref, g2_ref, wqh_ref, wql_ref, skh_ref, skl_ref,
                       h2t_ref, r2_ref, a2_ref, n1_ref, b1_ref, qt_ref):
    tb = x_ref.shape[0]
    shift, scale = ada_ref[0, 3:4, :], ada_ref[0, 4:5, :]
    h2 = _norm_mod(x_ref[...], g2_ref[...], shift, scale)
    h2t = h2.T
    hh, hl = _split(h2t)
    h2t_ref[...] = hh
    qt_ref[...] = _dot(wqh_ref[...], hh) + (_dot(wqh_ref[...], hl) + _dot(wql_ref[...], hh))

    iota_k = lax.broadcasted_iota(jnp.int32, (PEER_KEYS, tb), 0).astype(F32)
    iota_c = lax.broadcasted_iota(jnp.int32, (_N_CAND, tb), 0)
    valid = iota_c < 24
    for g, (_, nv) in enumerate(_STAIR[1:], start=1):
        valid = jnp.logical_or(valid, jnp.logical_and(iota_c >= 16 + 8 * g, iota_c < 16 + 8 * g + nv))
    valid = jnp.logical_or(valid, iota_c >= _N_CAND - 8)
    iota_c = iota_c.astype(F32)

    def head(hd, carry):
        def scores(p):
            qp = qt_ref[pl.ds(pl.multiple_of((2 * hd + p) * PEER_KEYS, PEER_KEYS), PEER_KEYS), :]
            qh, ql = _split(qp)
            kh, kl = skh_ref[2 * hd + p], skl_ref[2 * hd + p]
            return _dot(kh, qh) + (_dot(kh, ql) + _dot(kl, qh))

        s1, s2 = scores(0), scores(1)
        v1, rank1 = _top16(s1, iota_k)
        v2, rank2 = _top16(s2, iota_k)
        top1 = jnp.concatenate(v1, axis=0)
        top2 = jnp.concatenate(v2, axis=0)
        groups = [top2 + v1[0]] + [top2[0:8] + v1[k1] for k1, _ in _STAIR] + [top1[8:16] + v2[0]]
        cand = jnp.where(valid, jnp.concatenate(groups, axis=0), -jnp.inf)
        taken = jnp.zeros((_N_CAND, tb), F32)
        z = jnp.zeros((1, tb), F32)
        best = None
        for r in range(PEER_TOPK):
            m = jnp.max(cand, axis=0, keepdims=True)
            first = jnp.min(jnp.where(cand == m, iota_c, 1e9), axis=0, keepdims=True)
            sel = iota_c == first
            best = m if best is None else best
            z = z + jnp.exp(m - best)
            taken = jnp.where(sel, 1.0, taken)
            cand = jnp.where(sel, -jnp.inf, cand)
        counts = [jnp.sum(taken[0:16], axis=0, keepdims=True)]
        counts += [jnp.sum(taken[16 + 8 * g:24 + 8 * g], axis=0, keepdims=True)
                   for g in range(len(_STAIR))]
        counts += [taken[_N_CAND - 8 + i:_N_CAND - 7 + i] for i in range(8)]
        n1 = jnp.zeros((PEER_KEYS, tb), F32)
        for k1 in range(PEER_TOPK):
            n1 = jnp.where(rank1 == float(k1), counts[k1], n1)
        r2_ref[hd] = _pack_bf16(rank2)
        a2_ref[hd] = _pack_bf16(jnp.exp(s2 - v2[0]))
        n1_ref[hd] = _dup_bf16(n1)
        b1_ref[hd] = _dup_bf16(jnp.exp(s1 - v1[0]) * (1.0 / z))
        return carry

    lax.fori_loop(0, PEER_HEADS, head, 0)


def _peer_query(x, ada, g2, wqh, wql, skh, skl, *, tb, rows_per_ada):
    T = x.shape[0]
    qd = wqh.shape[0]
    const = lambda *shape: pl.BlockSpec(shape, lambda j: (0,) * len(shape))
    tab = lambda rows: pl.BlockSpec((PEER_HEADS, rows, tb), lambda j: (0, 0, j))
    tab_shape = lambda rows: jax.ShapeDtypeStruct((PEER_HEADS, rows, T), jnp.uint32)
    half = PEER_KEYS // 2
    return pl.pallas_call(
        _peer_query_kernel,
        grid=(T // tb,),
        in_specs=[pl.BlockSpec((tb, D_MODEL), lambda j: (j, 0)),
                  pl.BlockSpec((1, N_ADA, D_MODEL), lambda j: ((j * tb) // rows_per_ada, 0, 0)),
                  const(1, D_MODEL), const(qd, D_MODEL), const(qd, D_MODEL),
                  const(2 * PEER_HEADS, PEER_KEYS, PEER_KEYS),
                  const(2 * PEER_HEADS, PEER_KEYS, PEER_KEYS)],
        out_specs=[pl.BlockSpec((D_MODEL, tb), lambda j: (0, j)), tab(half), tab(half),
                   tab(PEER_KEYS), tab(PEER_KEYS)],
        out_shape=[jax.ShapeDtypeStruct((D_MODEL, T), BF16), tab_shape(half), tab_shape(half),
                   tab_shape(PEER_KEYS), tab_shape(PEER_KEYS)],
        scratch_shapes=[pltpu.VMEM((qd, tb), F32)],
        compiler_params=pltpu.CompilerParams(
            dimension_semantics=("parallel",), vmem_limit_bytes=VMEM_LIMIT),
        name="peer_query",
    )(x, ada, g2, wqh, wql, skh, skl)


def _peer_dense_kernel(h2t_ref, u_ref, vt_ref, r2_ref, a2_ref, n1_ref, b1_ref, x_ref, ada_ref,
                       o_ref, acc_ref, p_ref, g_ref, *, n_et):
    i = pl.program_id(1)
    te, tb = p_ref.shape

    @pl.when(i == 0)
    def _():
        acc_ref[...] = jnp.zeros_like(acc_ref)
        p_ref[...] = jnp.zeros_like(p_ref)

    def packed_row(ref, hd, j, cs):
        word = jnp.broadcast_to(ref[hd, 0, j:j + 1, cs], (SUBLANES, LANES))
        return pltpu.bitcast(word, BF16)

    at = _dot(u_ref[...], h2t_ref[...])

    for j in range(te // PEER_KEYS):
        for c in range(tb // LANES):
            cs = slice(c * LANES, (c + 1) * LANES)
            cnt = [packed_row(n1_ref, hd, j, cs) for hd in range(PEER_HEADS)]
            wgt = [packed_row(b1_ref, hd, j, cs) for hd in range(PEER_HEADS)]
            for k in range(PEER_KEYS // BF16_ROWS):
                words = slice(k * SUBLANES, (k + 1) * SUBLANES)
                g = None
                for hd in range(PEER_HEADS):
                    rank = pltpu.bitcast(r2_ref[hd, words, cs], BF16)
                    val = pltpu.bitcast(a2_ref[hd, words, cs], BF16)
                    term = jnp.where(rank < cnt[hd], val, jnp.zeros_like(val)) * wgt[hd]
                    g = term if g is None else g + term
                g_ref[j * PEER_KEYS + k * BF16_ROWS:j * PEER_KEYS + (k + 1) * BF16_ROWS, cs] = g

    acc_ref[...] += _dot(vt_ref[...], p_ref[...])
    p_ref[...] = g_ref[...] * _gelu(at).astype(BF16)

    @pl.when(i == n_et)
    def _():
        o_ref[...] = x_ref[...] + ada_ref[0, 5:6, :] * acc_ref[...].T


def _peer_dense(x, h2t, tabs, u, vt, ada, *, tb, te, rows_per_ada):
    T = x.shape[0]
    n_exp = u.shape[0]
    n_et = n_exp // te
    n1 = te // PEER_KEYS
    assert T % tb == 0 and rows_per_ada % tb == 0 and n_exp % te == 0
    r2, a2, n1w, b1w = tabs
    n1w = n1w.reshape(PEER_HEADS, n_et, n1, T)
    b1w = b1w.reshape(PEER_HEADS, n_et, n1, T)
    tab = pl.BlockSpec((PEER_HEADS, PEER_KEYS // 2, tb), lambda j, i: (0, 0, j))
    tab1 = pl.BlockSpec((PEER_HEADS, 1, n1, tb), lambda j, i: (0, jnp.minimum(i, n_et - 1), 0, j))
    return pl.pallas_call(
        functools.partial(_peer_dense_kernel, n_et=n_et),
        grid=(T // tb, n_et + 1),
        in_specs=[pl.BlockSpec((D_MODEL, tb), lambda j, i: (0, j)),
                  pl.BlockSpec((te, D_MODEL), lambda j, i: (jnp.minimum(i, n_et - 1), 0)),
                  pl.BlockSpec((D_MODEL, te), lambda j, i: (0, jnp.maximum(i - 1, 0))),
                  tab, tab, tab1, tab1,
                  pl.BlockSpec((tb, D_MODEL), lambda j, i: (j, 0)),
                  pl.BlockSpec((1, N_ADA, D_MODEL), lambda j, i: ((j * tb) // rows_per_ada, 0, 0))],
        out_specs=pl.BlockSpec((tb, D_MODEL), lambda j, i: (j, 0)),
        out_shape=jax.ShapeDtypeStruct(x.shape, F32),
        scratch_shapes=[pltpu.VMEM((D_MODEL, tb), F32), pltpu.VMEM((te, tb), BF16),
                        pltpu.VMEM((te, tb), BF16)],
        compiler_params=pltpu.CompilerParams(
            dimension_semantics=("parallel", "arbitrary"), vmem_limit_bytes=VMEM_LIMIT),
        name="peer_dense",
    )(h2t, u, vt, r2, a2, n1w, b1w, x, ada)


def _peer(x, ada, g2, wq, subkeys, u_tab, v_tab, *, rows_per_ada):
    wqh, wql = _split(wq.T)
    skh, skl = _split(subkeys.reshape(2 * PEER_HEADS, PEER_KEYS, -1))
    h2t, *tabs = _peer_query(x, ada, g2, wqh, wql, skh, skl, tb=256, rows_per_ada=rows_per_ada)

    def packed_order(tab):
        half = PEER_KEYS // 2
        t4 = tab.astype(BF16).reshape(PEER_KEYS, 2, half, D_MODEL)
        return t4.transpose(0, 2, 1, 3).reshape(PEER_KEYS * PEER_KEYS, D_MODEL)

    return _peer_dense(x, h2t, tabs, packed_order(u_tab), packed_order(v_tab).T, ada,
                       tb=1024, te=512, rows_per_ada=rows_per_ada)


def _rope_tables(L):
    pos = jnp.arange(L, dtype=jnp.int32)
    r = (pos // GRID_W).astype(F32)
    col = (pos % GRID_W).astype(F32)
    n_freq = D_HEAD_DIM // 4
    inv = ROPE_THETA ** (-jnp.arange(n_freq, dtype=F32) / n_freq)
    ang = jnp.concatenate([r[:, None] * inv, col[:, None] * inv], axis=-1)
    cos = jnp.repeat(jnp.cos(ang), 2, axis=-1)
    sin = jnp.repeat(jnp.sin(ang), 2, axis=-1) * jnp.tile(jnp.array([-1.0, 1.0], F32), D_HEAD_DIM // 2)
    reps = MIX_W // D_HEAD_DIM
    return jnp.tile(cos, (1, reps)), jnp.tile(sin, (1, reps))


def kernel(x, c, ctx, c_ctx, w_ada, b_ada, norm1_g, norm2_g, w_in, w_out, a_conv_w, b_vnorm_g,
           b_spatial_w, b_spatial_b, c_conv_w, c_conv_b, c_norm_g, d_qk_norm_g, d_lambda,
           d_subln_g, peer_wq, peer_subkeys, peer_u, peer_v):
    bsz, L, _ = x.shape
    lc = ctx.shape[1]
    depth = w_ada.shape[0]
    assert depth == 2 and bsz + 1 <= ADA_ROWS

    cc = jnp.zeros((ADA_ROWS, D_MODEL), F32).at[:bsz].set(c).at[bsz].set(c_ctx)
    ada = _ada(cc, w_ada, b_ada).reshape(depth, ADA_ROWS, N_ADA, D_MODEL)
    row = lambda v: v.reshape(1, -1)

    ada_x, ada_c = ada[0, :bsz], ada[0, bsz:bsz + 1]
    bsb = jnp.repeat(b_spatial_b[0].T, MIX_W // B_GROUPS, axis=1)
    even = functools.partial(
        _even_mixer, g1=row(norm1_g[0]), win=w_in[0].astype(BF16), wout=w_out[0].astype(BF16),
        cw=a_conv_w[0], vg=row(b_vnorm_g[0]), ws=b_spatial_w[0].astype(BF16), bsb=bsb)
    peer0 = functools.partial(_peer, g2=row(norm2_g[0]), wq=peer_wq[0], subkeys=peer_subkeys[0],
                              u_tab=peer_u[0], v_tab=peer_v[0])
    x = even(x, ada_x, tm=512)
    x = peer0(x.reshape(bsz * L, D_MODEL), ada_x, rows_per_ada=L).reshape(bsz, L, D_MODEL)
    ctx = even(ctx, jnp.broadcast_to(ada_c, (bsz, N_ADA, D_MODEL)), tm=lc)
    ctx = peer0(ctx.reshape(bsz * lc, D_MODEL), ada_c, rows_per_ada=bsz * lc).reshape(bsz, lc, D_MODEL)

    ada_x, ada_c = ada[1, :bsz], ada[1, bsz:bsz + 1]
    lam_init = 0.8 - 0.6 * math.exp(-0.3 * 1)
    win = w_in[1].astype(BF16)
    cos_t, sin_t = _rope_tables(L)
    seg_id = jnp.arange(MIX_W, dtype=jnp.int32) // D_HEAD_DIM
    seg = (seg_id[:, None] == seg_id[None, :]).astype(BF16)
    reps = MIX_W // D_HEAD_DIM
    qg, kg = row(jnp.tile(d_qk_norm_g[0, 0], reps)), row(jnp.tile(d_qk_norm_g[0, 1], reps))
    conf, q, ktx, vx = _odd_in(x, ada_x, row(norm1_g[1]), win, c_conv_w[0], row(c_conv_b[0]),
                               row(c_norm_g[0]), qg, kg, cos_t, sin_t, seg, tm=512)
    ktc, vc = _ctx_kv(ctx, jnp.broadcast_to(ada_c, (bsz, N_ADA, D_MODEL)), row(norm1_g[1]),
                      win[:, 3 * MIX_W:], kg, seg)
    x = _attn(q, ktc, ktx, vc, vx, conf, x, ada_x, w_out[1].astype(BF16), d_lambda[0],
              row(d_subln_g[0]), lam_init=lam_init, tq=256)
    x = _peer(x.reshape(bsz * L, D_MODEL), ada_x, row(norm2_g[1]), peer_wq[1], peer_subkeys[1],
              peer_u[1], peer_v[1], rows_per_ada=L)
    return x.reshape(bsz, L, D_MODEL)
```

```python
import functools
import math

import jax
import jax.numpy as jnp
from jax import lax
from jax.experimental import pallas as pl
from jax.experimental.pallas import tpu as pltpu

F32 = jnp.float32
BF16 = jnp.bfloat16

D_MODEL = 1024
MIX_W = 512
IN_W = 5 * MIX_W
GRID_W = 64
CHUNK = 128
B_GROUPS = 8
A_CONV = 3
C_CONV = 31
D_HEADS = 4
D_HEAD_DIM = 64
ROPE_THETA = 10000.0
N_ADA = 6
PEER_HEADS = 8
PEER_KEYS = 128
PEER_TOPK = 16
EPS = 1e-6

SUBLANES, LANES = 8, 128
BF16_ROWS = 2 * SUBLANES

ADA_ROWS = 16
CONV_HALO = 16
VMEM_LIMIT = 56 * 1024 * 1024

_STAIR = ((1, 8), (2, 5), (3, 4), (4, 3), (5, 2), (6, 2), (7, 2))


_GELU_C0 = -2.0 * math.sqrt(2.0 / math.pi) * math.log2(math.e)
_GELU_C1 = 0.044715 * _GELU_C0


def _gelu(x):
    return x / (1.0 + jnp.exp2(x * (_GELU_C0 + _GELU_C1 * (x * x))))


def _sigmoid(x):
    return 1.0 / (1.0 + jnp.exp(-x))


def _rms(x, g):
    return x * lax.rsqrt(jnp.mean(x * x, axis=-1, keepdims=True) + EPS) * g


def _norm_mod(x, g, shift, scale):
    return _rms(x, g) * (1.0 + scale) + shift


def _split(x):
    hi = x.astype(BF16)
    lo = (x - hi.astype(F32)).astype(BF16)
    return hi, lo


def _dot(a, b):
    return jnp.dot(a, b, preferred_element_type=F32)


def _ada_kernel(c_ref, w_ref, b_ref, o_ref):
    cc = c_ref[...]
    s = cc * _sigmoid(cc)
    o_ref[0] = jnp.dot(s, w_ref[0], precision=lax.Precision.HIGHEST,
                       preferred_element_type=F32) + b_ref[0]


def _ada(cc, w_ada, b_ada):
    depth, _, n = w_ada.shape
    tn = 1536
    return pl.pallas_call(
        _ada_kernel,
        grid=(depth, n // tn),
        in_specs=[pl.BlockSpec((ADA_ROWS, D_MODEL), lambda l, j: (0, 0)),
                  pl.BlockSpec((1, D_MODEL, tn), lambda l, j: (l, 0, j)),
                  pl.BlockSpec((1, 1, tn), lambda l, j: (l, 0, j))],
        out_specs=pl.BlockSpec((1, ADA_ROWS, tn), lambda l, j: (l, 0, j)),
        out_shape=jax.ShapeDtypeStruct((depth, ADA_ROWS, n), F32),
        compiler_params=pltpu.CompilerParams(vmem_limit_bytes=VMEM_LIMIT),
        name="ada",
    )(cc, w_ada, b_ada.reshape(depth, 1, n))


def _even_kernel(x_ref, xp_ref, xn_ref, ada_ref, g1_ref, win_ref, wout_ref, cw_ref, vg_ref,
                 ws_ref, bsb_ref, o_ref, *, n_tiles):
    t = pl.program_id(1)
    tm = x_ref.shape[1]
    shift, scale, gate = ada_ref[0, 0:1, :], ada_ref[0, 1:2, :], ada_ref[0, 2:3, :]
    g1 = g1_ref[...]

    x = x_ref[0]
    h = _norm_mod(x, g1, shift, scale).astype(BF16)
    proj = _dot(h, win_ref[...])
    a_h, a_c, a_b = proj[:, 0:MIX_W], proj[:, MIX_W:2 * MIX_W], proj[:, 2 * MIX_W:3 * MIX_W]
    b_u, b_v = proj[:, 3 * MIX_W:4 * MIX_W], proj[:, 4 * MIX_W:5 * MIX_W]

    z = a_c * a_h

    def edge_z(ref, row):
        hh = _norm_mod(ref[0], g1, shift, scale).astype(BF16)
        pp = _dot(hh, win_ref[:, 0:2 * MIX_W])
        return (pp[:, 0:MIX_W] * pp[:, MIX_W:2 * MIX_W])[row:row + 1, :]

    zp = jnp.where(t > 0, edge_z(xp_ref, 7), 0.0)
    zn = jnp.where(t < n_tiles - 1, edge_z(xn_ref, 0), 0.0)
    row = lax.broadcasted_iota(jnp.int32, (tm, MIX_W), 0)
    z_m1 = jnp.where(row == 0, zp, pltpu.roll(z, 1, axis=0))
    z_p1 = jnp.where(row == tm - 1, zn, pltpu.roll(z, tm - 1, axis=0))
    ya = a_b * (cw_ref[0:1, :] * z_m1 + cw_ref[1:2, :] * z + cw_ref[2:3, :] * z_p1)

    u = _gelu(b_u)
    v = _rms(_gelu(b_v), vg_ref[...]).astype(BF16)
    lane = lax.broadcasted_iota(jnp.int32, (CHUNK, CHUNK), 1)
    group_w = MIX_W // B_GROUPS
    rows = []
    for n in range(tm // CHUNK):
        cols = []
        for j in range(MIX_W // CHUNK):
            vv = v[n * CHUNK:(n + 1) * CHUNK, j * CHUNK:(j + 1) * CHUNK]
            s0 = _dot(ws_ref[2 * j], vv)
            s1 = _dot(ws_ref[2 * j + 1], vv)
            cols.append(jnp.where(lane < group_w, s0, s1))
        rows.append(jnp.concatenate(cols, axis=1) + bsb_ref[...])
    s = jnp.concatenate(rows, axis=0)
    yb = u * s

    y = jnp.concatenate([ya, yb], axis=1).astype(BF16)
    o_ref[0] = x + gate * _dot(y, wout_ref[...])


def _even_mixer(x, ada, g1, win, wout, cw, vg, ws, bsb, *, tm):
    bsz, L, _ = x.shape
    n_tiles = L // tm
    hb = tm // 8
    const = lambda *shape: pl.BlockSpec(shape, lambda b, t: (0,) * len(shape))
    return pl.pallas_call(
        functools.partial(_even_kernel, n_tiles=n_tiles),
        grid=(bsz, n_tiles),
        in_specs=[pl.BlockSpec((1, tm, D_MODEL), lambda b, t: (b, t, 0)),
                  pl.BlockSpec((1, 8, D_MODEL), lambda b, t: (b, jnp.maximum(t * hb - 1, 0), 0)),
                  pl.BlockSpec((1, 8, D_MODEL),
                               lambda b, t: (b, jnp.minimum((t + 1) * hb, L // 8 - 1), 0)),
                  pl.BlockSpec((1, N_ADA, D_MODEL), lambda b, t: (b, 0, 0)),
                  const(1, D_MODEL), const(D_MODEL, IN_W), const(2 * MIX_W, D_MODEL),
                  const(A_CONV, MIX_W), const(1, MIX_W), const(B_GROUPS, CHUNK, CHUNK),
                  const(CHUNK, MIX_W)],
        out_specs=pl.BlockSpec((1, tm, D_MODEL), lambda b, t: (b, t, 0)),
        out_shape=jax.ShapeDtypeStruct(x.shape, F32),
        compiler_params=pltpu.CompilerParams(
            dimension_semantics=("parallel", "parallel"), vmem_limit_bytes=VMEM_LIMIT),
        name="even_mixer",
    )(x, x, x, ada, g1, win, wout, cw, vg, ws, bsb)


def _head_norm(t, gain, seg_ref):
    hi, lo = _split(t * t)
    ss = _dot(hi, seg_ref[...]) + _dot(lo, seg_ref[...])
    return t * lax.rsqrt(ss * (1.0 / D_HEAD_DIM) + EPS) * gain


def _odd_in_kernel(x_ref, xp_ref, xn_ref, ada_ref, g1_ref, win_ref, cw_ref, cb_ref, cg_ref,
                   qg_ref, kg_ref, cos_ref, sin_ref, seg_ref,
                   conf_ref, q_ref, kt_ref, v_ref, ybuf_ref, *, n_tiles):
    t = pl.program_id(1)
    tm = x_ref.shape[1]
    shift, scale = ada_ref[0, 0:1, :], ada_ref[0, 1:2, :]
    g1 = g1_ref[...]

    h = _norm_mod(x_ref[0], g1, shift, scale).astype(BF16)
    proj = _dot(h, win_ref[...])

    def glu_rows(ref):
        hh = _norm_mod(ref[0], g1, shift, scale).astype(BF16)
        pp = _dot(hh, win_ref[:, 0:2 * MIX_W])
        return pp[:, 0:MIX_W] * _sigmoid(pp[:, MIX_W:2 * MIX_W])

    ybuf_ref[0:CONV_HALO, :] = jnp.where(t > 0, glu_rows(xp_ref), 0.0)
    ybuf_ref[CONV_HALO:CONV_HALO + tm, :] = proj[:, 0:MIX_W] * _sigmoid(proj[:, MIX_W:2 * MIX_W])
    ybuf_ref[CONV_HALO + tm:2 * CONV_HALO + tm, :] = jnp.where(t < n_tiles - 1, glu_rows(xn_ref), 0.0)
    base = CONV_HALO - C_CONV // 2
    acc = jnp.zeros((tm, MIX_W), F32) + cb_ref[...]
    for k in range(C_CONV):
        acc = acc + cw_ref[k:k + 1, :] * ybuf_ref[base + k:base + k + tm, :]
    yn = _rms(acc, cg_ref[...])
    conf_ref[0] = (yn * _sigmoid(yn)).astype(BF16)

    lane = lax.broadcasted_iota(jnp.int32, (tm, MIX_W), 1)
    even_lane = (lane % 2) == 0

    def rope(tn):
        partner = jnp.where(even_lane, pltpu.roll(tn, MIX_W - 1, axis=1), pltpu.roll(tn, 1, axis=1))
        return tn * cos_ref[...] + partner * sin_ref[...]

    q = rope(_head_norm(proj[:, 2 * MIX_W:3 * MIX_W], qg_ref[...], seg_ref))
    k = rope(_head_norm(proj[:, 3 * MIX_W:4 * MIX_W], kg_ref[...], seg_ref))
    q_ref[0] = (q * (D_HEAD_DIM ** -0.5)).astype(BF16)
    kt_ref[0] = k.T.astype(BF16)
    v_ref[0] = proj[:, 4 * MIX_W:5 * MIX_W].astype(BF16)


def _odd_in(x, ada, g1, win, cw, cb, cg, qg, kg, cos_t, sin_t, seg, *, tm):
    bsz, L, _ = x.shape
    n_tiles = L // tm
    hb = tm // CONV_HALO
    const = lambda *shape: pl.BlockSpec(shape, lambda b, t: (0,) * len(shape))
    tok = lambda w: pl.BlockSpec((1, tm, w), lambda b, t: (b, t, 0))
    return pl.pallas_call(
        functools.partial(_odd_in_kernel, n_tiles=n_tiles),
        grid=(bsz, n_tiles),
        in_specs=[tok(D_MODEL),
                  pl.BlockSpec((1, CONV_HALO, D_MODEL),
                               lambda b, t: (b, jnp.maximum(t * hb - 1, 0), 0)),
                  pl.BlockSpec((1, CONV_HALO, D_MODEL),
                               lambda b, t: (b, jnp.minimum((t + 1) * hb, L // CONV_HALO - 1), 0)),
                  pl.BlockSpec((1, N_ADA, D_MODEL), lambda b, t: (b, 0, 0)),
                  const(1, D_MODEL), const(D_MODEL, IN_W), const(C_CONV, MIX_W),
                  const(1, MIX_W), const(1, MIX_W), const(1, MIX_W), const(1, MIX_W),
                  pl.BlockSpec((tm, MIX_W), lambda b, t: (t, 0)),
                  pl.BlockSpec((tm, MIX_W), lambda b, t: (t, 0)),
                  const(MIX_W, MIX_W)],
        out_specs=[tok(MIX_W), tok(MIX_W),
                   pl.BlockSpec((1, MIX_W, tm), lambda b, t: (b, 0, t)),
                   tok(MIX_W)],
        out_shape=[jax.ShapeDtypeStruct((bsz, L, MIX_W), BF16),
                   jax.ShapeDtypeStruct((bsz, L, MIX_W), BF16),
                   jax.ShapeDtypeStruct((bsz, MIX_W, L), BF16),
                   jax.ShapeDtypeStruct((bsz, L, MIX_W), BF16)],
        scratch_shapes=[pltpu.VMEM((tm + 2 * CONV_HALO, MIX_W), F32)],
        compiler_params=pltpu.CompilerParams(
            dimension_semantics=("parallel", "parallel"), vmem_limit_bytes=VMEM_LIMIT),
        name="odd_in",
    )(x, x, x, ada, g1, win, cw, cb, cg, qg, kg, cos_t, sin_t, seg)


def _ctx_kv_kernel(x_ref, ada_ref, g1_ref, wkv_ref, kg_ref, seg_ref, kt_ref, v_ref):
    shift, scale = ada_ref[0, 0:1, :], ada_ref[0, 1:2, :]
    h = _norm_mod(x_ref[0], g1_ref[...], shift, scale).astype(BF16)
    proj = _dot(h, wkv_ref[...])
    k = _head_norm(proj[:, 0:MIX_W], kg_ref[...], seg_ref)
    kt_ref[0] = k.T.astype(BF16)
    v_ref[0] = proj[:, MIX_W:2 * MIX_W].astype(BF16)


def _ctx_kv(ctx, ada, g1, wkv, kg, seg):
    bsz, L, _ = ctx.shape
    const = lambda *shape: pl.BlockSpec(shape, lambda b: (0,) * len(shape))
    return pl.pallas_call(
        _ctx_kv_kernel,
        grid=(bsz,),
        in_specs=[pl.BlockSpec((1, L, D_MODEL), lambda b: (b, 0, 0)),
                  pl.BlockSpec((1, N_ADA, D_MODEL), lambda b: (b, 0, 0)),
                  const(1, D_MODEL), const(D_MODEL, 2 * MIX_W), const(1, MIX_W),
                  const(MIX_W, MIX_W)],
        out_specs=[pl.BlockSpec((1, MIX_W, L), lambda b: (b, 0, 0)),
                   pl.BlockSpec((1, L, MIX_W), lambda b: (b, 0, 0))],
        out_shape=[jax.ShapeDtypeStruct((bsz, MIX_W, L), BF16),
                   jax.ShapeDtypeStruct((bsz, L, MIX_W), BF16)],
        compiler_params=pltpu.CompilerParams(
            dimension_semantics=("parallel",), vmem_limit_bytes=VMEM_LIMIT),
        name="ctx_kv",
    )(ctx, ada, g1, wkv, kg, seg)


def _attn_kernel(q_ref, ktc_ref, ktx_ref, vc_ref, vx_ref, conf_ref, x_ref, ada_ref, wout_ref,
                 lam_ref, sg_ref, o_ref, *, lam_init):
    tq = q_ref.shape[1]
    vw = 2 * D_HEAD_DIM
    lam_p = lam_ref[...]
    lam = (jnp.exp(jnp.sum(lam_p[0:1] * lam_p[1:2], axis=-1, keepdims=True))
           - jnp.exp(jnp.sum(lam_p[2:3] * lam_p[3:4], axis=-1, keepdims=True)) + lam_init)
    first_map = lax.broadcasted_iota(jnp.int32, (tq, vw), 1) < D_HEAD_DIM
    outs = []
    for hd in range(D_HEADS):
        sl = slice(hd * vw, (hd + 1) * vw)
        qh = q_ref[0, :, sl]
        probs = []
        for m in range(2):
            qm = jnp.where(first_map if m == 0 else jnp.logical_not(first_map), qh, jnp.zeros_like(qh))
            sc = _dot(qm, ktc_ref[0, sl, :])
            sx = _dot(qm, ktx_ref[0, sl, :])
            mx = jnp.maximum(jnp.max(sc, axis=-1, keepdims=True), jnp.max(sx, axis=-1, keepdims=True))
            pc, px = jnp.exp(sc - mx), jnp.exp(sx - mx)
            inv = 1.0 / (jnp.sum(pc, axis=-1, keepdims=True) + jnp.sum(px, axis=-1, keepdims=True))
            probs.append((pc * inv, px * inv))
        ac = (probs[0][0] - lam * probs[1][0]).astype(BF16)
        ax = (probs[0][1] - lam * probs[1][1]).astype(BF16)
        o = _dot(ac, vc_ref[0, :, sl]) + _dot(ax, vx_ref[0, :, sl])
        outs.append(_rms(o, sg_ref[...]) * (1.0 - lam_init))
    y = jnp.concatenate([conf_ref[0]] + [o.astype(BF16) for o in outs], axis=1)
    o_ref[0] = x_ref[0] + ada_ref[0, 2:3, :] * _dot(y, wout_ref[...])


def _attn(q, ktc, ktx, vc, vx, conf, x, ada, wout, lam_p, sg, *, lam_init, tq):
    bsz, L, _ = x.shape
    lc = vc.shape[1]
    const = lambda *shape: pl.BlockSpec(shape, lambda b, t: (0,) * len(shape))
    tok = lambda w: pl.BlockSpec((1, tq, w), lambda b, t: (b, t, 0))
    return pl.pallas_call(
        functools.partial(_attn_kernel, lam_init=lam_init),
        grid=(bsz, L // tq),
        in_specs=[tok(MIX_W),
                  pl.BlockSpec((1, MIX_W, lc), lambda b, t: (b, 0, 0)),
                  pl.BlockSpec((1, MIX_W, L), lambda b, t: (b, 0, 0)),
                  pl.BlockSpec((1, lc, MIX_W), lambda b, t: (b, 0, 0)),
                  pl.BlockSpec((1, L, MIX_W), lambda b, t: (b, 0, 0)),
                  tok(MIX_W), tok(D_MODEL),
                  pl.BlockSpec((1, N_ADA, D_MODEL), lambda b, t: (b, 0, 0)),
                  const(2 * MIX_W, D_MODEL), const(4, D_HEAD_DIM), const(1, 2 * D_HEAD_DIM)],
        out_specs=tok(D_MODEL),
        out_shape=jax.ShapeDtypeStruct(x.shape, F32),
        compiler_params=pltpu.CompilerParams(
            dimension_semantics=("parallel", "parallel"), vmem_limit_bytes=VMEM_LIMIT),
        name="diff_attn",
    )(q, ktc, ktx, vc, vx, conf, x, ada, wout, lam_p, sg)


def _exact_ranks(s, label):
    work = s
    rank = jnp.full(s.shape, float(PEER_TOPK), F32)
    for r in range(PEER_TOPK):
        m = jnp.max(work, axis=0, keepdims=True)
        first = jnp.min(jnp.where(work == m, label, 1e9), axis=0, keepdims=True)
        sel = label == first
        rank = jnp.where(sel, float(r), rank)
        work = jnp.where(sel, -jnp.inf, work)
    return rank


def _batcher_network(n):
    def merge(lo, hi, r):
        step = 2 * r
        if step < hi - lo:
            yield from merge(lo, hi, step)
            yield from merge(lo + r, hi, step)
            yield from [(i, i + r) for i in range(lo + r, hi - r, step)]
        else:
            yield (lo, lo + r)

    def sort(lo, hi):
        if hi > lo:
            mid = lo + (hi - lo) // 2
            yield from sort(lo, mid)
            yield from sort(mid + 1, hi)
            yield from merge(lo, hi, 1)

    return tuple(sort(0, n - 1))


_SORT16 = _batcher_network(PEER_TOPK)


def _sublane_allreduce(x, op):
    for shift in (4, 2, 1):
        x = op(x, pltpu.roll(x, shift, axis=0))
    return x


def _sorted_top16(rows):
    b = list(rows)
    for i, j in _SORT16:
        b[i], b[j] = jnp.maximum(b[i], b[j]), jnp.minimum(b[i], b[j])
    for shift in (4, 2, 1):
        m = [jnp.maximum(b[i], pltpu.roll(b[PEER_TOPK - 1 - i], shift, axis=0))
             for i in range(PEER_TOPK)]
        d = PEER_TOPK // 2
        while d:
            for i in range(PEER_TOPK):
                if not i & d:
                    m[i], m[i + d] = jnp.maximum(m[i], m[i + d]), jnp.minimum(m[i], m[i + d])
            d //= 2
        b = m
    return b


def _lookup_by_rank(rows, top, values):
    out = []
    for a in rows:
        r = jnp.zeros_like(a) + values[0]
        for k in range(PEER_TOPK):
            r = jnp.where(top[k] > a, values[k + 1], r)
        out.append(r)
    return jnp.concatenate(out, axis=0)


def _tie_flag(rows, top):
    flag = jnp.zeros_like(top[0])
    for k in range(PEER_TOPK - 1):
        flag = jnp.where(top[k] == top[k + 1], 1.0, flag)
    n_ge = None
    for a in rows:
        hit = jnp.where(a >= top[PEER_TOPK - 1], 1.0, 0.0)
        n_ge = hit if n_ge is None else n_ge + hit
    n_ge = _sublane_allreduce(n_ge, jnp.add)
    return jnp.where(n_ge != float(PEER_TOPK), 1.0, flag)


def _dup_bf16(v):
    bits = pltpu.bitcast(v.astype(BF16).astype(F32), jnp.uint32)
    return bits | (bits >> 16)


def _pack_bf16(v):
    n = v.shape[0] // 2
    bits = pltpu.bitcast(v.astype(BF16).astype(F32), jnp.uint32)
    return (bits[:n] >> 16) | bits[n:]


def _peer_query_kernel(x_ref, ada_ref, g2_ref, wqh_ref, wql_ref, skh_ref, skl_ref,
                       h2t_ref, r2_ref, a2_ref, n1_ref, b1_ref, qt_ref):
    tb = x_ref.shape[0]
    shift, scale = ada_ref[0, 3:4, :], ada_ref[0, 4:5, :]
    h2 = _norm_mod(x_ref[...], g2_ref[...], shift, scale)
    h2t = h2.T
    hh, hl = _split(h2t)
    h2t_ref[...] = hh
    qt_ref[...] = _dot(wqh_ref[...], hh) + (_dot(wqh_ref[...], hl) + _dot(wql_ref[...], hh))

    pos = lax.broadcasted_iota(jnp.int32, (PEER_KEYS, tb), 0)
    label1 = pos.astype(F32)
    label2 = jnp.where(pos < PEER_KEYS // 2, 2 * pos, 2 * pos - (PEER_KEYS - 1)).astype(F32)
    sub = lax.broadcasted_iota(jnp.int32, (SUBLANES, tb), 0)
    subf = sub.astype(F32)
    cand_label = ([subf, subf + 8.0] + [subf + 16.0 * k1 for k1, _ in _STAIR]
                  + [16.0 * (subf + 8.0)])
    cand_valid = [None, None] + [None if nv == SUBLANES else sub < nv for _, nv in _STAIR] + [None]

    def spread(vals):
        out = vals[0]
        for u in range(1, SUBLANES):
            out = jnp.where(sub == u, vals[u], out)
        return out

    def head(hd, carry):
        def scores(p):
            qp = qt_ref[pl.ds(pl.multiple_of((2 * hd + p) * PEER_KEYS, PEER_KEYS), PEER_KEYS), :]
            qh, ql = _split(qp)
            kh, kl = skh_ref[2 * hd + p], skl_ref[2 * hd + p]
            return _dot(kh, qh) + (_dot(kh, ql) + _dot(kl, qh))

        s1, s2 = scores(0), scores(1)
        rows1 = [s1[SUBLANES * r:SUBLANES * (r + 1)] for r in range(PEER_KEYS // SUBLANES)]
        rows2 = [s2[SUBLANES * r:SUBLANES * (r + 1)] for r in range(PEER_KEYS // SUBLANES)]
        top1, top2 = _sorted_top16(rows1), _sorted_top16(rows2)

        t2a, t2b, t1b = spread(top2[:SUBLANES]), spread(top2[SUBLANES:]), spread(top1[SUBLANES:])
        cand = ([t2a + top1[0], t2b + top1[0]] + [t2a + top1[k1] for k1, _ in _STAIR]
                + [t1b + top2[0]])
        cand = [c if v is None else jnp.where(v, c, -jnp.inf) for c, v in zip(cand, cand_valid)]
        z = jnp.zeros((SUBLANES, tb), F32)
        for r in range(PEER_TOPK):
            m = functools.reduce(jnp.maximum, cand)
            m = _sublane_allreduce(m, jnp.maximum)
            first = functools.reduce(
                jnp.minimum, [jnp.where(c == m, lab, 1e9) for c, lab in zip(cand, cand_label)])
            first = _sublane_allreduce(first, jnp.minimum)
            cand = [jnp.where(lab == first, -jnp.inf, c) for c, lab in zip(cand, cand_label)]
            z = z + jnp.exp(m - (top1[0] + top2[0]))
        taken = [jnp.where(c == -jnp.inf, 1.0, 0.0) if v is None
                 else jnp.where(jnp.logical_and(v, c == -jnp.inf), 1.0, 0.0)
                 for c, v in zip(cand, cand_valid)]
        counts = [_sublane_allreduce(taken[0] + taken[1], jnp.add)]
        counts += [_sublane_allreduce(t, jnp.add) for t in taken[2:-1]]
        counts += [_sublane_allreduce(jnp.where(sub == u, taken[-1], 0.0), jnp.add)
                   for u in range(SUBLANES)]
        counts.append(0.0)
        ranks = [float(k) for k in range(PEER_TOPK + 1)]

        def fast():
            return _lookup_by_rank(rows2, top2, ranks), _lookup_by_rank(rows1, top1, counts)

        def exact():
            rank1 = _exact_ranks(s1, label1)
            n1 = jnp.zeros_like(s1)
            for k1 in range(PEER_TOPK):
                n1 = jnp.where(rank1 == float(k1), jnp.tile(counts[k1], (PEER_KEYS // SUBLANES, 1)), n1)
            return _exact_ranks(s2, label2), n1

        ties = jnp.max(jnp.maximum(_tie_flag(rows1, top1), _tie_flag(rows2, top2)))
        rank2, n1 = lax.cond(ties > 0.0, exact, fast)
        tile = lambda v: jnp.tile(v, (PEER_KEYS // SUBLANES, 1))
        r2_ref[hd] = _pack_bf16(rank2)
        a2_ref[hd] = _pack_bf16(jnp.exp(s2 - tile(top2[0])))
        n1_ref[hd] = _dup_bf16(n1)
        b1_ref[hd] = _dup_bf16(jnp.exp(s1 - tile(top1[0])) * tile(1.0 / z))
        return carry

    lax.fori_loop(0, PEER_HEADS, head, 0)


def _peer_query(x, ada, g2, wqh, wql, skh, skl, *, tb, rows_per_ada):
    T = x.shape[0]
    qd = wqh.shape[0]
    const = lambda *shape: pl.BlockSpec(shape, lambda j: (0,) * len(shape))
    tab = lambda rows: pl.BlockSpec((PEER_HEADS, rows, tb), lambda j: (0, 0, j))
    tab_shape = lambda rows: jax.ShapeDtypeStruct((PEER_HEADS, rows, T), jnp.uint32)
    half = PEER_KEYS // 2
    return pl.pallas_call(
        _peer_query_kernel,
        grid=(T // tb,),
        in_specs=[pl.BlockSpec((tb, D_MODEL), lambda j: (j, 0)),
                  pl.BlockSpec((1, N_ADA, D_MODEL), lambda j: ((j * tb) // rows_per_ada, 0, 0)),
                  const(1, D_MODEL), const(qd, D_MODEL), const(qd, D_MODEL),
                  const(2 * PEER_HEADS, PEER_KEYS, PEER_KEYS),
                  const(2 * PEER_HEADS, PEER_KEYS, PEER_KEYS)],
        out_specs=[pl.BlockSpec((D_MODEL, tb), lambda j: (0, j)), tab(half), tab(half),
                   tab(PEER_KEYS), tab(PEER_KEYS)],
        out_shape=[jax.ShapeDtypeStruct((D_MODEL, T), BF16), tab_shape(half), tab_shape(half),
                   tab_shape(PEER_KEYS), tab_shape(PEER_KEYS)],
        scratch_shapes=[pltpu.VMEM((qd, tb), F32)],
        compiler_params=pltpu.CompilerParams(
            dimension_semantics=("parallel",), vmem_limit_bytes=VMEM_LIMIT),
        name="peer_query",
    )(x, ada, g2, wqh, wql, skh, skl)


def _peer_dense_kernel(h2t_ref, u_ref, vt_ref, r2_ref, a2_ref, n1_ref, b1_ref, x_ref, ada_ref,
                       o_ref, acc_ref, p_ref, g_ref, *, n_et):
    i = pl.program_id(1)
    te, tb = p_ref.shape

    @pl.when(i == 0)
    def _():
        acc_ref[...] = jnp.zeros_like(acc_ref)
        p_ref[...] = jnp.zeros_like(p_ref)

    def packed_row(ref, hd, j, cs):
        word = jnp.broadcast_to(ref[hd, 0, j:j + 1, cs], (SUBLANES, LANES))
        return pltpu.bitcast(word, BF16)

    at = _dot(u_ref[...], h2t_ref[...])

    for j in range(te // PEER_KEYS):
        for c in range(tb // LANES):
            cs = slice(c * LANES, (c + 1) * LANES)
            cnt = [packed_row(n1_ref, hd, j, cs) for hd in range(PEER_HEADS)]
            wgt = [packed_row(b1_ref, hd, j, cs) for hd in range(PEER_HEADS)]
            for k in range(PEER_KEYS // BF16_ROWS):
                words = slice(k * SUBLANES, (k + 1) * SUBLANES)
                g = None
                for hd in range(PEER_HEADS):
                    rank = pltpu.bitcast(r2_ref[hd, words, cs], BF16)
                    val = pltpu.bitcast(a2_ref[hd, words, cs], BF16)
                    term = jnp.where(rank < cnt[hd], val, jnp.zeros_like(val)) * wgt[hd]
                    g = term if g is None else g + term
                g_ref[j * PEER_KEYS + k * BF16_ROWS:j * PEER_KEYS + (k + 1) * BF16_ROWS, cs] = g

    acc_ref[...] += _dot(vt_ref[...], p_ref[...])
    p_ref[...] = g_ref[...] * _gelu(at).astype(BF16)

    @pl.when(i == n_et)
    def _():
        o_ref[...] = x_ref[...] + ada_ref[0, 5:6, :] * acc_ref[...].T


def _peer_dense(x, h2t, tabs, u, vt, ada, *, tb, te, rows_per_ada):
    T = x.shape[0]
    n_exp = u.shape[0]
    n_et = n_exp // te
    n1 = te // PEER_KEYS
    assert T % tb == 0 and rows_per_ada % tb == 0 and n_exp % te == 0
    r2, a2, n1w, b1w = tabs
    n1w = n1w.reshape(PEER_HEADS, n_et, n1, T)
    b1w = b1w.reshape(PEER_HEADS, n_et, n1, T)
    tab = pl.BlockSpec((PEER_HEADS, PEER_KEYS // 2, tb), lambda j, i: (0, 0, j))
    tab1 = pl.BlockSpec((PEER_HEADS, 1, n1, tb), lambda j, i: (0, jnp.minimum(i, n_et - 1), 0, j))
    return pl.pallas_call(
        functools.partial(_peer_dense_kernel, n_et=n_et),
        grid=(T // tb, n_et + 1),
        in_specs=[pl.BlockSpec((D_MODEL, tb), lambda j, i: (0, j)),
                  pl.BlockSpec((te, D_MODEL), lambda j, i: (jnp.minimum(i, n_et - 1), 0)),
                  pl.BlockSpec((D_MODEL, te), lambda j, i: (0, jnp.maximum(i - 1, 0))),
                  tab, tab, tab1, tab1,
                  pl.BlockSpec((tb, D_MODEL), lambda j, i: (j, 0)),
                  pl.BlockSpec((1, N_ADA, D_MODEL), lambda j, i: ((j * tb) // rows_per_ada, 0, 0))],
        out_specs=pl.BlockSpec((tb, D_MODEL), lambda j, i: (j, 0)),
        out_shape=jax.ShapeDtypeStruct(x.shape, F32),
        scratch_shapes=[pltpu.VMEM((D_MODEL, tb), F32), pltpu.VMEM((te, tb), BF16),
                        pltpu.VMEM((te, tb), BF16)],
        compiler_params=pltpu.CompilerParams(
            dimension_semantics=("parallel", "arbitrary"), vmem_limit_bytes=VMEM_LIMIT),
        name="peer_dense",
    )(h2t, u, vt, r2, a2, n1w, b1w, x, ada)


def _peer(x, ada, g2, wq, subkeys, u_tab, v_tab, *, rows_per_ada):
    wqh, wql = _split(wq.T)
    sk = jnp.stack([subkeys[:, 0], jnp.concatenate([subkeys[:, 1, 0::2], subkeys[:, 1, 1::2]], axis=1)],
                   axis=1)
    skh, skl = _split(sk.reshape(2 * PEER_HEADS, PEER_KEYS, -1))
    h2t, *tabs = _peer_query(x, ada, g2, wqh, wql, skh, skl, tb=256, rows_per_ada=rows_per_ada)
    return _peer_dense(x, h2t, tabs, u_tab.astype(BF16), v_tab.T.astype(BF16), ada,
                       tb=1024, te=512, rows_per_ada=rows_per_ada)


def _rope_tables(L):
    pos = jnp.arange(L, dtype=jnp.int32)
    r = (pos // GRID_W).astype(F32)
    col = (pos % GRID_W).astype(F32)
    n_freq = D_HEAD_DIM // 4
    inv = ROPE_THETA ** (-jnp.arange(n_freq, dtype=F32) / n_freq)
    ang = jnp.concatenate([r[:, None] * inv, col[:, None] * inv], axis=-1)
    cos = jnp.repeat(jnp.cos(ang), 2, axis=-1)
    sin = jnp.repeat(jnp.sin(ang), 2, axis=-1) * jnp.tile(jnp.array([-1.0, 1.0], F32), D_HEAD_DIM // 2)
    reps = MIX_W // D_HEAD_DIM
    return jnp.tile(cos, (1, reps)), jnp.tile(sin, (1, reps))


def kernel(x, c, ctx, c_ctx, w_ada, b_ada, norm1_g, norm2_g, w_in, w_out, a_conv_w, b_vnorm_g,
           b_spatial_w, b_spatial_b, c_conv_w, c_conv_b, c_norm_g, d_qk_norm_g, d_lambda,
           d_subln_g, peer_wq, peer_subkeys, peer_u, peer_v):
    bsz, L, _ = x.shape
    lc = ctx.shape[1]
    depth = w_ada.shape[0]
    assert depth == 2 and bsz + 1 <= ADA_ROWS

    cc = jnp.zeros((ADA_ROWS, D_MODEL), F32).at[:bsz].set(c).at[bsz].set(c_ctx)
    ada = _ada(cc, w_ada, b_ada).reshape(depth, ADA_ROWS, N_ADA, D_MODEL)
    row = lambda v: v.reshape(1, -1)

    ada_x, ada_c = ada[0, :bsz], ada[0, bsz:bsz + 1]
    bsb = jnp.repeat(b_spatial_b[0].T, MIX_W // B_GROUPS, axis=1)
    even = functools.partial(
        _even_mixer, g1=row(norm1_g[0]), win=w_in[0].astype(BF16), wout=w_out[0].astype(BF16),
        cw=a_conv_w[0], vg=row(b_vnorm_g[0]), ws=b_spatial_w[0].astype(BF16), bsb=bsb)
    peer0 = functools.partial(_peer, g2=row(norm2_g[0]), wq=peer_wq[0], subkeys=peer_subkeys[0],
                              u_tab=peer_u[0], v_tab=peer_v[0])
    x = even(x, ada_x, tm=512)
    x = peer0(x.reshape(bsz * L, D_MODEL), ada_x, rows_per_ada=L).reshape(bsz, L, D_MODEL)
    ctx = even(ctx, jnp.broadcast_to(ada_c, (bsz, N_ADA, D_MODEL)), tm=lc)
    ctx = peer0(ctx.reshape(bsz * lc, D_MODEL), ada_c, rows_per_ada=bsz * lc).reshape(bsz, lc, D_MODEL)

    ada_x, ada_c = ada[1, :bsz], ada[1, bsz:bsz + 1]
    lam_init = 0.8 - 0.6 * math.exp(-0.3 * 1)
    win = w_in[1].astype(BF16)
    cos_t, sin_t = _rope_tables(L)
    seg_id = jnp.arange(MIX_W, dtype=jnp.int32) // D_HEAD_DIM
    seg = (seg_id[:, None] == seg_id[None, :]).astype(BF16)
    reps = MIX_W // D_HEAD_DIM
    qg, kg = row(jnp.tile(d_qk_norm_g[0, 0], reps)), row(jnp.tile(d_qk_norm_g[0, 1], reps))
    conf, q, ktx, vx = _odd_in(x, ada_x, row(norm1_g[1]), win, c_conv_w[0], row(c_conv_b[0]),
                               row(c_norm_g[0]), qg, kg, cos_t, sin_t, seg, tm=512)
    ktc, vc = _ctx_kv(ctx, jnp.broadcast_to(ada_c, (bsz, N_ADA, D_MODEL)), row(norm1_g[1]),
                      win[:, 3 * MIX_W:], kg, seg)
    x = _attn(q, ktc, ktx, vc, vx, conf, x, ada_x, w_out[1].astype(BF16), d_lambda[0],
              row(d_subln_g[0]), lam_init=lam_init, tq=256)
    x = _peer(x.reshape(bsz * L, D_MODEL), ada_x, row(norm2_g[1]), peer_wq[1], peer_subkeys[1],
              peer_u[1], peer_v[1], rows_per_ada=L)
    return x.reshape(bsz, L, D_MODEL)
```

```python
import functools
import math

import jax
import jax.numpy as jnp
from jax import lax
from jax.experimental import pallas as pl
from jax.experimental.pallas import tpu as pltpu

F32 = jnp.float32
BF16 = jnp.bfloat16

D_MODEL = 1024
MIX_W = 512
IN_W = 5 * MIX_W
GRID_W = 64
CHUNK = 128
B_GROUPS = 8
A_CONV = 3
C_CONV = 31
D_HEADS = 4
D_HEAD_DIM = 64
ROPE_THETA = 10000.0
N_ADA = 6
PEER_HEADS = 8
PEER_KEYS = 128
PEER_TOPK = 16
EPS = 1e-6

SUBLANES, LANES = 8, 128
BF16_ROWS = 2 * SUBLANES

ADA_ROWS = 16
CONV_HALO = 16
VMEM_LIMIT = 56 * 1024 * 1024

_STAIR = ((1, 8), (2, 5), (3, 4), (4, 3), (5, 2), (6, 2), (7, 2))


_GELU_C0 = -2.0 * math.sqrt(2.0 / math.pi) * math.log2(math.e)
_GELU_C1 = 0.044715 * _GELU_C0


def _gelu(x):
    return x / (1.0 + jnp.exp2(x * (_GELU_C0 + _GELU_C1 * (x * x))))


def _sigmoid(x):
    return 1.0 / (1.0 + jnp.exp(-x))


def _rms(x, g):
    return x * lax.rsqrt(jnp.mean(x * x, axis=-1, keepdims=True) + EPS) * g


def _norm_mod(x, g, shift, scale):
    return _rms(x, g) * (1.0 + scale) + shift


def _split(x):
    hi = x.astype(BF16)
    lo = (x - hi.astype(F32)).astype(BF16)
    return hi, lo


def _dot(a, b):
    return jnp.dot(a, b, preferred_element_type=F32)


def _ada_kernel(c_ref, w_ref, b_ref, o_ref):
    cc = c_ref[...]
    s = cc * _sigmoid(cc)
    o_ref[0] = jnp.dot(s, w_ref[0], precision=lax.Precision.HIGHEST,
                       preferred_element_type=F32) + b_ref[0]


def _ada(cc, w_ada, b_ada):
    depth, _, n = w_ada.shape
    tn = 1536
    return pl.pallas_call(
        _ada_kernel,
        grid=(depth, n // tn),
        in_specs=[pl.BlockSpec((ADA_ROWS, D_MODEL), lambda l, j: (0, 0)),
                  pl.BlockSpec((1, D_MODEL, tn), lambda l, j: (l, 0, j)),
                  pl.BlockSpec((1, 1, tn), lambda l, j: (l, 0, j))],
        out_specs=pl.BlockSpec((1, ADA_ROWS, tn), lambda l, j: (l, 0, j)),
        out_shape=jax.ShapeDtypeStruct((depth, ADA_ROWS, n), F32),
        compiler_params=pltpu.CompilerParams(vmem_limit_bytes=VMEM_LIMIT),
        name="ada",
    )(cc, w_ada, b_ada.reshape(depth, 1, n))


def _even_kernel(x_ref, xp_ref, xn_ref, ada_ref, g1_ref, win_ref, wout_ref, cw_ref, vg_ref,
                 ws_ref, bsb_ref, o_ref, *, n_tiles):
    t = pl.program_id(1)
    tm = x_ref.shape[1]
    shift, scale, gate = ada_ref[0, 0:1, :], ada_ref[0, 1:2, :], ada_ref[0, 2:3, :]
    g1 = g1_ref[...]

    x = x_ref[0]
    h = _norm_mod(x, g1, shift, scale).astype(BF16)
    proj = _dot(h, win_ref[...])
    a_h, a_c, a_b = proj[:, 0:MIX_W], proj[:, MIX_W:2 * MIX_W], proj[:, 2 * MIX_W:3 * MIX_W]
    b_u, b_v = proj[:, 3 * MIX_W:4 * MIX_W], proj[:, 4 * MIX_W:5 * MIX_W]

    z = a_c * a_h

    def edge_z(ref, row):
        hh = _norm_mod(ref[0], g1, shift, scale).astype(BF16)
        pp = _dot(hh, win_ref[:, 0:2 * MIX_W])
        return (pp[:, 0:MIX_W] * pp[:, MIX_W:2 * MIX_W])[row:row + 1, :]

    zp = jnp.where(t > 0, edge_z(xp_ref, 7), 0.0)
    zn = jnp.where(t < n_tiles - 1, edge_z(xn_ref, 0), 0.0)
    row = lax.broadcasted_iota(jnp.int32, (tm, MIX_W), 0)
    z_m1 = jnp.where(row == 0, zp, pltpu.roll(z, 1, axis=0))
    z_p1 = jnp.where(row == tm - 1, zn, pltpu.roll(z, tm - 1, axis=0))
    ya = a_b * (cw_ref[0:1, :] * z_m1 + cw_ref[1:2, :] * z + cw_ref[2:3, :] * z_p1)

    u = _gelu(b_u)
    v = _rms(_gelu(b_v), vg_ref[...]).astype(BF16)
    lane = lax.broadcasted_iota(jnp.int32, (CHUNK, CHUNK), 1)
    group_w = MIX_W // B_GROUPS
    rows = []
    for n in range(tm // CHUNK):
        cols = []
        for j in range(MIX_W // CHUNK):
            vv = v[n * CHUNK:(n + 1) * CHUNK, j * CHUNK:(j + 1) * CHUNK]
            s0 = _dot(ws_ref[2 * j], vv)
            s1 = _dot(ws_ref[2 * j + 1], vv)
            cols.append(jnp.where(lane < group_w, s0, s1))
        rows.append(jnp.concatenate(cols, axis=1) + bsb_ref[...])
    s = jnp.concatenate(rows, axis=0)
    yb = u * s

    y = jnp.concatenate([ya, yb], axis=1).astype(BF16)
    o_ref[0] = x + gate * _dot(y, wout_ref[...])


def _even_mixer(x, ada, g1, win, wout, cw, vg, ws, bsb, *, tm):
    bsz, L, _ = x.shape
    n_tiles = L // tm
    hb = tm // 8
    const = lambda *shape: pl.BlockSpec(shape, lambda b, t: (0,) * len(shape))
    return pl.pallas_call(
        functools.partial(_even_kernel, n_tiles=n_tiles),
        grid=(bsz, n_tiles),
        in_specs=[pl.BlockSpec((1, tm, D_MODEL), lambda b, t: (b, t, 0)),
                  pl.BlockSpec((1, 8, D_MODEL), lambda b, t: (b, jnp.maximum(t * hb - 1, 0), 0)),
                  pl.BlockSpec((1, 8, D_MODEL),
                               lambda b, t: (b, jnp.minimum((t + 1) * hb, L // 8 - 1), 0)),
                  pl.BlockSpec((1, N_ADA, D_MODEL), lambda b, t: (b, 0, 0)),
                  const(1, D_MODEL), const(D_MODEL, IN_W), const(2 * MIX_W, D_MODEL),
                  const(A_CONV, MIX_W), const(1, MIX_W), const(B_GROUPS, CHUNK, CHUNK),
                  const(CHUNK, MIX_W)],
        out_specs=pl.BlockSpec((1, tm, D_MODEL), lambda b, t: (b, t, 0)),
        out_shape=jax.ShapeDtypeStruct(x.shape, F32),
        compiler_params=pltpu.CompilerParams(
            dimension_semantics=("parallel", "parallel"), vmem_limit_bytes=VMEM_LIMIT),
        name="even_mixer",
    )(x, x, x, ada, g1, win, wout, cw, vg, ws, bsb)


def _head_norm(t, gain, seg_ref):
    hi, lo = _split(t * t)
    ss = _dot(hi, seg_ref[...]) + _dot(lo, seg_ref[...])
    return t * lax.rsqrt(ss * (1.0 / D_HEAD_DIM) + EPS) * gain


def _odd_in_kernel(x_ref, xp_ref, xn_ref, ada_ref, g1_ref, win_ref, cw_ref, cb_ref, cg_ref,
                   qg_ref, kg_ref, cos_ref, sin_ref, seg_ref,
                   conf_ref, q_ref, kt_ref, v_ref, ybuf_ref, *, n_tiles):
    t = pl.program_id(1)
    tm = x_ref.shape[1]
    shift, scale = ada_ref[0, 0:1, :], ada_ref[0, 1:2, :]
    g1 = g1_ref[...]

    h = _norm_mod(x_ref[0], g1, shift, scale).astype(BF16)
    proj = _dot(h, win_ref[...])

    def glu_rows(ref):
        hh = _norm_mod(ref[0], g1, shift, scale).astype(BF16)
        pp = _dot(hh, win_ref[:, 0:2 * MIX_W])
        return pp[:, 0:MIX_W] * _sigmoid(pp[:, MIX_W:2 * MIX_W])

    ybuf_ref[0:CONV_HALO, :] = jnp.where(t > 0, glu_rows(xp_ref), 0.0)
    ybuf_ref[CONV_HALO:CONV_HALO + tm, :] = proj[:, 0:MIX_W] * _sigmoid(proj[:, MIX_W:2 * MIX_W])
    ybuf_ref[CONV_HALO + tm:2 * CONV_HALO + tm, :] = jnp.where(t < n_tiles - 1, glu_rows(xn_ref), 0.0)
    base = CONV_HALO - C_CONV // 2
    acc = jnp.zeros((tm, MIX_W), F32) + cb_ref[...]
    for k in range(C_CONV):
        acc = acc + cw_ref[k:k + 1, :] * ybuf_ref[base + k:base + k + tm, :]
    yn = _rms(acc, cg_ref[...])
    conf_ref[0] = (yn * _sigmoid(yn)).astype(BF16)

    lane = lax.broadcasted_iota(jnp.int32, (tm, MIX_W), 1)
    even_lane = (lane % 2) == 0

    def rope(tn):
        partner = jnp.where(even_lane, pltpu.roll(tn, MIX_W - 1, axis=1), pltpu.roll(tn, 1, axis=1))
        return tn * cos_ref[...] + partner * sin_ref[...]

    q = rope(_head_norm(proj[:, 2 * MIX_W:3 * MIX_W], qg_ref[...], seg_ref))
    k = rope(_head_norm(proj[:, 3 * MIX_W:4 * MIX_W], kg_ref[...], seg_ref))
    q_ref[0] = (q * (D_HEAD_DIM ** -0.5)).astype(BF16)
    kt_ref[0] = k.T.astype(BF16)
    v_ref[0] = proj[:, 4 * MIX_W:5 * MIX_W].astype(BF16)


def _odd_in(x, ada, g1, win, cw, cb, cg, qg, kg, cos_t, sin_t, seg, *, tm):
    bsz, L, _ = x.shape
    n_tiles = L // tm
    hb = tm // CONV_HALO
    const = lambda *shape: pl.BlockSpec(shape, lambda b, t: (0,) * len(shape))
    tok = lambda w: pl.BlockSpec((1, tm, w), lambda b, t: (b, t, 0))
    return pl.pallas_call(
        functools.partial(_odd_in_kernel, n_tiles=n_tiles),
        grid=(bsz, n_tiles),
        in_specs=[tok(D_MODEL),
                  pl.BlockSpec((1, CONV_HALO, D_MODEL),
                               lambda b, t: (b, jnp.maximum(t * hb - 1, 0), 0)),
                  pl.BlockSpec((1, CONV_HALO, D_MODEL),
                               lambda b, t: (b, jnp.minimum((t + 1) * hb, L // CONV_HALO - 1), 0)),
                  pl.BlockSpec((1, N_ADA, D_MODEL), lambda b, t: (b, 0, 0)),
                  const(1, D_MODEL), const(D_MODEL, IN_W), const(C_CONV, MIX_W),
                  const(1, MIX_W), const(1, MIX_W), const(1, MIX_W), const(1, MIX_W),
                  pl.BlockSpec((tm, MIX_W), lambda b, t: (t, 0)),
                  pl.BlockSpec((tm, MIX_W), lambda b, t: (t, 0)),
                  const(MIX_W, MIX_W)],
        out_specs=[tok(MIX_W), tok(MIX_W),
                   pl.BlockSpec((1, MIX_W, tm), lambda b, t: (b, 0, t)),
                   tok(MIX_W)],
        out_shape=[jax.ShapeDtypeStruct((bsz, L, MIX_W), BF16),
                   jax.ShapeDtypeStruct((bsz, L, MIX_W), BF16),
                   jax.ShapeDtypeStruct((bsz, MIX_W, L), BF16),
                   jax.ShapeDtypeStruct((bsz, L, MIX_W), BF16)],
        scratch_shapes=[pltpu.VMEM((tm + 2 * CONV_HALO, MIX_W), F32)],
        compiler_params=pltpu.CompilerParams(
            dimension_semantics=("parallel", "parallel"), vmem_limit_bytes=VMEM_LIMIT),
        name="odd_in",
    )(x, x, x, ada, g1, win, cw, cb, cg, qg, kg, cos_t, sin_t, seg)


def _ctx_kv_kernel(x_ref, ada_ref, g1_ref, wkv_ref, kg_ref, seg_ref, kt_ref, v_ref):
    shift, scale = ada_ref[0, 0:1, :], ada_ref[0, 1:2, :]
    h = _norm_mod(x_ref[0], g1_ref[...], shift, scale).astype(BF16)
    proj = _dot(h, wkv_ref[...])
    k = _head_norm(proj[:, 0:MIX_W], kg_ref[...], seg_ref)
    kt_ref[0] = k.T.astype(BF16)
    v_ref[0] = proj[:, MIX_W:2 * MIX_W].astype(BF16)


def _ctx_kv(ctx, ada, g1, wkv, kg, seg):
    bsz, L, _ = ctx.shape
    const = lambda *shape: pl.BlockSpec(shape, lambda b: (0,) * len(shape))
    return pl.pallas_call(
        _ctx_kv_kernel,
        grid=(bsz,),
        in_specs=[pl.BlockSpec((1, L, D_MODEL), lambda b: (b, 0, 0)),
                  pl.BlockSpec((1, N_ADA, D_MODEL), lambda b: (b, 0, 0)),
                  const(1, D_MODEL), const(D_MODEL, 2 * MIX_W), const(1, MIX_W),
                  const(MIX_W, MIX_W)],
        out_specs=[pl.BlockSpec((1, MIX_W, L), lambda b: (b, 0, 0)),
                   pl.BlockSpec((1, L, MIX_W), lambda b: (b, 0, 0))],
        out_shape=[jax.ShapeDtypeStruct((bsz, MIX_W, L), BF16),
                   jax.ShapeDtypeStruct((bsz, L, MIX_W), BF16)],
        compiler_params=pltpu.CompilerParams(
            dimension_semantics=("parallel",), vmem_limit_bytes=VMEM_LIMIT),
        name="ctx_kv",
    )(ctx, ada, g1, wkv, kg, seg)


def _attn_kernel(q_ref, ktc_ref, ktx_ref, vc_ref, vx_ref, conf_ref, x_ref, ada_ref, wout_ref,
                 lam_ref, sg_ref, o_ref, *, lam_init):
    tq = q_ref.shape[1]
    vw = 2 * D_HEAD_DIM
    lam_p = lam_ref[...]
    lam = (jnp.exp(jnp.sum(lam_p[0:1] * lam_p[1:2], axis=-1, keepdims=True))
           - jnp.exp(jnp.sum(lam_p[2:3] * lam_p[3:4], axis=-1, keepdims=True)) + lam_init)
    first_map = lax.broadcasted_iota(jnp.int32, (tq, vw), 1) < D_HEAD_DIM
    outs = []
    for hd in range(D_HEADS):
        sl = slice(hd * vw, (hd + 1) * vw)
        qh = q_ref[0, :, sl]
        probs = []
        for m in range(2):
            qm = jnp.where(first_map if m == 0 else jnp.logical_not(first_map), qh, jnp.zeros_like(qh))
            sc = _dot(qm, ktc_ref[0, sl, :])
            sx = _dot(qm, ktx_ref[0, sl, :])
            mx = jnp.maximum(jnp.max(sc, axis=-1, keepdims=True), jnp.max(sx, axis=-1, keepdims=True))
            pc, px = jnp.exp(sc - mx), jnp.exp(sx - mx)
            inv = 1.0 / (jnp.sum(pc, axis=-1, keepdims=True) + jnp.sum(px, axis=-1, keepdims=True))
            probs.append((pc * inv, px * inv))
        ac = (probs[0][0] - lam * probs[1][0]).astype(BF16)
        ax = (probs[0][1] - lam * probs[1][1]).astype(BF16)
        o = _dot(ac, vc_ref[0, :, sl]) + _dot(ax, vx_ref[0, :, sl])
        outs.append(_rms(o, sg_ref[...]) * (1.0 - lam_init))
    y = jnp.concatenate([conf_ref[0]] + [o.astype(BF16) for o in outs], axis=1)
    o_ref[0] = x_ref[0] + ada_ref[0, 2:3, :] * _dot(y, wout_ref[...])


def _attn(q, ktc, ktx, vc, vx, conf, x, ada, wout, lam_p, sg, *, lam_init, tq):
    bsz, L, _ = x.shape
    lc = vc.shape[1]
    const = lambda *shape: pl.BlockSpec(shape, lambda b, t: (0,) * len(shape))
    tok = lambda w: pl.BlockSpec((1, tq, w), lambda b, t: (b, t, 0))
    return pl.pallas_call(
        functools.partial(_attn_kernel, lam_init=lam_init),
        grid=(bsz, L // tq),
        in_specs=[tok(MIX_W),
                  pl.BlockSpec((1, MIX_W, lc), lambda b, t: (b, 0, 0)),
                  pl.BlockSpec((1, MIX_W, L), lambda b, t: (b, 0, 0)),
                  pl.BlockSpec((1, lc, MIX_W), lambda b, t: (b, 0, 0)),
                  pl.BlockSpec((1, L, MIX_W), lambda b, t: (b, 0, 0)),
                  tok(MIX_W), tok(D_MODEL),
                  pl.BlockSpec((1, N_ADA, D_MODEL), lambda b, t: (b, 0, 0)),
                  const(2 * MIX_W, D_MODEL), const(4, D_HEAD_DIM), const(1, 2 * D_HEAD_DIM)],
        out_specs=tok(D_MODEL),
        out_shape=jax.ShapeDtypeStruct(x.shape, F32),
        compiler_params=pltpu.CompilerParams(
            dimension_semantics=("parallel", "parallel"), vmem_limit_bytes=VMEM_LIMIT),
        name="diff_attn",
    )(q, ktc, ktx, vc, vx, conf, x, ada, wout, lam_p, sg)


def _exact_ranks(s, label):
    work = s
    rank = jnp.full(s.shape, float(PEER_TOPK), F32)
    for r in range(PEER_TOPK):
        m = jnp.max(work, axis=0, keepdims=True)
        first = jnp.min(jnp.where(work == m, label, 1e9), axis=0, keepdims=True)
        sel = label == first
        rank = jnp.where(sel, float(r), rank)
        work = jnp.where(sel, -jnp.inf, work)
    return rank


def _batcher_network(n):
    def merge(lo, hi, r):
        step = 2 * r
        if step < hi - lo:
            yield from merge(lo, hi, step)
            yield from merge(lo + r, hi, step)
            yield from [(i, i + r) for i in range(lo + r, hi - r, step)]
        else:
            yield (lo, lo + r)

    def sort(lo, hi):
        if hi > lo:
            mid = lo + (hi - lo) // 2
            yield from sort(lo, mid)
            yield from sort(mid + 1, hi)
            yield from merge(lo, hi, 1)

    return tuple(sort(0, n - 1))


_SORT16 = _batcher_network(PEER_TOPK)


def _sublane_allreduce(x, op):
    for shift in (4, 2, 1):
        x = op(x, pltpu.roll(x, shift, axis=0))
    return x


def _sorted_top16(rows):
    b = list(rows)
    for i, j in _SORT16:
        b[i], b[j] = jnp.maximum(b[i], b[j]), jnp.minimum(b[i], b[j])
    for shift in (4, 2, 1):
        m = [jnp.maximum(b[i], pltpu.roll(b[PEER_TOPK - 1 - i], shift, axis=0))
             for i in range(PEER_TOPK)]
        d = PEER_TOPK // 2
        while d:
            for i in range(PEER_TOPK):
                if not i & d:
                    m[i], m[i + d] = jnp.maximum(m[i], m[i + d]), jnp.minimum(m[i], m[i + d])
            d //= 2
        b = m
    return b


def _lookup_by_rank(rows, top, values):
    out = []
    for a in rows:
        r = jnp.zeros_like(a) + values[0]
        for k in range(PEER_TOPK):
            r = jnp.where(top[k] > a, values[k + 1], r)
        out.append(r)
    return jnp.concatenate(out, axis=0)


def _tie_flag(rows, top):
    flag = jnp.zeros_like(top[0])
    for k in range(PEER_TOPK - 1):
        flag = jnp.where(top[k] == top[k + 1], 1.0, flag)
    n_ge = None
    for a in rows:
        hit = jnp.where(a >= top[PEER_TOPK - 1], 1.0, 0.0)
        n_ge = hit if n_ge is None else n_ge + hit
    n_ge = _sublane_allreduce(n_ge, jnp.add)
    return jnp.where(n_ge != float(PEER_TOPK), 1.0, flag)


def _dup_bf16(v):
    bits = pltpu.bitcast(v.astype(BF16).astype(F32), jnp.uint32)
    return bits | (bits >> 16)


def _pack_bf16(v):
    n = v.shape[0] // 2
    bits = pltpu.bitcast(v.astype(BF16).astype(F32), jnp.uint32)
    return (bits[:n] >> 16) | bits[n:]


def _peer_query_kernel(x_ref, ada_ref, g2_ref, wqh_ref, wql_ref, skh_ref, skl_ref,
                       h2t_ref, r2_ref, a2_ref, n1_ref, b1_ref, qt_ref):
    tb = x_ref.shape[0]
    shift, scale = ada_ref[0, 3:4, :], ada_ref[0, 4:5, :]
    h2 = _norm_mod(x_ref[...], g2_ref[...], shift, scale)
    h2t = h2.T
    hh, hl = _split(h2t)
    h2t_ref[...] = hh
    qt_ref[...] = _dot(wqh_ref[...], hh) + (_dot(wqh_ref[...], hl) + _dot(wql_ref[...], hh))

    pos = lax.broadcasted_iota(jnp.int32, (PEER_KEYS, tb), 0)
    label1 = pos.astype(F32)
    label2 = jnp.where(pos < PEER_KEYS // 2, 2 * pos, 2 * pos - (PEER_KEYS - 1)).astype(F32)
    sub = lax.broadcasted_iota(jnp.int32, (SUBLANES, tb), 0)
    subf = sub.astype(F32)
    cand_label = ([subf, subf + 8.0] + [subf + 16.0 * k1 for k1, _ in _STAIR]
                  + [16.0 * (subf + 8.0)])
    cand_valid = [None, None] + [None if nv == SUBLANES else sub < nv for _, nv in _STAIR] + [None]

    def spread(vals):
        out = vals[0]
        for u in range(1, SUBLANES):
            out = jnp.where(sub == u, vals[u], out)
        return out

    def head(hd, carry):
        def scores(p):
            qp = qt_ref[pl.ds(pl.multiple_of((2 * hd + p) * PEER_KEYS, PEER_KEYS), PEER_KEYS), :]
            qh, ql = _split(qp)
            kh, kl = skh_ref[2 * hd + p], skl_ref[2 * hd + p]
            return _dot(kh, qh) + (_dot(kh, ql) + _dot(kl, qh))

        s1, s2 = scores(0), scores(1)
        rows1 = [s1[SUBLANES * r:SUBLANES * (r + 1)] for r in range(PEER_KEYS // SUBLANES)]
        rows2 = [s2[SUBLANES * r:SUBLANES * (r + 1)] for r in range(PEER_KEYS // SUBLANES)]
        top1, top2 = _sorted_top16(rows1), _sorted_top16(rows2)

        t2a, t2b, t1b = spread(top2[:SUBLANES]), spread(top2[SUBLANES:]), spread(top1[SUBLANES:])
        cand = ([t2a + top1[0], t2b + top1[0]] + [t2a + top1[k1] for k1, _ in _STAIR]
                + [t1b + top2[0]])
        cand = [c if v is None else jnp.where(v, c, -jnp.inf) for c, v in zip(cand, cand_valid)]
        z = jnp.zeros((SUBLANES, tb), F32)
        for r in range(PEER_TOPK):
            m = functools.reduce(jnp.maximum, cand)
            m = _sublane_allreduce(m, jnp.maximum)
            first = functools.reduce(
                jnp.minimum, [jnp.where(c == m, lab, 1e9) for c, lab in zip(cand, cand_label)])
            first = _sublane_allreduce(first, jnp.minimum)
            cand = [jnp.where(lab == first, -jnp.inf, c) for c, lab in zip(cand, cand_label)]
            z = z + jnp.exp(m - (top1[0] + top2[0]))
        taken = [jnp.where(c == -jnp.inf, 1.0, 0.0) if v is None
                 else jnp.where(jnp.logical_and(v, c == -jnp.inf), 1.0, 0.0)
                 for c, v in zip(cand, cand_valid)]
        counts = [_sublane_allreduce(taken[0] + taken[1], jnp.add)]
        counts += [_sublane_allreduce(t, jnp.add) for t in taken[2:-1]]
        counts += [_sublane_allreduce(jnp.where(sub == u, taken[-1], 0.0), jnp.add)
                   for u in range(SUBLANES)]
        counts.append(0.0)
        ranks = [float(k) for k in range(PEER_TOPK + 1)]

        def fast():
            return _lookup_by_rank(rows2, top2, ranks), _lookup_by_rank(rows1, top1, counts)

        def exact():
            rank1 = _exact_ranks(s1, label1)
            n1 = jnp.zeros_like(s1)
            for k1 in range(PEER_TOPK):
                n1 = jnp.where(rank1 == float(k1), jnp.tile(counts[k1], (PEER_KEYS // SUBLANES, 1)), n1)
            return _exact_ranks(s2, label2), n1

        ties = jnp.max(jnp.maximum(_tie_flag(rows1, top1), _tie_flag(rows2, top2)))
        rank2, n1 = lax.cond(ties > 0.0, exact, fast)
        tile = lambda v: jnp.tile(v, (PEER_KEYS // SUBLANES, 1))
        r2_ref[hd] = _pack_bf16(rank2)
        a2_ref[hd] = _pack_bf16(jnp.exp(s2 - tile(top2[0])))
        n1_ref[hd] = _dup_bf16(n1)
        b1_ref[hd] = _dup_bf16(jnp.exp(s1 - tile(top1[0])) * tile(1.0 / z))
        return carry

    lax.fori_loop(0, PEER_HEADS, head, 0)


def _peer_query(x, ada, g2, wqh, wql, skh, skl, *, tb, rows_per_ada):
    T = x.shape[0]
    qd = wqh.shape[0]
    const = lambda *shape: pl.BlockSpec(shape, lambda j: (0,) * len(shape))
    tab = lambda rows: pl.BlockSpec((PEER_HEADS, rows, tb), lambda j: (0, 0, j))
    tab_shape = lambda rows: jax.ShapeDtypeStruct((PEER_HEADS, rows, T), jnp.uint32)
    half = PEER_KEYS // 2
    return pl.pallas_call(
        _peer_query_kernel,
        grid=(T // tb,),
        in_specs=[pl.BlockSpec((tb, D_MODEL), lambda j: (j, 0)),
                  pl.BlockSpec((1, N_ADA, D_MODEL), lambda j: ((j * tb) // rows_per_ada, 0, 0)),
                  const(1, D_MODEL), const(qd, D_MODEL), const(qd, D_MODEL),
                  const(2 * PEER_HEADS, PEER_KEYS, PEER_KEYS),
                  const(2 * PEER_HEADS, PEER_KEYS, PEER_KEYS)],
        out_specs=[pl.BlockSpec((D_MODEL, tb), lambda j: (0, j)), tab(half), tab(half),
                   tab(PEER_KEYS), tab(PEER_KEYS)],
        out_shape=[jax.ShapeDtypeStruct((D_MODEL, T), BF16), tab_shape(half), tab_shape(half),
                   tab_shape(PEER_KEYS), tab_shape(PEER_KEYS)],
        scratch_shapes=[pltpu.VMEM((qd, tb), F32)],
        compiler_params=pltpu.CompilerParams(
            dimension_semantics=("parallel",), vmem_limit_bytes=VMEM_LIMIT),
        name="peer_query",
    )(x, ada, g2, wqh, wql, skh, skl)


def _peer_dense_kernel(h2t_ref, u_ref, vt_ref, r2_ref, a2_ref, n1_ref, b1_ref, x_ref, ada_ref,
                       o_ref, acc_ref, p_ref, g_ref, act_ref, *, n_et):
    i = pl.program_id(1)
    te, tb = p_ref.shape

    @pl.when(i == 0)
    def _():
        acc_ref[...] = jnp.zeros_like(acc_ref)
        p_ref[...] = jnp.zeros_like(p_ref)

    def packed_row(ref, hd, j, cs):
        word = jnp.broadcast_to(ref[hd, j:j + 1, cs], (SUBLANES, LANES))
        return pltpu.bitcast(word, BF16)

    def contract_previous():
        acc_ref[...] += _dot(vt_ref[...], p_ref[...])

    @pl.when(i < n_et)
    def _():
        act_ref[...] = _gelu(_dot(u_ref[...], h2t_ref[...])).astype(BF16)
        contract_previous()
        for j in range(te // PEER_KEYS):
            for c in range(tb // LANES):
                cs = slice(c * LANES, (c + 1) * LANES)
                cnt = [packed_row(n1_ref, hd, j, cs) for hd in range(PEER_HEADS)]
                wgt = [packed_row(b1_ref, hd, j, cs) for hd in range(PEER_HEADS)]
                for k in range(PEER_KEYS // BF16_ROWS):
                    words = slice(k * SUBLANES, (k + 1) * SUBLANES)
                    g = None
                    for hd in range(PEER_HEADS):
                        rank = pltpu.bitcast(r2_ref[hd, words, cs], BF16)
                        val = pltpu.bitcast(a2_ref[hd, words, cs], BF16)
                        term = jnp.where(rank < cnt[hd], val, jnp.zeros_like(val)) * wgt[hd]
                        g = term if g is None else g + term
                    g_ref[j * PEER_KEYS + k * BF16_ROWS:j * PEER_KEYS + (k + 1) * BF16_ROWS, cs] = g
        p_ref[...] = g_ref[...] * act_ref[...]

    @pl.when(i == n_et)
    def _():
        contract_previous()
        o_ref[...] = x_ref[...] + ada_ref[0, 5:6, :] * acc_ref[...].T


def _peer_dense(x, h2t, tabs, u, vt, ada, *, tb, te, rows_per_ada):
    T = x.shape[0]
    n_exp = u.shape[0]
    n_et = n_exp // te
    n1 = te // PEER_KEYS
    assert T % tb == 0 and rows_per_ada % tb == 0 and n_exp % te == 0 and n1 % SUBLANES == 0
    r2, a2, n1w, b1w = tabs
    tab = pl.BlockSpec((PEER_HEADS, PEER_KEYS // 2, tb), lambda j, i: (0, 0, j))
    tab1 = pl.BlockSpec((PEER_HEADS, n1, tb), lambda j, i: (0, jnp.minimum(i, n_et - 1), j))
    return pl.pallas_call(
        functools.partial(_peer_dense_kernel, n_et=n_et),
        grid=(T // tb, n_et + 1),
        in_specs=[pl.BlockSpec((D_MODEL, tb), lambda j, i: (0, j)),
                  pl.BlockSpec((te, D_MODEL), lambda j, i: (jnp.minimum(i, n_et - 1), 0)),
                  pl.BlockSpec((D_MODEL, te), lambda j, i: (0, jnp.maximum(i - 1, 0))),
                  tab, tab, tab1, tab1,
                  pl.BlockSpec((tb, D_MODEL), lambda j, i: (j, 0)),
                  pl.BlockSpec((1, N_ADA, D_MODEL), lambda j, i: ((j * tb) // rows_per_ada, 0, 0))],
        out_specs=pl.BlockSpec((tb, D_MODEL), lambda j, i: (j, 0)),
        out_shape=jax.ShapeDtypeStruct(x.shape, F32),
        scratch_shapes=[pltpu.VMEM((D_MODEL, tb), F32), pltpu.VMEM((te, tb), BF16),
                        pltpu.VMEM((te, tb), BF16), pltpu.VMEM((te, tb), BF16)],
        compiler_params=pltpu.CompilerParams(
            dimension_semantics=("parallel", "arbitrary"), vmem_limit_bytes=VMEM_LIMIT),
        name="peer_dense",
    )(h2t, u, vt, r2, a2, n1w, b1w, x, ada)


def _peer(x, ada, g2, wq, subkeys, u_tab, v_tab, *, rows_per_ada):
    wqh, wql = _split(wq.T)
    sk = jnp.stack([subkeys[:, 0], jnp.concatenate([subkeys[:, 1, 0::2], subkeys[:, 1, 1::2]], axis=1)],
                   axis=1)
    skh, skl = _split(sk.reshape(2 * PEER_HEADS, PEER_KEYS, -1))
    h2t, *tabs = _peer_query(x, ada, g2, wqh, wql, skh, skl, tb=256, rows_per_ada=rows_per_ada)
    return _peer_dense(x, h2t, tabs, u_tab.astype(BF16), v_tab.T.astype(BF16), ada,
                       tb=1024, te=1024, rows_per_ada=rows_per_ada)


def _rope_tables(L):
    pos = jnp.arange(L, dtype=jnp.int32)
    r = (pos // GRID_W).astype(F32)
    col = (pos % GRID_W).astype(F32)
    n_freq = D_HEAD_DIM // 4
    inv = ROPE_THETA ** (-jnp.arange(n_freq, dtype=F32) / n_freq)
    ang = jnp.concatenate([r[:, None] * inv, col[:, None] * inv], axis=-1)
    cos = jnp.repeat(jnp.cos(ang), 2, axis=-1)
    sin = jnp.repeat(jnp.sin(ang), 2, axis=-1) * jnp.tile(jnp.array([-1.0, 1.0], F32), D_HEAD_DIM // 2)
    reps = MIX_W // D_HEAD_DIM
    return jnp.tile(cos, (1, reps)), jnp.tile(sin, (1, reps))


def kernel(x, c, ctx, c_ctx, w_ada, b_ada, norm1_g, norm2_g, w_in, w_out, a_conv_w, b_vnorm_g,
           b_spatial_w, b_spatial_b, c_conv_w, c_conv_b, c_norm_g, d_qk_norm_g, d_lambda,
           d_subln_g, peer_wq, peer_subkeys, peer_u, peer_v):
    bsz, L, _ = x.shape
    lc = ctx.shape[1]
    depth = w_ada.shape[0]
    assert depth == 2 and bsz + 1 <= ADA_ROWS

    cc = jnp.zeros((ADA_ROWS, D_MODEL), F32).at[:bsz].set(c).at[bsz].set(c_ctx)
    ada = _ada(cc, w_ada, b_ada).reshape(depth, ADA_ROWS, N_ADA, D_MODEL)
    row = lambda v: v.reshape(1, -1)

    ada_x, ada_c = ada[0, :bsz], ada[0, bsz:bsz + 1]
    bsb = jnp.repeat(b_spatial_b[0].T, MIX_W // B_GROUPS, axis=1)
    even = functools.partial(
        _even_mixer, g1=row(norm1_g[0]), win=w_in[0].astype(BF16), wout=w_out[0].astype(BF16),
        cw=a_conv_w[0], vg=row(b_vnorm_g[0]), ws=b_spatial_w[0].astype(BF16), bsb=bsb)
    peer0 = functools.partial(_peer, g2=row(norm2_g[0]), wq=peer_wq[0], subkeys=peer_subkeys[0],
                              u_tab=peer_u[0], v_tab=peer_v[0])
    x = even(x, ada_x, tm=512)
    x = peer0(x.reshape(bsz * L, D_MODEL), ada_x, rows_per_ada=L).reshape(bsz, L, D_MODEL)
    ctx = even(ctx, jnp.broadcast_to(ada_c, (bsz, N_ADA, D_MODEL)), tm=lc)
    ctx = peer0(ctx.reshape(bsz * lc, D_MODEL), ada_c, rows_per_ada=bsz * lc).reshape(bsz, lc, D_MODEL)

    ada_x, ada_c = ada[1, :bsz], ada[1, bsz:bsz + 1]
    lam_init = 0.8 - 0.6 * math.exp(-0.3 * 1)
    win = w_in[1].astype(BF16)
    cos_t, sin_t = _rope_tables(L)
    seg_id = jnp.arange(MIX_W, dtype=jnp.int32) // D_HEAD_DIM
    seg = (seg_id[:, None] == seg_id[None, :]).astype(BF16)
    reps = MIX_W // D_HEAD_DIM
    qg, kg = row(jnp.tile(d_qk_norm_g[0, 0], reps)), row(jnp.tile(d_qk_norm_g[0, 1], reps))
    conf, q, ktx, vx = _odd_in(x, ada_x, row(norm1_g[1]), win, c_conv_w[0], row(c_conv_b[0]),
                               row(c_norm_g[0]), qg, kg, cos_t, sin_t, seg, tm=512)
    ktc, vc = _ctx_kv(ctx, jnp.broadcast_to(ada_c, (bsz, N_ADA, D_MODEL)), row(norm1_g[1]),
                      win[:, 3 * MIX_W:], kg, seg)
    x = _attn(q, ktc, ktx, vc, vx, conf, x, ada_x, w_out[1].astype(BF16), d_lambda[0],
              row(d_subln_g[0]), lam_init=lam_init, tq=256)
    x = _peer(x.reshape(bsz * L, D_MODEL), ada_x, row(norm2_g[1]), peer_wq[1], peer_subkeys[1],
              peer_u[1], peer_v[1], rows_per_ada=L)
    return x.reshape(bsz, L, D_MODEL)
```

```python
import functools
import math

import jax
import jax.numpy as jnp
from jax import lax
from jax.experimental import pallas as pl
from jax.experimental.pallas import tpu as pltpu

F32 = jnp.float32
BF16 = jnp.bfloat16

D_MODEL = 1024
MIX_W = 512
IN_W = 5 * MIX_W
GRID_W = 64
CHUNK = 128
B_GROUPS = 8
A_CONV = 3
C_CONV = 31
D_HEADS = 4
D_HEAD_DIM = 64
ROPE_THETA = 10000.0
N_ADA = 6
PEER_HEADS = 8
PEER_KEYS = 128
PEER_TOPK = 16
EPS = 1e-6

SUBLANES, LANES = 8, 128
BF16_ROWS = 2 * SUBLANES
GATE_ROWS = 2

ADA_ROWS = 16
CONV_HALO = 16
VMEM_LIMIT = 56 * 1024 * 1024

_STAIR = ((1, 8), (2, 5), (3, 4), (4, 3), (5, 2), (6, 2), (7, 2))


LOG2E = math.log2(math.e)
_GELU_C0 = -2.0 * math.sqrt(2.0 / math.pi) * LOG2E
_GELU_C1 = 0.044715 * _GELU_C0


def _gelu(x):
    return x / (1.0 + jnp.exp2(x * (_GELU_C0 + _GELU_C1 * (x * x))))


def _sigmoid(x):
    return 1.0 / (1.0 + jnp.exp(-x))


def _rms(x, g):
    return x * lax.rsqrt(jnp.mean(x * x, axis=-1, keepdims=True) + EPS) * g


def _norm_mod(x, g, shift, scale):
    return _rms(x, g) * (1.0 + scale) + shift


def _split(x):
    hi = x.astype(BF16)
    lo = (x - hi.astype(F32)).astype(BF16)
    return hi, lo


def _dot(a, b):
    return jnp.dot(a, b, preferred_element_type=F32)


def _ada_kernel(c_ref, w_ref, b_ref, o_ref):
    cc = c_ref[...]
    s = cc * _sigmoid(cc)
    o_ref[0] = jnp.dot(s, w_ref[0], precision=lax.Precision.HIGHEST,
                       preferred_element_type=F32) + b_ref[0]


def _ada(cc, w_ada, b_ada):
    depth, _, n = w_ada.shape
    tn = 1536
    return pl.pallas_call(
        _ada_kernel,
        grid=(depth, n // tn),
        in_specs=[pl.BlockSpec((ADA_ROWS, D_MODEL), lambda l, j: (0, 0)),
                  pl.BlockSpec((1, D_MODEL, tn), lambda l, j: (l, 0, j)),
                  pl.BlockSpec((1, 1, tn), lambda l, j: (l, 0, j))],
        out_specs=pl.BlockSpec((1, ADA_ROWS, tn), lambda l, j: (l, 0, j)),
        out_shape=jax.ShapeDtypeStruct((depth, ADA_ROWS, n), F32),
        compiler_params=pltpu.CompilerParams(vmem_limit_bytes=VMEM_LIMIT),
        name="ada",
    )(cc, w_ada, b_ada.reshape(depth, 1, n))


def _even_kernel(x_ref, xp_ref, xn_ref, ada_ref, g1_ref, win_ref, wout_ref, cw_ref, vg_ref,
                 ws_ref, bsb_ref, o_ref, *, n_tiles):
    t = pl.program_id(1)
    tm = x_ref.shape[1]
    shift, scale, gate = ada_ref[0, 0:1, :], ada_ref[0, 1:2, :], ada_ref[0, 2:3, :]
    g1 = g1_ref[...]

    x = x_ref[0]
    h = _norm_mod(x, g1, shift, scale).astype(BF16)
    proj = _dot(h, win_ref[...])
    a_h, a_c, a_b = proj[:, 0:MIX_W], proj[:, MIX_W:2 * MIX_W], proj[:, 2 * MIX_W:3 * MIX_W]
    b_u, b_v = proj[:, 3 * MIX_W:4 * MIX_W], proj[:, 4 * MIX_W:5 * MIX_W]

    z = a_c * a_h

    def edge_z(ref, row):
        hh = _norm_mod(ref[0], g1, shift, scale).astype(BF16)
        pp = _dot(hh, win_ref[:, 0:2 * MIX_W])
        return (pp[:, 0:MIX_W] * pp[:, MIX_W:2 * MIX_W])[row:row + 1, :]

    zp = jnp.where(t > 0, edge_z(xp_ref, 7), 0.0)
    zn = jnp.where(t < n_tiles - 1, edge_z(xn_ref, 0), 0.0)
    row = lax.broadcasted_iota(jnp.int32, (tm, MIX_W), 0)
    z_m1 = jnp.where(row == 0, zp, pltpu.roll(z, 1, axis=0))
    z_p1 = jnp.where(row == tm - 1, zn, pltpu.roll(z, tm - 1, axis=0))
    ya = a_b * (cw_ref[0:1, :] * z_m1 + cw_ref[1:2, :] * z + cw_ref[2:3, :] * z_p1)

    u = _gelu(b_u)
    v = _rms(_gelu(b_v), vg_ref[...]).astype(BF16)
    lane = lax.broadcasted_iota(jnp.int32, (CHUNK, CHUNK), 1)
    group_w = MIX_W // B_GROUPS
    rows = []
    for n in range(tm // CHUNK):
        cols = []
        for j in range(MIX_W // CHUNK):
            vv = v[n * CHUNK:(n + 1) * CHUNK, j * CHUNK:(j + 1) * CHUNK]
            s0 = _dot(ws_ref[2 * j], vv)
            s1 = _dot(ws_ref[2 * j + 1], vv)
            cols.append(jnp.where(lane < group_w, s0, s1))
        rows.append(jnp.concatenate(cols, axis=1) + bsb_ref[...])
    s = jnp.concatenate(rows, axis=0)
    yb = u * s

    y = jnp.concatenate([ya, yb], axis=1).astype(BF16)
    o_ref[0] = x + gate * _dot(y, wout_ref[...])


def _even_mixer(x, ada, g1, win, wout, cw, vg, ws, bsb, *, tm):
    bsz, L, _ = x.shape
    n_tiles = L // tm
    hb = tm // 8
    const = lambda *shape: pl.BlockSpec(shape, lambda b, t: (0,) * len(shape))
    return pl.pallas_call(
        functools.partial(_even_kernel, n_tiles=n_tiles),
        grid=(bsz, n_tiles),
        in_specs=[pl.BlockSpec((1, tm, D_MODEL), lambda b, t: (b, t, 0)),
                  pl.BlockSpec((1, 8, D_MODEL), lambda b, t: (b, jnp.maximum(t * hb - 1, 0), 0)),
                  pl.BlockSpec((1, 8, D_MODEL),
                               lambda b, t: (b, jnp.minimum((t + 1) * hb, L // 8 - 1), 0)),
                  pl.BlockSpec((1, N_ADA, D_MODEL), lambda b, t: (b, 0, 0)),
                  const(1, D_MODEL), const(D_MODEL, IN_W), const(2 * MIX_W, D_MODEL),
                  const(A_CONV, MIX_W), const(1, MIX_W), const(B_GROUPS, CHUNK, CHUNK),
                  const(CHUNK, MIX_W)],
        out_specs=pl.BlockSpec((1, tm, D_MODEL), lambda b, t: (b, t, 0)),
        out_shape=jax.ShapeDtypeStruct(x.shape, F32),
        compiler_params=pltpu.CompilerParams(
            dimension_semantics=("parallel", "parallel"), vmem_limit_bytes=VMEM_LIMIT),
        name="even_mixer",
    )(x, x, x, ada, g1, win, wout, cw, vg, ws, bsb)


def _head_norm(t, gain, seg_ref):
    hi, lo = _split(t * t)
    ss = _dot(hi, seg_ref[...]) + _dot(lo, seg_ref[...])
    return t * lax.rsqrt(ss * (1.0 / D_HEAD_DIM) + EPS) * gain


def _odd_in_kernel(x_ref, xp_ref, xn_ref, ada_ref, g1_ref, win_ref, cw_ref, cb_ref, cg_ref,
                   qg_ref, kg_ref, cos_ref, sin_ref, seg_ref,
                   conf_ref, q_ref, kt_ref, v_ref, ybuf_ref, *, n_tiles):
    t = pl.program_id(1)
    tm = x_ref.shape[1]
    shift, scale = ada_ref[0, 0:1, :], ada_ref[0, 1:2, :]
    g1 = g1_ref[...]

    h = _norm_mod(x_ref[0], g1, shift, scale).astype(BF16)
    proj = _dot(h, win_ref[...])

    def glu_rows(ref):
        hh = _norm_mod(ref[0], g1, shift, scale).astype(BF16)
        pp = _dot(hh, win_ref[:, 0:2 * MIX_W])
        return pp[:, 0:MIX_W] * _sigmoid(pp[:, MIX_W:2 * MIX_W])

    ybuf_ref[0:CONV_HALO, :] = jnp.where(t > 0, glu_rows(xp_ref), 0.0)
    ybuf_ref[CONV_HALO:CONV_HALO + tm, :] = proj[:, 0:MIX_W] * _sigmoid(proj[:, MIX_W:2 * MIX_W])
    ybuf_ref[CONV_HALO + tm:2 * CONV_HALO + tm, :] = jnp.where(t < n_tiles - 1, glu_rows(xn_ref), 0.0)
    base = CONV_HALO - C_CONV // 2
    acc = jnp.zeros((tm, MIX_W), F32) + cb_ref[...]
    for k in range(C_CONV):
        acc = acc + cw_ref[k:k + 1, :] * ybuf_ref[base + k:base + k + tm, :]
    yn = _rms(acc, cg_ref[...])
    conf_ref[0] = (yn * _sigmoid(yn)).astype(BF16)

    lane = lax.broadcasted_iota(jnp.int32, (tm, MIX_W), 1)
    even_lane = (lane % 2) == 0

    def rope(tn):
        partner = jnp.where(even_lane, pltpu.roll(tn, MIX_W - 1, axis=1), pltpu.roll(tn, 1, axis=1))
        return tn * cos_ref[...] + partner * sin_ref[...]

    q = rope(_head_norm(proj[:, 2 * MIX_W:3 * MIX_W], qg_ref[...], seg_ref))
    k = rope(_head_norm(proj[:, 3 * MIX_W:4 * MIX_W], kg_ref[...], seg_ref))
    q_ref[0] = (q * (D_HEAD_DIM ** -0.5 * LOG2E)).astype(BF16)
    kt_ref[0] = k.T.astype(BF16)
    v_ref[0] = proj[:, 4 * MIX_W:5 * MIX_W].astype(BF16)


def _odd_in(x, ada, g1, win, cw, cb, cg, qg, kg, cos_t, sin_t, seg, *, tm):
    bsz, L, _ = x.shape
    n_tiles = L // tm
    hb = tm // CONV_HALO
    const = lambda *shape: pl.BlockSpec(shape, lambda b, t: (0,) * len(shape))
    tok = lambda w: pl.BlockSpec((1, tm, w), lambda b, t: (b, t, 0))
    return pl.pallas_call(
        functools.partial(_odd_in_kernel, n_tiles=n_tiles),
        grid=(bsz, n_tiles),
        in_specs=[tok(D_MODEL),
                  pl.BlockSpec((1, CONV_HALO, D_MODEL),
                               lambda b, t: (b, jnp.maximum(t * hb - 1, 0), 0)),
                  pl.BlockSpec((1, CONV_HALO, D_MODEL),
                               lambda b, t: (b, jnp.minimum((t + 1) * hb, L // CONV_HALO - 1), 0)),
                  pl.BlockSpec((1, N_ADA, D_MODEL), lambda b, t: (b, 0, 0)),
                  const(1, D_MODEL), const(D_MODEL, IN_W), const(C_CONV, MIX_W),
                  const(1, MIX_W), const(1, MIX_W), const(1, MIX_W), const(1, MIX_W),
                  pl.BlockSpec((tm, MIX_W), lambda b, t: (t, 0)),
                  pl.BlockSpec((tm, MIX_W), lambda b, t: (t, 0)),
                  const(MIX_W, MIX_W)],
        out_specs=[tok(MIX_W), tok(MIX_W),
                   pl.BlockSpec((1, MIX_W, tm), lambda b, t: (b, 0, t)),
                   tok(MIX_W)],
        out_shape=[jax.ShapeDtypeStruct((bsz, L, MIX_W), BF16),
                   jax.ShapeDtypeStruct((bsz, L, MIX_W), BF16),
                   jax.ShapeDtypeStruct((bsz, MIX_W, L), BF16),
                   jax.ShapeDtypeStruct((bsz, L, MIX_W), BF16)],
        scratch_shapes=[pltpu.VMEM((tm + 2 * CONV_HALO, MIX_W), F32)],
        compiler_params=pltpu.CompilerParams(
            dimension_semantics=("parallel", "parallel"), vmem_limit_bytes=VMEM_LIMIT),
        name="odd_in",
    )(x, x, x, ada, g1, win, cw, cb, cg, qg, kg, cos_t, sin_t, seg)


def _ctx_kv_kernel(x_ref, ada_ref, g1_ref, wkv_ref, kg_ref, seg_ref, kt_ref, v_ref):
    shift, scale = ada_ref[0, 0:1, :], ada_ref[0, 1:2, :]
    h = _norm_mod(x_ref[0], g1_ref[...], shift, scale).astype(BF16)
    proj = _dot(h, wkv_ref[...])
    k = _head_norm(proj[:, 0:MIX_W], kg_ref[...], seg_ref)
    kt_ref[0] = k.T.astype(BF16)
    v_ref[0] = proj[:, MIX_W:2 * MIX_W].astype(BF16)


def _ctx_kv(ctx, ada, g1, wkv, kg, seg):
    bsz, L, _ = ctx.shape
    const = lambda *shape: pl.BlockSpec(shape, lambda b: (0,) * len(shape))
    return pl.pallas_call(
        _ctx_kv_kernel,
        grid=(bsz,),
        in_specs=[pl.BlockSpec((1, L, D_MODEL), lambda b: (b, 0, 0)),
                  pl.BlockSpec((1, N_ADA, D_MODEL), lambda b: (b, 0, 0)),
                  const(1, D_MODEL), const(D_MODEL, 2 * MIX_W), const(1, MIX_W),
                  const(MIX_W, MIX_W)],
        out_specs=[pl.BlockSpec((1, MIX_W, L), lambda b: (b, 0, 0)),
                   pl.BlockSpec((1, L, MIX_W), lambda b: (b, 0, 0))],
        out_shape=[jax.ShapeDtypeStruct((bsz, MIX_W, L), BF16),
                   jax.ShapeDtypeStruct((bsz, L, MIX_W), BF16)],
        compiler_params=pltpu.CompilerParams(
            dimension_semantics=("parallel",), vmem_limit_bytes=VMEM_LIMIT),
        name="ctx_kv",
    )(ctx, ada, g1, wkv, kg, seg)


def _attn_kernel(q_ref, ktc_ref, ktx_ref, vc_ref, vx_ref, conf_ref, x_ref, ada_ref, wout_ref,
                 lam_ref, sg_ref, o_ref, *, lam_init):
    tq = q_ref.shape[1]
    vw = 2 * D_HEAD_DIM
    lam_p = lam_ref[...]
    lam = (jnp.exp(jnp.sum(lam_p[0:1] * lam_p[1:2], axis=-1, keepdims=True))
           - jnp.exp(jnp.sum(lam_p[2:3] * lam_p[3:4], axis=-1, keepdims=True)) + lam_init)
    first_map = lax.broadcasted_iota(jnp.int32, (tq, vw), 1) < D_HEAD_DIM
    outs = []
    for hd in range(D_HEADS):
        sl = slice(hd * vw, (hd + 1) * vw)
        qh = q_ref[0, :, sl]
        maps = []
        for m in range(2):
            qm = jnp.where(first_map if m == 0 else jnp.logical_not(first_map), qh, jnp.zeros_like(qh))
            sc = _dot(qm, ktc_ref[0, sl, :])
            sx = _dot(qm, ktx_ref[0, sl, :])
            mx = jnp.maximum(jnp.max(sc, axis=-1, keepdims=True), jnp.max(sx, axis=-1, keepdims=True))
            pc, px = jnp.exp2(sc - mx), jnp.exp2(sx - mx)
            inv = 1.0 / (jnp.sum(pc, axis=-1, keepdims=True) + jnp.sum(px, axis=-1, keepdims=True))
            pv = _dot(pc.astype(BF16), vc_ref[0, :, sl]) + _dot(px.astype(BF16), vx_ref[0, :, sl])
            maps.append(pv * inv)
        o = maps[0] - lam * maps[1]
        outs.append(_rms(o, sg_ref[...]) * (1.0 - lam_init))
    y = jnp.concatenate([conf_ref[0]] + [o.astype(BF16) for o in outs], axis=1)
    o_ref[0] = x_ref[0] + ada_ref[0, 2:3, :] * _dot(y, wout_ref[...])


def _attn(q, ktc, ktx, vc, vx, conf, x, ada, wout, lam_p, sg, *, lam_init, tq):
    bsz, L, _ = x.shape
    lc = vc.shape[1]
    const = lambda *shape: pl.BlockSpec(shape, lambda b, t: (0,) * len(shape))
    tok = lambda w: pl.BlockSpec((1, tq, w), lambda b, t: (b, t, 0))
    return pl.pallas_call(
        functools.partial(_attn_kernel, lam_init=lam_init),
        grid=(bsz, L // tq),
        in_specs=[tok(MIX_W),
                  pl.BlockSpec((1, MIX_W, lc), lambda b, t: (b, 0, 0)),
                  pl.BlockSpec((1, MIX_W, L), lambda b, t: (b, 0, 0)),
                  pl.BlockSpec((1, lc, MIX_W), lambda b, t: (b, 0, 0)),
                  pl.BlockSpec((1, L, MIX_W), lambda b, t: (b, 0, 0)),
                  tok(MIX_W), tok(D_MODEL),
                  pl.BlockSpec((1, N_ADA, D_MODEL), lambda b, t: (b, 0, 0)),
                  const(2 * MIX_W, D_MODEL), const(4, D_HEAD_DIM), const(1, 2 * D_HEAD_DIM)],
        out_specs=tok(D_MODEL),
        out_shape=jax.ShapeDtypeStruct(x.shape, F32),
        compiler_params=pltpu.CompilerParams(
            dimension_semantics=("parallel", "parallel"), vmem_limit_bytes=VMEM_LIMIT),
        name="diff_attn",
    )(q, ktc, ktx, vc, vx, conf, x, ada, wout, lam_p, sg)


def _exact_ranks(s, label):
    work = s
    rank = jnp.full(s.shape, float(PEER_TOPK), F32)
    for r in range(PEER_TOPK):
        m = jnp.max(work, axis=0, keepdims=True)
        first = jnp.min(jnp.where(work == m, label, 1e9), axis=0, keepdims=True)
        sel = label == first
        rank = jnp.where(sel, float(r), rank)
        work = jnp.where(sel, -jnp.inf, work)
    return rank


def _batcher_network(n):
    def merge(lo, hi, r):
        step = 2 * r
        if step < hi - lo:
            yield from merge(lo, hi, step)
            yield from merge(lo + r, hi, step)
            yield from [(i, i + r) for i in range(lo + r, hi - r, step)]
        else:
            yield (lo, lo + r)

    def sort(lo, hi):
        if hi > lo:
            mid = lo + (hi - lo) // 2
            yield from sort(lo, mid)
            yield from sort(mid + 1, hi)
            yield from merge(lo, hi, 1)

    return tuple(sort(0, n - 1))


_SORT16 = _batcher_network(PEER_TOPK)


def _sublane_allreduce(x, op):
    for shift in (4, 2, 1):
        x = op(x, pltpu.roll(x, shift, axis=0))
    return x


def _sorted_top16(rows):
    b = list(rows)
    for i, j in _SORT16:
        b[i], b[j] = jnp.maximum(b[i], b[j]), jnp.minimum(b[i], b[j])
    for shift in (4, 2, 1):
        m = [jnp.maximum(b[i], pltpu.roll(b[PEER_TOPK - 1 - i], shift, axis=0))
             for i in range(PEER_TOPK)]
        d = PEER_TOPK // 2
        while d:
            for i in range(PEER_TOPK):
                if not i & d:
                    m[i], m[i + d] = jnp.maximum(m[i], m[i + d]), jnp.minimum(m[i], m[i + d])
            d //= 2
        b = m
    return b


def _lookup_by_rank(rows, top, values):
    out = []
    for a in rows:
        r = jnp.zeros_like(a) + values[0]
        for k in range(PEER_TOPK):
            r = jnp.where(top[k] > a, values[k + 1], r)
        out.append(r)
    return jnp.concatenate(out, axis=0)


def _tie_flag(rows, top):
    flag = jnp.zeros_like(top[0])
    for k in range(PEER_TOPK - 1):
        flag = jnp.where(top[k] == top[k + 1], 1.0, flag)
    n_ge = None
    for a in rows:
        hit = jnp.where(a >= top[PEER_TOPK - 1], 1.0, 0.0)
        n_ge = hit if n_ge is None else n_ge + hit
    n_ge = _sublane_allreduce(n_ge, jnp.add)
    return jnp.where(n_ge != float(PEER_TOPK), 1.0, flag)


def _dup_bf16(v):
    bits = pltpu.bitcast(v.astype(BF16).astype(F32), jnp.uint32)
    return bits | (bits >> 16)


def _pack_bf16(v):
    n = v.shape[0] // 2
    bits = pltpu.bitcast(v.astype(BF16).astype(F32), jnp.uint32)
    return (bits[:n] >> 16) | bits[n:]


def _peer_query_kernel(x_ref, ada_ref, g2_ref, wqh_ref, wql_ref, skh_ref, skl_ref,
                       h2t_ref, r2_ref, a2_ref, n1_ref, b1_ref, qt_ref):
    tb = x_ref.shape[0]
    shift, scale = ada_ref[0, 3:4, :], ada_ref[0, 4:5, :]
    h2 = _norm_mod(x_ref[...], g2_ref[...], shift, scale)
    h2t = h2.T
    hh, hl = _split(h2t)
    h2t_ref[...] = hh
    qt_ref[...] = _dot(wqh_ref[...], hh) + (_dot(wqh_ref[...], hl) + _dot(wql_ref[...], hh))

    pos = lax.broadcasted_iota(jnp.int32, (PEER_KEYS, tb), 0)
    label1 = pos.astype(F32)
    label2 = jnp.where(pos < PEER_KEYS // 2, 2 * pos, 2 * pos - (PEER_KEYS - 1)).astype(F32)
    sub = lax.broadcasted_iota(jnp.int32, (SUBLANES, tb), 0)
    subf = sub.astype(F32)
    cand_label = ([subf, subf + 8.0] + [subf + 16.0 * k1 for k1, _ in _STAIR]
                  + [16.0 * (subf + 8.0)])
    cand_valid = [None, None] + [None if nv == SUBLANES else sub < nv for _, nv in _STAIR] + [None]

    def spread(vals):
        out = vals[0]
        for u in range(1, SUBLANES):
            out = jnp.where(sub == u, vals[u], out)
        return out

    def head(hd, carry):
        def scores(p):
            qp = qt_ref[pl.ds(pl.multiple_of((2 * hd + p) * PEER_KEYS, PEER_KEYS), PEER_KEYS), :]
            qh, ql = _split(qp)
            kh, kl = skh_ref[2 * hd + p], skl_ref[2 * hd + p]
            return _dot(kh, qh) + (_dot(kh, ql) + _dot(kl, qh))

        s1, s2 = scores(0), scores(1)
        rows1 = [s1[SUBLANES * r:SUBLANES * (r + 1)] for r in range(PEER_KEYS // SUBLANES)]
        rows2 = [s2[SUBLANES * r:SUBLANES * (r + 1)] for r in range(PEER_KEYS // SUBLANES)]
        top1, top2 = _sorted_top16(rows1), _sorted_top16(rows2)

        t2a, t2b, t1b = spread(top2[:SUBLANES]), spread(top2[SUBLANES:]), spread(top1[SUBLANES:])
        cand = ([t2a + top1[0], t2b + top1[0]] + [t2a + top1[k1] for k1, _ in _STAIR]
                + [t1b + top2[0]])
        cand = [c if v is None else jnp.where(v, c, -jnp.inf) for c, v in zip(cand, cand_valid)]
        z = jnp.zeros((SUBLANES, tb), F32)
        for r in range(PEER_TOPK):
            m = functools.reduce(jnp.maximum, cand)
            m = _sublane_allreduce(m, jnp.maximum)
            first = functools.reduce(
                jnp.minimum, [jnp.where(c == m, lab, 1e9) for c, lab in zip(cand, cand_label)])
            first = _sublane_allreduce(first, jnp.minimum)
            cand = [jnp.where(lab == first, -jnp.inf, c) for c, lab in zip(cand, cand_label)]
            z = z + jnp.exp(m - (top1[0] + top2[0]))
        taken = [jnp.where(c == -jnp.inf, 1.0, 0.0) if v is None
                 else jnp.where(jnp.logical_and(v, c == -jnp.inf), 1.0, 0.0)
                 for c, v in zip(cand, cand_valid)]
        counts = [_sublane_allreduce(taken[0] + taken[1], jnp.add)]
        counts += [_sublane_allreduce(t, jnp.add) for t in taken[2:-1]]
        counts += [_sublane_allreduce(jnp.where(sub == u, taken[-1], 0.0), jnp.add)
                   for u in range(SUBLANES)]
        counts.append(0.0)
        ranks = [float(k) for k in range(PEER_TOPK + 1)]

        def fast():
            return _lookup_by_rank(rows2, top2, ranks), _lookup_by_rank(rows1, top1, counts)

        def exact():
            rank1 = _exact_ranks(s1, label1)
            n1 = jnp.zeros_like(s1)
            for k1 in range(PEER_TOPK):
                n1 = jnp.where(rank1 == float(k1), jnp.tile(counts[k1], (PEER_KEYS // SUBLANES, 1)), n1)
            return _exact_ranks(s2, label2), n1

        ties = jnp.max(jnp.maximum(_tie_flag(rows1, top1), _tie_flag(rows2, top2)))
        rank2, n1 = lax.cond(ties > 0.0, exact, fast)
        tile = lambda v: jnp.tile(v, (PEER_KEYS // SUBLANES, 1))
        r2_ref[hd] = _pack_bf16(rank2)
        a2_ref[hd] = _pack_bf16(jnp.exp(s2 - tile(top2[0])))
        n1_ref[hd] = _dup_bf16(n1)
        b1_ref[hd] = _dup_bf16(jnp.exp(s1 - tile(top1[0])) * tile(1.0 / z))
        return carry

    lax.fori_loop(0, PEER_HEADS, head, 0)


def _peer_query(x, ada, g2, wqh, wql, skh, skl, *, tb, rows_per_ada):
    T = x.shape[0]
    qd = wqh.shape[0]
    const = lambda *shape: pl.BlockSpec(shape, lambda j: (0,) * len(shape))
    tab = lambda rows: pl.BlockSpec((PEER_HEADS, rows, tb), lambda j: (0, 0, j))
    tab_shape = lambda rows: jax.ShapeDtypeStruct((PEER_HEADS, rows, T), jnp.uint32)
    half = PEER_KEYS // 2
    return pl.pallas_call(
        _peer_query_kernel,
        grid=(T // tb,),
        in_specs=[pl.BlockSpec((tb, D_MODEL), lambda j: (j, 0)),
                  pl.BlockSpec((1, N_ADA, D_MODEL), lambda j: ((j * tb) // rows_per_ada, 0, 0)),
                  const(1, D_MODEL), const(qd, D_MODEL), const(qd, D_MODEL),
                  const(2 * PEER_HEADS, PEER_KEYS, PEER_KEYS),
                  const(2 * PEER_HEADS, PEER_KEYS, PEER_KEYS)],
        out_specs=[pl.BlockSpec((D_MODEL, tb), lambda j: (0, j)), tab(half), tab(half),
                   tab(PEER_KEYS), tab(PEER_KEYS)],
        out_shape=[jax.ShapeDtypeStruct((D_MODEL, T), BF16), tab_shape(half), tab_shape(half),
                   tab_shape(PEER_KEYS), tab_shape(PEER_KEYS)],
        scratch_shapes=[pltpu.VMEM((qd, tb), F32)],
        compiler_params=pltpu.CompilerParams(
            dimension_semantics=("parallel",), vmem_limit_bytes=VMEM_LIMIT),
        name="peer_query",
    )(x, ada, g2, wqh, wql, skh, skl)


def _peer_dense_kernel(h2t_ref, u_ref, vt_ref, r2_ref, a2_ref, n1_ref, b1_ref, x_ref, ada_ref,
                       o_ref, acc_ref, p_ref, g_ref, act_ref, *, n_et):
    i = pl.program_id(1)
    te, tb = p_ref.shape

    @pl.when(i == 0)
    def _():
        acc_ref[...] = jnp.zeros_like(acc_ref)
        p_ref[...] = jnp.zeros_like(p_ref)

    def packed_row(ref, hd, j, cs):
        word = jnp.broadcast_to(ref[hd, j:j + 1, cs], (SUBLANES, LANES))
        return pltpu.bitcast(word, BF16)

    def contract_previous():
        acc_ref[...] += _dot(vt_ref[...], p_ref[...])

    @pl.when(i < n_et)
    def _():
        act_ref[...] = _gelu(_dot(u_ref[...], h2t_ref[...])).astype(BF16)
        contract_previous()
        slabs = PEER_KEYS // BF16_ROWS
        for c in range(tb // LANES):
            cs = slice(c * LANES, (c + 1) * LANES)
            for j0 in range(0, te // PEER_KEYS, GATE_ROWS):
                g = [[None] * slabs for _ in range(GATE_ROWS)]
                for hd in range(PEER_HEADS):
                    cnt = [packed_row(n1_ref, hd, j0 + jj, cs) for jj in range(GATE_ROWS)]
                    wgt = [packed_row(b1_ref, hd, j0 + jj, cs) for jj in range(GATE_ROWS)]
                    for k in range(slabs):
                        words = slice(k * SUBLANES, (k + 1) * SUBLANES)
                        rank = pltpu.bitcast(r2_ref[hd, words, cs], BF16)
                        val = pltpu.bitcast(a2_ref[hd, words, cs], BF16)
                        for jj in range(GATE_ROWS):
                            term = jnp.where(rank < cnt[jj], val, jnp.zeros_like(val)) * wgt[jj]
                            g[jj][k] = term if hd == 0 else g[jj][k] + term
                for jj in range(GATE_ROWS):
                    for k in range(slabs):
                        r0 = (j0 + jj) * PEER_KEYS + k * BF16_ROWS
                        g_ref[r0:r0 + BF16_ROWS, cs] = g[jj][k] * act_ref[r0:r0 + BF16_ROWS, cs]
        p_ref[...] = g_ref[...]

    @pl.when(i == n_et)
    def _():
        contract_previous()
        o_ref[...] = x_ref[...] + ada_ref[0, 5:6, :] * acc_ref[...].T


def _peer_dense(x, h2t, tabs, u, vt, ada, *, tb, te, rows_per_ada):
    T = x.shape[0]
    n_exp = u.shape[0]
    n_et = n_exp // te
    n1 = te // PEER_KEYS
    assert T % tb == 0 and rows_per_ada % tb == 0 and n_exp % te == 0 and n1 % SUBLANES == 0
    r2, a2, n1w, b1w = tabs
    tab = pl.BlockSpec((PEER_HEADS, PEER_KEYS // 2, tb), lambda j, i: (0, 0, j))
    tab1 = pl.BlockSpec((PEER_HEADS, n1, tb), lambda j, i: (0, jnp.minimum(i, n_et - 1), j))
    return pl.pallas_call(
        functools.partial(_peer_dense_kernel, n_et=n_et),
        grid=(T // tb, n_et + 1),
        in_specs=[pl.BlockSpec((D_MODEL, tb), lambda j, i: (0, j)),
                  pl.BlockSpec((te, D_MODEL), lambda j, i: (jnp.minimum(i, n_et - 1), 0)),
                  pl.BlockSpec((D_MODEL, te), lambda j, i: (0, jnp.maximum(i - 1, 0))),
                  tab, tab, tab1, tab1,
                  pl.BlockSpec((tb, D_MODEL), lambda j, i: (j, 0)),
                  pl.BlockSpec((1, N_ADA, D_MODEL), lambda j, i: ((j * tb) // rows_per_ada, 0, 0))],
        out_specs=pl.BlockSpec((tb, D_MODEL), lambda j, i: (j, 0)),
        out_shape=jax.ShapeDtypeStruct(x.shape, F32),
        scratch_shapes=[pltpu.VMEM((D_MODEL, tb), F32), pltpu.VMEM((te, tb), BF16),
                        pltpu.VMEM((te, tb), BF16), pltpu.VMEM((te, tb), BF16)],
        compiler_params=pltpu.CompilerParams(
            dimension_semantics=("parallel", "arbitrary"), vmem_limit_bytes=VMEM_LIMIT),
        name="peer_dense",
    )(h2t, u, vt, r2, a2, n1w, b1w, x, ada)


def _peer(x, ada, g2, wq, subkeys, u_tab, v_tab, *, rows_per_ada):
    wqh, wql = _split(wq.T)
    sk = jnp.stack([subkeys[:, 0], jnp.concatenate([subkeys[:, 1, 0::2], subkeys[:, 1, 1::2]], axis=1)],
                   axis=1)
    skh, skl = _split(sk.reshape(2 * PEER_HEADS, PEER_KEYS, -1))
    h2t, *tabs = _peer_query(x, ada, g2, wqh, wql, skh, skl, tb=256, rows_per_ada=rows_per_ada)
    return _peer_dense(x, h2t, tabs, u_tab.astype(BF16), v_tab.T.astype(BF16), ada,
                       tb=1024, te=1024, rows_per_ada=rows_per_ada)


def _rope_tables(L):
    pos = jnp.arange(L, dtype=jnp.int32)
    r = (pos // GRID_W).astype(F32)
    col = (pos % GRID_W).astype(F32)
    n_freq = D_HEAD_DIM // 4
    inv = ROPE_THETA ** (-jnp.arange(n_freq, dtype=F32) / n_freq)
    ang = jnp.concatenate([r[:, None] * inv, col[:, None] * inv], axis=-1)
    cos = jnp.repeat(jnp.cos(ang), 2, axis=-1)
    sin = jnp.repeat(jnp.sin(ang), 2, axis=-1) * jnp.tile(jnp.array([-1.0, 1.0], F32), D_HEAD_DIM // 2)
    reps = MIX_W // D_HEAD_DIM
    return jnp.tile(cos, (1, reps)), jnp.tile(sin, (1, reps))


def kernel(x, c, ctx, c_ctx, w_ada, b_ada, norm1_g, norm2_g, w_in, w_out, a_conv_w, b_vnorm_g,
           b_spatial_w, b_spatial_b, c_conv_w, c_conv_b, c_norm_g, d_qk_norm_g, d_lambda,
           d_subln_g, peer_wq, peer_subkeys, peer_u, peer_v):
    bsz, L, _ = x.shape
    lc = ctx.shape[1]
    depth = w_ada.shape[0]
    assert depth == 2 and bsz + 1 <= ADA_ROWS

    cc = jnp.zeros((ADA_ROWS, D_MODEL), F32).at[:bsz].set(c).at[bsz].set(c_ctx)
    ada = _ada(cc, w_ada, b_ada).reshape(depth, ADA_ROWS, N_ADA, D_MODEL)
    row = lambda v: v.reshape(1, -1)

    ada_x, ada_c = ada[0, :bsz], ada[0, bsz:bsz + 1]
    bsb = jnp.repeat(b_spatial_b[0].T, MIX_W // B_GROUPS, axis=1)
    even = functools.partial(
        _even_mixer, g1=row(norm1_g[0]), win=w_in[0].astype(BF16), wout=w_out[0].astype(BF16),
        cw=a_conv_w[0], vg=row(b_vnorm_g[0]), ws=b_spatial_w[0].astype(BF16), bsb=bsb)
    peer0 = functools.partial(_peer, g2=row(norm2_g[0]), wq=peer_wq[0], subkeys=peer_subkeys[0],
                              u_tab=peer_u[0], v_tab=peer_v[0])
    x = even(x, ada_x, tm=512)
    x = peer0(x.reshape(bsz * L, D_MODEL), ada_x, rows_per_ada=L).reshape(bsz, L, D_MODEL)
    ctx = even(ctx, jnp.broadcast_to(ada_c, (bsz, N_ADA, D_MODEL)), tm=lc)
    ctx = peer0(ctx.reshape(bsz * lc, D_MODEL), ada_c, rows_per_ada=bsz * lc).reshape(bsz, lc, D_MODEL)

    ada_x, ada_c = ada[1, :bsz], ada[1, bsz:bsz + 1]
    lam_init = 0.8 - 0.6 * math.exp(-0.3 * 1)
    win = w_in[1].astype(BF16)
    cos_t, sin_t = _rope_tables(L)
    seg_id = jnp.arange(MIX_W, dtype=jnp.int32) // D_HEAD_DIM
    seg = (seg_id[:, None] == seg_id[None, :]).astype(BF16)
    reps = MIX_W // D_HEAD_DIM
    qg, kg = row(jnp.tile(d_qk_norm_g[0, 0], reps)), row(jnp.tile(d_qk_norm_g[0, 1], reps))
    conf, q, ktx, vx = _odd_in(x, ada_x, row(norm1_g[1]), win, c_conv_w[0], row(c_conv_b[0]),
                               row(c_norm_g[0]), qg, kg, cos_t, sin_t, seg, tm=512)
    ktc, vc = _ctx_kv(ctx, jnp.broadcast_to(ada_c, (bsz, N_ADA, D_MODEL)), row(norm1_g[1]),
                      win[:, 3 * MIX_W:], kg, seg)
    x = _attn(q, ktc, ktx, vc, vx, conf, x, ada_x, w_out[1].astype(BF16), d_lambda[0],
              row(d_subln_g[0]), lam_init=lam_init, tq=512)
    x = _peer(x.reshape(bsz * L, D_MODEL), ada_x, row(norm2_g[1]), peer_wq[1], peer_subkeys[1],
              peer_u[1], peer_v[1], rows_per_ada=L)
    return x.reshape(bsz, L, D_MODEL)
```

```python
import functools
import math

import jax
import jax.numpy as jnp
from jax import lax
from jax.experimental import pallas as pl
from jax.experimental.pallas import tpu as pltpu

F32 = jnp.float32
BF16 = jnp.bfloat16

D_MODEL = 1024
MIX_W = 512
IN_W = 5 * MIX_W
GRID_W = 64
CHUNK = 128
B_GROUPS = 8
A_CONV = 3
C_CONV = 31
D_HEADS = 4
D_HEAD_DIM = 64
ROPE_THETA = 10000.0
N_ADA = 6
PEER_HEADS = 8
PEER_KEYS = 128
PEER_TOPK = 16
EPS = 1e-6

SUBLANES, LANES = 8, 128
BF16_ROWS = 2 * SUBLANES
GATE_CHUNKS = 4
GATE_SLABS = 4

ADA_ROWS = 16
CONV_HALO = 16
VMEM_LIMIT = 56 * 1024 * 1024

_STAIR = ((1, 8), (2, 5), (3, 4), (4, 3), (5, 2), (6, 2), (7, 2))


LOG2E = math.log2(math.e)
_GELU_C0 = -2.0 * math.sqrt(2.0 / math.pi) * LOG2E
_GELU_C1 = 0.044715 * _GELU_C0


def _gelu(x):
    return x / (1.0 + jnp.exp2(x * (_GELU_C0 + _GELU_C1 * (x * x))))


def _sigmoid(x):
    return 1.0 / (1.0 + jnp.exp(-x))


def _rms(x, g):
    return x * lax.rsqrt(jnp.mean(x * x, axis=-1, keepdims=True) + EPS) * g


def _norm_mod(x, g, shift, scale):
    return _rms(x, g) * (1.0 + scale) + shift


def _split(x):
    hi = x.astype(BF16)
    lo = (x - hi.astype(F32)).astype(BF16)
    return hi, lo


def _dot(a, b):
    return jnp.dot(a, b, preferred_element_type=F32)


def _ada_kernel(c_ref, w_ref, b_ref, o_ref):
    cc = c_ref[...]
    s = cc * _sigmoid(cc)
    o_ref[0] = jnp.dot(s, w_ref[0], precision=lax.Precision.HIGHEST,
                       preferred_element_type=F32) + b_ref[0]


def _ada(cc, w_ada, b_ada):
    depth, _, n = w_ada.shape
    tn = 1536
    return pl.pallas_call(
        _ada_kernel,
        grid=(depth, n // tn),
        in_specs=[pl.BlockSpec((ADA_ROWS, D_MODEL), lambda l, j: (0, 0)),
                  pl.BlockSpec((1, D_MODEL, tn), lambda l, j: (l, 0, j)),
                  pl.BlockSpec((1, 1, tn), lambda l, j: (l, 0, j))],
        out_specs=pl.BlockSpec((1, ADA_ROWS, tn), lambda l, j: (l, 0, j)),
        out_shape=jax.ShapeDtypeStruct((depth, ADA_ROWS, n), F32),
        compiler_params=pltpu.CompilerParams(vmem_limit_bytes=VMEM_LIMIT),
        name="ada",
    )(cc, w_ada, b_ada.reshape(depth, 1, n))


def _even_kernel(x_ref, xp_ref, xn_ref, ada_ref, g1_ref, win_ref, wout_ref, cw_ref, vg_ref,
                 ws_ref, bsb_ref, o_ref, *, n_tiles):
    t = pl.program_id(1)
    tm = x_ref.shape[1]
    shift, scale, gate = ada_ref[0, 0:1, :], ada_ref[0, 1:2, :], ada_ref[0, 2:3, :]
    g1 = g1_ref[...]

    x = x_ref[0]
    h = _norm_mod(x, g1, shift, scale).astype(BF16)
    proj = _dot(h, win_ref[...])
    a_h, a_c, a_b = proj[:, 0:MIX_W], proj[:, MIX_W:2 * MIX_W], proj[:, 2 * MIX_W:3 * MIX_W]
    b_u, b_v = proj[:, 3 * MIX_W:4 * MIX_W], proj[:, 4 * MIX_W:5 * MIX_W]

    z = a_c * a_h

    def edge_z(ref, row):
        hh = _norm_mod(ref[0], g1, shift, scale).astype(BF16)
        pp = _dot(hh, win_ref[:, 0:2 * MIX_W])
        return (pp[:, 0:MIX_W] * pp[:, MIX_W:2 * MIX_W])[row:row + 1, :]

    zp = jnp.where(t > 0, edge_z(xp_ref, 7), 0.0)
    zn = jnp.where(t < n_tiles - 1, edge_z(xn_ref, 0), 0.0)
    row = lax.broadcasted_iota(jnp.int32, (tm, MIX_W), 0)
    z_m1 = jnp.where(row == 0, zp, pltpu.roll(z, 1, axis=0))
    z_p1 = jnp.where(row == tm - 1, zn, pltpu.roll(z, tm - 1, axis=0))
    ya = a_b * (cw_ref[0:1, :] * z_m1 + cw_ref[1:2, :] * z + cw_ref[2:3, :] * z_p1)

    u = _gelu(b_u)
    v = _rms(_gelu(b_v), vg_ref[...]).astype(BF16)
    lane = lax.broadcasted_iota(jnp.int32, (CHUNK, CHUNK), 1)
    group_w = MIX_W // B_GROUPS
    rows = []
    for n in range(tm // CHUNK):
        cols = []
        for j in range(MIX_W // CHUNK):
            vv = v[n * CHUNK:(n + 1) * CHUNK, j * CHUNK:(j + 1) * CHUNK]
            s0 = _dot(ws_ref[2 * j], vv)
            s1 = _dot(ws_ref[2 * j + 1], vv)
            cols.append(jnp.where(lane < group_w, s0, s1))
        rows.append(jnp.concatenate(cols, axis=1) + bsb_ref[...])
    s = jnp.concatenate(rows, axis=0)
    yb = u * s

    y = jnp.concatenate([ya, yb], axis=1).astype(BF16)
    o_ref[0] = x + gate * _dot(y, wout_ref[...])


def _even_mixer(x, ada, g1, win, wout, cw, vg, ws, bsb, *, tm):
    bsz, L, _ = x.shape
    n_tiles = L // tm
    hb = tm // 8
    const = lambda *shape: pl.BlockSpec(shape, lambda b, t: (0,) * len(shape))
    return pl.pallas_call(
        functools.partial(_even_kernel, n_tiles=n_tiles),
        grid=(bsz, n_tiles),
        in_specs=[pl.BlockSpec((1, tm, D_MODEL), lambda b, t: (b, t, 0)),
                  pl.BlockSpec((1, 8, D_MODEL), lambda b, t: (b, jnp.maximum(t * hb - 1, 0), 0)),
                  pl.BlockSpec((1, 8, D_MODEL),
                               lambda b, t: (b, jnp.minimum((t + 1) * hb, L // 8 - 1), 0)),
                  pl.BlockSpec((1, N_ADA, D_MODEL), lambda b, t: (b, 0, 0)),
                  const(1, D_MODEL), const(D_MODEL, IN_W), const(2 * MIX_W, D_MODEL),
                  const(A_CONV, MIX_W), const(1, MIX_W), const(B_GROUPS, CHUNK, CHUNK),
                  const(CHUNK, MIX_W)],
        out_specs=pl.BlockSpec((1, tm, D_MODEL), lambda b, t: (b, t, 0)),
        out_shape=jax.ShapeDtypeStruct(x.shape, F32),
        compiler_params=pltpu.CompilerParams(
            dimension_semantics=("parallel", "parallel"), vmem_limit_bytes=VMEM_LIMIT),
        name="even_mixer",
    )(x, x, x, ada, g1, win, wout, cw, vg, ws, bsb)


def _head_norm(t, gain, seg_ref):
    hi, lo = _split(t * t)
    ss = _dot(hi, seg_ref[...]) + _dot(lo, seg_ref[...])
    return t * lax.rsqrt(ss * (1.0 / D_HEAD_DIM) + EPS) * gain


def _odd_in_kernel(x_ref, xp_ref, xn_ref, ada_ref, g1_ref, win_ref, cw_ref, cb_ref, cg_ref,
                   qg_ref, kg_ref, cos_ref, sin_ref, seg_ref,
                   conf_ref, q_ref, kt_ref, v_ref, ybuf_ref, *, n_tiles):
    t = pl.program_id(1)
    tm = x_ref.shape[1]
    shift, scale = ada_ref[0, 0:1, :], ada_ref[0, 1:2, :]
    g1 = g1_ref[...]

    h = _norm_mod(x_ref[0], g1, shift, scale).astype(BF16)
    proj = _dot(h, win_ref[...])

    def glu_rows(ref):
        hh = _norm_mod(ref[0], g1, shift, scale).astype(BF16)
        pp = _dot(hh, win_ref[:, 0:2 * MIX_W])
        return pp[:, 0:MIX_W] * _sigmoid(pp[:, MIX_W:2 * MIX_W])

    ybuf_ref[0:CONV_HALO, :] = jnp.where(t > 0, glu_rows(xp_ref), 0.0)
    ybuf_ref[CONV_HALO:CONV_HALO + tm, :] = proj[:, 0:MIX_W] * _sigmoid(proj[:, MIX_W:2 * MIX_W])
    ybuf_ref[CONV_HALO + tm:2 * CONV_HALO + tm, :] = jnp.where(t < n_tiles - 1, glu_rows(xn_ref), 0.0)
    base = CONV_HALO - C_CONV // 2
    acc = jnp.zeros((tm, MIX_W), F32) + cb_ref[...]
    for k in range(C_CONV):
        acc = acc + cw_ref[k:k + 1, :] * ybuf_ref[base + k:base + k + tm, :]
    yn = _rms(acc, cg_ref[...])
    conf_ref[0] = (yn * _sigmoid(yn)).astype(BF16)

    lane = lax.broadcasted_iota(jnp.int32, (tm, MIX_W), 1)
    even_lane = (lane % 2) == 0

    def rope(tn):
        partner = jnp.where(even_lane, pltpu.roll(tn, MIX_W - 1, axis=1), pltpu.roll(tn, 1, axis=1))
        return tn * cos_ref[...] + partner * sin_ref[...]

    q = rope(_head_norm(proj[:, 2 * MIX_W:3 * MIX_W], qg_ref[...], seg_ref))
    k = rope(_head_norm(proj[:, 3 * MIX_W:4 * MIX_W], kg_ref[...], seg_ref))
    q_ref[0] = (q * (D_HEAD_DIM ** -0.5 * LOG2E)).astype(BF16)
    kt_ref[0] = k.T.astype(BF16)
    v_ref[0] = proj[:, 4 * MIX_W:5 * MIX_W].astype(BF16)


def _odd_in(x, ada, g1, win, cw, cb, cg, qg, kg, cos_t, sin_t, seg, *, tm):
    bsz, L, _ = x.shape
    n_tiles = L // tm
    hb = tm // CONV_HALO
    const = lambda *shape: pl.BlockSpec(shape, lambda b, t: (0,) * len(shape))
    tok = lambda w: pl.BlockSpec((1, tm, w), lambda b, t: (b, t, 0))
    return pl.pallas_call(
        functools.partial(_odd_in_kernel, n_tiles=n_tiles),
        grid=(bsz, n_tiles),
        in_specs=[tok(D_MODEL),
                  pl.BlockSpec((1, CONV_HALO, D_MODEL),
                               lambda b, t: (b, jnp.maximum(t * hb - 1, 0), 0)),
                  pl.BlockSpec((1, CONV_HALO, D_MODEL),
                               lambda b, t: (b, jnp.minimum((t + 1) * hb, L // CONV_HALO - 1), 0)),
                  pl.BlockSpec((1, N_ADA, D_MODEL), lambda b, t: (b, 0, 0)),
                  const(1, D_MODEL), const(D_MODEL, IN_W), const(C_CONV, MIX_W),
                  const(1, MIX_W), const(1, MIX_W), const(1, MIX_W), const(1, MIX_W),
                  pl.BlockSpec((tm, MIX_W), lambda b, t: (t, 0)),
                  pl.BlockSpec((tm, MIX_W), lambda b, t: (t, 0)),
                  const(MIX_W, MIX_W)],
        out_specs=[tok(MIX_W), tok(MIX_W),
                   pl.BlockSpec((1, MIX_W, tm), lambda b, t: (b, 0, t)),
                   tok(MIX_W)],
        out_shape=[jax.ShapeDtypeStruct((bsz, L, MIX_W), BF16),
                   jax.ShapeDtypeStruct((bsz, L, MIX_W), BF16),
                   jax.ShapeDtypeStruct((bsz, MIX_W, L), BF16),
                   jax.ShapeDtypeStruct((bsz, L, MIX_W), BF16)],
        scratch_shapes=[pltpu.VMEM((tm + 2 * CONV_HALO, MIX_W), F32)],
        compiler_params=pltpu.CompilerParams(
            dimension_semantics=("parallel", "parallel"), vmem_limit_bytes=VMEM_LIMIT),
        name="odd_in",
    )(x, x, x, ada, g1, win, cw, cb, cg, qg, kg, cos_t, sin_t, seg)


def _ctx_kv_kernel(x_ref, ada_ref, g1_ref, wkv_ref, kg_ref, seg_ref, kt_ref, v_ref):
    shift, scale = ada_ref[0, 0:1, :], ada_ref[0, 1:2, :]
    h = _norm_mod(x_ref[0], g1_ref[...], shift, scale).astype(BF16)
    proj = _dot(h, wkv_ref[...])
    k = _head_norm(proj[:, 0:MIX_W], kg_ref[...], seg_ref)
    kt_ref[0] = k.T.astype(BF16)
    v_ref[0] = proj[:, MIX_W:2 * MIX_W].astype(BF16)


def _ctx_kv(ctx, ada, g1, wkv, kg, seg):
    bsz, L, _ = ctx.shape
    const = lambda *shape: pl.BlockSpec(shape, lambda b: (0,) * len(shape))
    return pl.pallas_call(
        _ctx_kv_kernel,
        grid=(bsz,),
        in_specs=[pl.BlockSpec((1, L, D_MODEL), lambda b: (b, 0, 0)),
                  pl.BlockSpec((1, N_ADA, D_MODEL), lambda b: (b, 0, 0)),
                  const(1, D_MODEL), const(D_MODEL, 2 * MIX_W), const(1, MIX_W),
                  const(MIX_W, MIX_W)],
        out_specs=[pl.BlockSpec((1, MIX_W, L), lambda b: (b, 0, 0)),
                   pl.BlockSpec((1, L, MIX_W), lambda b: (b, 0, 0))],
        out_shape=[jax.ShapeDtypeStruct((bsz, MIX_W, L), BF16),
                   jax.ShapeDtypeStruct((bsz, L, MIX_W), BF16)],
        compiler_params=pltpu.CompilerParams(
            dimension_semantics=("parallel",), vmem_limit_bytes=VMEM_LIMIT),
        name="ctx_kv",
    )(ctx, ada, g1, wkv, kg, seg)


def _attn_kernel(q_ref, ktc_ref, ktx_ref, vc_ref, vx_ref, conf_ref, x_ref, ada_ref, wout_ref,
                 lam_ref, sg_ref, o_ref, *, lam_init):
    tq = q_ref.shape[1]
    vw = 2 * D_HEAD_DIM
    lam_p = lam_ref[...]
    lam = (jnp.exp(jnp.sum(lam_p[0:1] * lam_p[1:2], axis=-1, keepdims=True))
           - jnp.exp(jnp.sum(lam_p[2:3] * lam_p[3:4], axis=-1, keepdims=True)) + lam_init)
    first_map = lax.broadcasted_iota(jnp.int32, (tq, vw), 1) < D_HEAD_DIM
    outs = []
    for hd in range(D_HEADS):
        sl = slice(hd * vw, (hd + 1) * vw)
        qh = q_ref[0, :, sl]
        maps = []
        for m in range(2):
            qm = jnp.where(first_map if m == 0 else jnp.logical_not(first_map), qh, jnp.zeros_like(qh))
            sc = _dot(qm, ktc_ref[0, sl, :])
            sx = _dot(qm, ktx_ref[0, sl, :])
            mx = jnp.maximum(jnp.max(sc, axis=-1, keepdims=True), jnp.max(sx, axis=-1, keepdims=True))
            pc, px = jnp.exp2(sc - mx), jnp.exp2(sx - mx)
            inv = 1.0 / (jnp.sum(pc, axis=-1, keepdims=True) + jnp.sum(px, axis=-1, keepdims=True))
            pv = _dot(pc.astype(BF16), vc_ref[0, :, sl]) + _dot(px.astype(BF16), vx_ref[0, :, sl])
            maps.append(pv * inv)
        o = maps[0] - lam * maps[1]
        outs.append(_rms(o, sg_ref[...]) * (1.0 - lam_init))
    y = jnp.concatenate([conf_ref[0]] + [o.astype(BF16) for o in outs], axis=1)
    o_ref[0] = x_ref[0] + ada_ref[0, 2:3, :] * _dot(y, wout_ref[...])


def _attn(q, ktc, ktx, vc, vx, conf, x, ada, wout, lam_p, sg, *, lam_init, tq):
    bsz, L, _ = x.shape
    lc = vc.shape[1]
    const = lambda *shape: pl.BlockSpec(shape, lambda b, t: (0,) * len(shape))
    tok = lambda w: pl.BlockSpec((1, tq, w), lambda b, t: (b, t, 0))
    return pl.pallas_call(
        functools.partial(_attn_kernel, lam_init=lam_init),
        grid=(bsz, L // tq),
        in_specs=[tok(MIX_W),
                  pl.BlockSpec((1, MIX_W, lc), lambda b, t: (b, 0, 0)),
                  pl.BlockSpec((1, MIX_W, L), lambda b, t: (b, 0, 0)),
                  pl.BlockSpec((1, lc, MIX_W), lambda b, t: (b, 0, 0)),
                  pl.BlockSpec((1, L, MIX_W), lambda b, t: (b, 0, 0)),
                  tok(MIX_W), tok(D_MODEL),
                  pl.BlockSpec((1, N_ADA, D_MODEL), lambda b, t: (b, 0, 0)),
                  const(2 * MIX_W, D_MODEL), const(4, D_HEAD_DIM), const(1, 2 * D_HEAD_DIM)],
        out_specs=tok(D_MODEL),
        out_shape=jax.ShapeDtypeStruct(x.shape, F32),
        compiler_params=pltpu.CompilerParams(
            dimension_semantics=("parallel", "parallel"), vmem_limit_bytes=VMEM_LIMIT),
        name="diff_attn",
    )(q, ktc, ktx, vc, vx, conf, x, ada, wout, lam_p, sg)


def _exact_ranks(s, label):
    work = s
    rank = jnp.full(s.shape, float(PEER_TOPK), F32)
    for r in range(PEER_TOPK):
        m = jnp.max(work, axis=0, keepdims=True)
        first = jnp.min(jnp.where(work == m, label, 1e9), axis=0, keepdims=True)
        sel = label == first
        rank = jnp.where(sel, float(r), rank)
        work = jnp.where(sel, -jnp.inf, work)
    return rank


def _batcher_network(n):
    def merge(lo, hi, r):
        step = 2 * r
        if step < hi - lo:
            yield from merge(lo, hi, step)
            yield from merge(lo + r, hi, step)
            yield from [(i, i + r) for i in range(lo + r, hi - r, step)]
        else:
            yield (lo, lo + r)

    def sort(lo, hi):
        if hi > lo:
            mid = lo + (hi - lo) // 2
            yield from sort(lo, mid)
            yield from sort(mid + 1, hi)
            yield from merge(lo, hi, 1)

    return tuple(sort(0, n - 1))


_SORT16 = _batcher_network(PEER_TOPK)


def _sublane_allreduce(x, op):
    for shift in (4, 2, 1):
        x = op(x, pltpu.roll(x, shift, axis=0))
    return x


def _sorted_top16(rows):
    b = list(rows)
    for i, j in _SORT16:
        b[i], b[j] = jnp.maximum(b[i], b[j]), jnp.minimum(b[i], b[j])
    for shift in (4, 2, 1):
        m = [jnp.maximum(b[i], pltpu.roll(b[PEER_TOPK - 1 - i], shift, axis=0))
             for i in range(PEER_TOPK)]
        d = PEER_TOPK // 2
        while d:
            for i in range(PEER_TOPK):
                if not i & d:
                    m[i], m[i + d] = jnp.maximum(m[i], m[i + d]), jnp.minimum(m[i], m[i + d])
            d //= 2
        b = m
    return b


def _lookup_by_rank(rows, top, values):
    out = []
    for a in rows:
        r = jnp.zeros_like(a) + values[0]
        for k in range(PEER_TOPK):
            r = jnp.where(top[k] > a, values[k + 1], r)
        out.append(r)
    return jnp.concatenate(out, axis=0)


def _tie_flag(rows, top):
    flag = jnp.zeros_like(top[0])
    for k in range(PEER_TOPK - 1):
        flag = jnp.where(top[k] == top[k + 1], 1.0, flag)
    n_ge = None
    for a in rows:
        hit = jnp.where(a >= top[PEER_TOPK - 1], 1.0, 0.0)
        n_ge = hit if n_ge is None else n_ge + hit
    n_ge = _sublane_allreduce(n_ge, jnp.add)
    return jnp.where(n_ge != float(PEER_TOPK), 1.0, flag)


def _dup_bf16(v):
    bits = pltpu.bitcast(v.astype(BF16).astype(F32), jnp.uint32)
    return bits | (bits >> 16)


def _pack_bf16(v):
    n = v.shape[0] // 2
    bits = pltpu.bitcast(v.astype(BF16).astype(F32), jnp.uint32)
    return (bits[:n] >> 16) | bits[n:]


def _peer_query_kernel(x_ref, ada_ref, g2_ref, wqh_ref, wql_ref, skh_ref, skl_ref,
                       h2t_ref, r2_ref, a2_ref, n1_ref, b1_ref, qt_ref):
    tb = x_ref.shape[0]
    shift, scale = ada_ref[0, 3:4, :], ada_ref[0, 4:5, :]
    h2 = _norm_mod(x_ref[...], g2_ref[...], shift, scale)
    h2t = h2.T
    hh, hl = _split(h2t)
    h2t_ref[...] = hh
    qt_ref[...] = _dot(wqh_ref[...], hh) + (_dot(wqh_ref[...], hl) + _dot(wql_ref[...], hh))

    pos = lax.broadcasted_iota(jnp.int32, (PEER_KEYS, tb), 0)
    label1 = pos.astype(F32)
    label2 = jnp.where(pos < PEER_KEYS // 2, 2 * pos, 2 * pos - (PEER_KEYS - 1)).astype(F32)
    sub = lax.broadcasted_iota(jnp.int32, (SUBLANES, tb), 0)
    subf = sub.astype(F32)
    cand_label = ([subf, subf + 8.0] + [subf + 16.0 * k1 for k1, _ in _STAIR]
                  + [16.0 * (subf + 8.0)])
    cand_valid = [None, None] + [None if nv == SUBLANES else sub < nv for _, nv in _STAIR] + [None]

    def spread(vals):
        out = vals[0]
        for u in range(1, SUBLANES):
            out = jnp.where(sub == u, vals[u], out)
        return out

    def head(hd, carry):
        def scores(p):
            qp = qt_ref[pl.ds(pl.multiple_of((2 * hd + p) * PEER_KEYS, PEER_KEYS), PEER_KEYS), :]
            qh, ql = _split(qp)
            kh, kl = skh_ref[2 * hd + p], skl_ref[2 * hd + p]
            return _dot(kh, qh) + (_dot(kh, ql) + _dot(kl, qh))

        s1, s2 = scores(0), scores(1)
        rows1 = [s1[SUBLANES * r:SUBLANES * (r + 1)] for r in range(PEER_KEYS // SUBLANES)]
        rows2 = [s2[SUBLANES * r:SUBLANES * (r + 1)] for r in range(PEER_KEYS // SUBLANES)]
        top1, top2 = _sorted_top16(rows1), _sorted_top16(rows2)

        t2a, t2b, t1b = spread(top2[:SUBLANES]), spread(top2[SUBLANES:]), spread(top1[SUBLANES:])
        cand = ([t2a + top1[0], t2b + top1[0]] + [t2a + top1[k1] for k1, _ in _STAIR]
                + [t1b + top2[0]])
        cand = [c if v is None else jnp.where(v, c, -jnp.inf) for c, v in zip(cand, cand_valid)]
        z = jnp.zeros((SUBLANES, tb), F32)
        for r in range(PEER_TOPK):
            m = functools.reduce(jnp.maximum, cand)
            m = _sublane_allreduce(m, jnp.maximum)
            first = functools.reduce(
                jnp.minimum, [jnp.where(c == m, lab, 1e9) for c, lab in zip(cand, cand_label)])
            first = _sublane_allreduce(first, jnp.minimum)
            cand = [jnp.where(lab == first, -jnp.inf, c) for c, lab in zip(cand, cand_label)]
            z = z + jnp.exp(m - (top1[0] + top2[0]))
        taken = [jnp.where(c == -jnp.inf, 1.0, 0.0) if v is None
                 else jnp.where(jnp.logical_and(v, c == -jnp.inf), 1.0, 0.0)
                 for c, v in zip(cand, cand_valid)]
        counts = [_sublane_allreduce(taken[0] + taken[1], jnp.add)]
        counts += [_sublane_allreduce(t, jnp.add) for t in taken[2:-1]]
        counts += [_sublane_allreduce(jnp.where(sub == u, taken[-1], 0.0), jnp.add)
                   for u in range(SUBLANES)]
        counts.append(0.0)
        ranks = [float(k) for k in range(PEER_TOPK + 1)]

        def fast():
            return _lookup_by_rank(rows2, top2, ranks), _lookup_by_rank(rows1, top1, counts)

        def exact():
            rank1 = _exact_ranks(s1, label1)
            n1 = jnp.zeros_like(s1)
            for k1 in range(PEER_TOPK):
                n1 = jnp.where(rank1 == float(k1), jnp.tile(counts[k1], (PEER_KEYS // SUBLANES, 1)), n1)
            return _exact_ranks(s2, label2), n1

        ties = jnp.max(jnp.maximum(_tie_flag(rows1, top1), _tie_flag(rows2, top2)))
        rank2, n1 = lax.cond(ties > 0.0, exact, fast)
        tile = lambda v: jnp.tile(v, (PEER_KEYS // SUBLANES, 1))
        r2_ref[hd] = _pack_bf16(rank2)
        a2_ref[hd] = _pack_bf16(jnp.exp(s2 - tile(top2[0])))
        n1_ref[hd] = _dup_bf16(n1)
        b1_ref[hd] = _dup_bf16(jnp.exp(s1 - tile(top1[0])) * tile(1.0 / z))
        return carry

    lax.fori_loop(0, PEER_HEADS, head, 0)


def _peer_query(x, ada, g2, wqh, wql, skh, skl, *, tb, rows_per_ada):
    T = x.shape[0]
    qd = wqh.shape[0]
    const = lambda *shape: pl.BlockSpec(shape, lambda j: (0,) * len(shape))
    tab = lambda rows: pl.BlockSpec((PEER_HEADS, rows, tb), lambda j: (0, 0, j))
    tab_shape = lambda rows: jax.ShapeDtypeStruct((PEER_HEADS, rows, T), jnp.uint32)
    half = PEER_KEYS // 2
    return pl.pallas_call(
        _peer_query_kernel,
        grid=(T // tb,),
        in_specs=[pl.BlockSpec((tb, D_MODEL), lambda j: (j, 0)),
                  pl.BlockSpec((1, N_ADA, D_MODEL), lambda j: ((j * tb) // rows_per_ada, 0, 0)),
                  const(1, D_MODEL), const(qd, D_MODEL), const(qd, D_MODEL),
                  const(2 * PEER_HEADS, PEER_KEYS, PEER_KEYS),
                  const(2 * PEER_HEADS, PEER_KEYS, PEER_KEYS)],
        out_specs=[pl.BlockSpec((D_MODEL, tb), lambda j: (0, j)), tab(half), tab(half),
                   tab(PEER_KEYS), tab(PEER_KEYS)],
        out_shape=[jax.ShapeDtypeStruct((D_MODEL, T), BF16), tab_shape(half), tab_shape(half),
                   tab_shape(PEER_KEYS), tab_shape(PEER_KEYS)],
        scratch_shapes=[pltpu.VMEM((qd, tb), F32)],
        compiler_params=pltpu.CompilerParams(
            dimension_semantics=("parallel",), vmem_limit_bytes=VMEM_LIMIT),
        name="peer_query",
    )(x, ada, g2, wqh, wql, skh, skl)


def _peer_dense_kernel(h2t_ref, u_ref, vt_ref, r2_ref, a2_ref, n1_ref, b1_ref, x_ref, ada_ref,
                       o_ref, acc_ref, p_ref, act_ref, *, n_et):
    i = pl.program_id(1)
    te, tb = p_ref.shape

    @pl.when(i == 0)
    def _():
        acc_ref[...] = jnp.zeros_like(acc_ref)
        p_ref[...] = jnp.zeros_like(p_ref)

    def packed_row(ref, hd, j, cs):
        word = jnp.broadcast_to(ref[hd, j:j + 1, cs], (SUBLANES, LANES))
        return pltpu.bitcast(word, BF16)

    def contract_previous():
        acc_ref[...] += _dot(vt_ref[...], p_ref[...])

    @pl.when(i < n_et)
    def _():
        act_ref[...] = _gelu(_dot(u_ref[...], h2t_ref[...])).astype(BF16)
        contract_previous()
        slabs = PEER_KEYS // BF16_ROWS
        for j in range(te // PEER_KEYS):
            for c0 in range(0, tb // LANES, GATE_CHUNKS):
                for k0 in range(0, slabs, GATE_SLABS):
                    g = [[None] * GATE_SLABS for _ in range(GATE_CHUNKS)]
                    for hd in range(PEER_HEADS):
                        for cc in range(GATE_CHUNKS):
                            cs = slice((c0 + cc) * LANES, (c0 + cc + 1) * LANES)
                            cnt = packed_row(n1_ref, hd, j, cs)
                            wgt = packed_row(b1_ref, hd, j, cs)
                            for kk in range(GATE_SLABS):
                                words = slice((k0 + kk) * SUBLANES, (k0 + kk + 1) * SUBLANES)
                                rank = pltpu.bitcast(r2_ref[hd, words, cs], BF16)
                                val = pltpu.bitcast(a2_ref[hd, words, cs], BF16)
                                term = jnp.where(rank < cnt, val, jnp.zeros_like(val)) * wgt
                                g[cc][kk] = term if hd == 0 else g[cc][kk] + term
                    for cc in range(GATE_CHUNKS):
                        cs = slice((c0 + cc) * LANES, (c0 + cc + 1) * LANES)
                        for kk in range(GATE_SLABS):
                            r0 = j * PEER_KEYS + (k0 + kk) * BF16_ROWS
                            p_ref[r0:r0 + BF16_ROWS, cs] = g[cc][kk] * act_ref[r0:r0 + BF16_ROWS, cs]

    @pl.when(i == n_et)
    def _():
        contract_previous()
        o_ref[...] = x_ref[...] + ada_ref[0, 5:6, :] * acc_ref[...].T


def _peer_dense(x, h2t, tabs, u, vt, ada, *, tb, te, rows_per_ada):
    T = x.shape[0]
    n_exp = u.shape[0]
    n_et = n_exp // te
    n1 = te // PEER_KEYS
    assert T % tb == 0 and rows_per_ada % tb == 0 and n_exp % te == 0 and n1 % SUBLANES == 0
    r2, a2, n1w, b1w = tabs
    tab = pl.BlockSpec((PEER_HEADS, PEER_KEYS // 2, tb), lambda j, i: (0, 0, j))
    tab1 = pl.BlockSpec((PEER_HEADS, n1, tb), lambda j, i: (0, jnp.minimum(i, n_et - 1), j))
    return pl.pallas_call(
        functools.partial(_peer_dense_kernel, n_et=n_et),
        grid=(T // tb, n_et + 1),
        in_specs=[pl.BlockSpec((D_MODEL, tb), lambda j, i: (0, j)),
                  pl.BlockSpec((te, D_MODEL), lambda j, i: (jnp.minimum(i, n_et - 1), 0)),
                  pl.BlockSpec((D_MODEL, te), lambda j, i: (0, jnp.maximum(i - 1, 0))),
                  tab, tab, tab1, tab1,
                  pl.BlockSpec((tb, D_MODEL), lambda j, i: (j, 0)),
                  pl.BlockSpec((1, N_ADA, D_MODEL), lambda j, i: ((j * tb) // rows_per_ada, 0, 0))],
        out_specs=pl.BlockSpec((tb, D_MODEL), lambda j, i: (j, 0)),
        out_shape=jax.ShapeDtypeStruct(x.shape, F32),
        scratch_shapes=[pltpu.VMEM((D_MODEL, tb), F32), pltpu.VMEM((te, tb), BF16),
                        pltpu.VMEM((te, tb), BF16)],
        compiler_params=pltpu.CompilerParams(
            dimension_semantics=("parallel", "arbitrary"), vmem_limit_bytes=VMEM_LIMIT),
        name="peer_dense",
    )(h2t, u, vt, r2, a2, n1w, b1w, x, ada)


def _peer(x, ada, g2, wq, subkeys, u_tab, v_tab, *, rows_per_ada):
    wqh, wql = _split(wq.T)
    sk = jnp.stack([subkeys[:, 0], jnp.concatenate([subkeys[:, 1, 0::2], subkeys[:, 1, 1::2]], axis=1)],
                   axis=1)
    skh, skl = _split(sk.reshape(2 * PEER_HEADS, PEER_KEYS, -1))
    h2t, *tabs = _peer_query(x, ada, g2, wqh, wql, skh, skl, tb=256, rows_per_ada=rows_per_ada)
    return _peer_dense(x, h2t, tabs, u_tab.astype(BF16), v_tab.T.astype(BF16), ada,
                       tb=1024, te=1024, rows_per_ada=rows_per_ada)


def _rope_tables(L):
    pos = jnp.arange(L, dtype=jnp.int32)
    r = (pos // GRID_W).astype(F32)
    col = (pos % GRID_W).astype(F32)
    n_freq = D_HEAD_DIM // 4
    inv = ROPE_THETA ** (-jnp.arange(n_freq, dtype=F32) / n_freq)
    ang = jnp.concatenate([r[:, None] * inv, col[:, None] * inv], axis=-1)
    cos = jnp.repeat(jnp.cos(ang), 2, axis=-1)
    sin = jnp.repeat(jnp.sin(ang), 2, axis=-1) * jnp.tile(jnp.array([-1.0, 1.0], F32), D_HEAD_DIM // 2)
    reps = MIX_W // D_HEAD_DIM
    return jnp.tile(cos, (1, reps)), jnp.tile(sin, (1, reps))


def kernel(x, c, ctx, c_ctx, w_ada, b_ada, norm1_g, norm2_g, w_in, w_out, a_conv_w, b_vnorm_g,
           b_spatial_w, b_spatial_b, c_conv_w, c_conv_b, c_norm_g, d_qk_norm_g, d_lambda,
           d_subln_g, peer_wq, peer_subkeys, peer_u, peer_v):
    bsz, L, _ = x.shape
    lc = ctx.shape[1]
    depth = w_ada.shape[0]
    assert depth == 2 and bsz + 1 <= ADA_ROWS

    cc = jnp.zeros((ADA_ROWS, D_MODEL), F32).at[:bsz].set(c).at[bsz].set(c_ctx)
    ada = _ada(cc, w_ada, b_ada).reshape(depth, ADA_ROWS, N_ADA, D_MODEL)
    row = lambda v: v.reshape(1, -1)

    ada_x, ada_c = ada[0, :bsz], ada[0, bsz:bsz + 1]
    bsb = jnp.repeat(b_spatial_b[0].T, MIX_W // B_GROUPS, axis=1)
    even = functools.partial(
        _even_mixer, g1=row(norm1_g[0]), win=w_in[0].astype(BF16), wout=w_out[0].astype(BF16),
        cw=a_conv_w[0], vg=row(b_vnorm_g[0]), ws=b_spatial_w[0].astype(BF16), bsb=bsb)
    peer0 = functools.partial(_peer, g2=row(norm2_g[0]), wq=peer_wq[0], subkeys=peer_subkeys[0],
                              u_tab=peer_u[0], v_tab=peer_v[0])
    x = even(x, ada_x, tm=512)
    x = peer0(x.reshape(bsz * L, D_MODEL), ada_x, rows_per_ada=L).reshape(bsz, L, D_MODEL)
    ctx = even(ctx, jnp.broadcast_to(ada_c, (bsz, N_ADA, D_MODEL)), tm=lc)
    ctx = peer0(ctx.reshape(bsz * lc, D_MODEL), ada_c, rows_per_ada=bsz * lc).reshape(bsz, lc, D_MODEL)

    ada_x, ada_c = ada[1, :bsz], ada[1, bsz:bsz + 1]
    lam_init = 0.8 - 0.6 * math.exp(-0.3 * 1)
    win = w_in[1].astype(BF16)
    cos_t, sin_t = _rope_tables(L)
    seg_id = jnp.arange(MIX_W, dtype=jnp.int32) // D_HEAD_DIM
    seg = (seg_id[:, None] == seg_id[None, :]).astype(BF16)
    reps = MIX_W // D_HEAD_DIM
    qg, kg = row(jnp.tile(d_qk_norm_g[0, 0], reps)), row(jnp.tile(d_qk_norm_g[0, 1], reps))
    conf, q, ktx, vx = _odd_in(x, ada_x, row(norm1_g[1]), win, c_conv_w[0], row(c_conv_b[0]),
                               row(c_norm_g[0]), qg, kg, cos_t, sin_t, seg, tm=512)
    ktc, vc = _ctx_kv(ctx, jnp.broadcast_to(ada_c, (bsz, N_ADA, D_MODEL)), row(norm1_g[1]),
                      win[:, 3 * MIX_W:], kg, seg)
    x = _attn(q, ktc, ktx, vc, vx, conf, x, ada_x, w_out[1].astype(BF16), d_lambda[0],
              row(d_subln_g[0]), lam_init=lam_init, tq=512)
    x = _peer(x.reshape(bsz * L, D_MODEL), ada_x, row(norm2_g[1]), peer_wq[1], peer_subkeys[1],
              peer_u[1], peer_v[1], rows_per_ada=L)
    return x.reshape(bsz, L, D_MODEL)
```

```python
import functools
import math

import jax
import jax.numpy as jnp
from jax import lax
from jax.experimental import pallas as pl
from jax.experimental.pallas import tpu as pltpu

F32 = jnp.float32
BF16 = jnp.bfloat16

D_MODEL = 1024
MIX_W = 512
IN_W = 5 * MIX_W
GRID_W = 64
CHUNK = 128
B_GROUPS = 8
A_CONV = 3
C_CONV = 31
D_HEADS = 4
D_HEAD_DIM = 64
ROPE_THETA = 10000.0
N_ADA = 6
PEER_HEADS = 8
PEER_KEYS = 128
PEER_TOPK = 16
EPS = 1e-6

SUBLANES, LANES = 8, 128
BF16_ROWS = 2 * SUBLANES
GATE_CHUNKS = 4
GATE_SLABS = 4

ADA_ROWS = 16
CONV_HALO = 16
VMEM_LIMIT = 56 * 1024 * 1024

_STAIR = ((1, 8), (2, 5), (3, 4), (4, 3), (5, 2), (6, 2), (7, 2))


LOG2E = math.log2(math.e)
_GELU_C0 = -2.0 * math.sqrt(2.0 / math.pi) * LOG2E
_GELU_C1 = 0.044715 * _GELU_C0


def _gelu(x):
    return x / (1.0 + jnp.exp2(x * (_GELU_C0 + _GELU_C1 * (x * x))))


def _sigmoid(x):
    return 1.0 / (1.0 + jnp.exp(-x))


def _rms(x, g):
    return x * lax.rsqrt(jnp.mean(x * x, axis=-1, keepdims=True) + EPS) * g


def _norm_mod(x, g, shift, scale):
    return _rms(x, g) * (1.0 + scale) + shift


def _split(x):
    hi = x.astype(BF16)
    lo = (x - hi.astype(F32)).astype(BF16)
    return hi, lo


def _dot(a, b):
    return jnp.dot(a, b, preferred_element_type=F32)


def _ada_kernel(c_ref, w_ref, b_ref, o_ref):
    cc = c_ref[...]
    s = cc * _sigmoid(cc)
    o_ref[0] = jnp.dot(s, w_ref[0], precision=lax.Precision.HIGHEST,
                       preferred_element_type=F32) + b_ref[0]


def _ada(cc, w_ada, b_ada):
    depth, _, n = w_ada.shape
    tn = 1536
    return pl.pallas_call(
        _ada_kernel,
        grid=(depth, n // tn),
        in_specs=[pl.BlockSpec((ADA_ROWS, D_MODEL), lambda l, j: (0, 0)),
                  pl.BlockSpec((1, D_MODEL, tn), lambda l, j: (l, 0, j)),
                  pl.BlockSpec((1, 1, tn), lambda l, j: (l, 0, j))],
        out_specs=pl.BlockSpec((1, ADA_ROWS, tn), lambda l, j: (l, 0, j)),
        out_shape=jax.ShapeDtypeStruct((depth, ADA_ROWS, n), F32),
        compiler_params=pltpu.CompilerParams(vmem_limit_bytes=VMEM_LIMIT),
        name="ada",
    )(cc, w_ada, b_ada.reshape(depth, 1, n))


def _even_kernel(x_ref, xp_ref, xn_ref, ada_ref, g1_ref, win_ref, wout_ref, cw_ref, vg_ref,
                 ws_ref, bsb_ref, o_ref, *, n_tiles):
    t = pl.program_id(1)
    tm = x_ref.shape[1]
    shift, scale, gate = ada_ref[0, 0:1, :], ada_ref[0, 1:2, :], ada_ref[0, 2:3, :]
    g1 = g1_ref[...]

    x = x_ref[0]
    h = _norm_mod(x, g1, shift, scale).astype(BF16)
    proj = _dot(h, win_ref[...])
    a_h, a_c, a_b = proj[:, 0:MIX_W], proj[:, MIX_W:2 * MIX_W], proj[:, 2 * MIX_W:3 * MIX_W]
    b_u, b_v = proj[:, 3 * MIX_W:4 * MIX_W], proj[:, 4 * MIX_W:5 * MIX_W]

    z = a_c * a_h

    def edge_z(ref, row):
        hh = _norm_mod(ref[0], g1, shift, scale).astype(BF16)
        pp = _dot(hh, win_ref[:, 0:2 * MIX_W])
        return (pp[:, 0:MIX_W] * pp[:, MIX_W:2 * MIX_W])[row:row + 1, :]

    zp = jnp.where(t > 0, edge_z(xp_ref, 7), 0.0)
    zn = jnp.where(t < n_tiles - 1, edge_z(xn_ref, 0), 0.0)
    row = lax.broadcasted_iota(jnp.int32, (tm, MIX_W), 0)
    z_m1 = jnp.where(row == 0, zp, pltpu.roll(z, 1, axis=0))
    z_p1 = jnp.where(row == tm - 1, zn, pltpu.roll(z, tm - 1, axis=0))
    ya = a_b * (cw_ref[0:1, :] * z_m1 + cw_ref[1:2, :] * z + cw_ref[2:3, :] * z_p1)

    u = _gelu(b_u)
    v = _rms(_gelu(b_v), vg_ref[...]).astype(BF16)
    lane = lax.broadcasted_iota(jnp.int32, (CHUNK, CHUNK), 1)
    group_w = MIX_W // B_GROUPS
    rows = []
    for n in range(tm // CHUNK):
        cols = []
        for j in range(MIX_W // CHUNK):
            vv = v[n * CHUNK:(n + 1) * CHUNK, j * CHUNK:(j + 1) * CHUNK]
            s0 = _dot(ws_ref[2 * j], vv)
            s1 = _dot(ws_ref[2 * j + 1], vv)
            cols.append(jnp.where(lane < group_w, s0, s1))
        rows.append(jnp.concatenate(cols, axis=1) + bsb_ref[...])
    s = jnp.concatenate(rows, axis=0)
    yb = u * s

    y = jnp.concatenate([ya, yb], axis=1).astype(BF16)
    o_ref[0] = x + gate * _dot(y, wout_ref[...])


def _even_mixer(x, ada, g1, win, wout, cw, vg, ws, bsb, *, tm):
    bsz, L, _ = x.shape
    n_tiles = L // tm
    hb = tm // 8
    const = lambda *shape: pl.BlockSpec(shape, lambda b, t: (0,) * len(shape))
    return pl.pallas_call(
        functools.partial(_even_kernel, n_tiles=n_tiles),
        grid=(bsz, n_tiles),
        in_specs=[pl.BlockSpec((1, tm, D_MODEL), lambda b, t: (b, t, 0)),
                  pl.BlockSpec((1, 8, D_MODEL), lambda b, t: (b, jnp.maximum(t * hb - 1, 0), 0)),
                  pl.BlockSpec((1, 8, D_MODEL),
                               lambda b, t: (b, jnp.minimum((t + 1) * hb, L // 8 - 1), 0)),
                  pl.BlockSpec((1, N_ADA, D_MODEL), lambda b, t: (b, 0, 0)),
                  const(1, D_MODEL), const(D_MODEL, IN_W), const(2 * MIX_W, D_MODEL),
                  const(A_CONV, MIX_W), const(1, MIX_W), const(B_GROUPS, CHUNK, CHUNK),
                  const(CHUNK, MIX_W)],
        out_specs=pl.BlockSpec((1, tm, D_MODEL), lambda b, t: (b, t, 0)),
        out_shape=jax.ShapeDtypeStruct(x.shape, F32),
        compiler_params=pltpu.CompilerParams(
            dimension_semantics=("parallel", "parallel"), vmem_limit_bytes=VMEM_LIMIT),
        name="even_mixer",
    )(x, x, x, ada, g1, win, wout, cw, vg, ws, bsb)


def _head_norm(t, gain, seg_ref):
    hi, lo = _split(t * t)
    ss = _dot(hi, seg_ref[...]) + _dot(lo, seg_ref[...])
    return t * lax.rsqrt(ss * (1.0 / D_HEAD_DIM) + EPS) * gain


def _odd_in_kernel(x_ref, xp_ref, xn_ref, ada_ref, g1_ref, win_ref, cw_ref, cb_ref, cg_ref,
                   qg_ref, kg_ref, cos_ref, sin_ref, seg_ref,
                   conf_ref, q_ref, kt_ref, v_ref, ybuf_ref, *, n_tiles):
    t = pl.program_id(1)
    tm = x_ref.shape[1]
    shift, scale = ada_ref[0, 0:1, :], ada_ref[0, 1:2, :]
    g1 = g1_ref[...]

    h = _norm_mod(x_ref[0], g1, shift, scale).astype(BF16)
    proj = _dot(h, win_ref[...])

    def glu_rows(ref):
        hh = _norm_mod(ref[0], g1, shift, scale).astype(BF16)
        pp = _dot(hh, win_ref[:, 0:2 * MIX_W])
        return pp[:, 0:MIX_W] * _sigmoid(pp[:, MIX_W:2 * MIX_W])

    ybuf_ref[0:CONV_HALO, :] = jnp.where(t > 0, glu_rows(xp_ref), 0.0)
    ybuf_ref[CONV_HALO:CONV_HALO + tm, :] = proj[:, 0:MIX_W] * _sigmoid(proj[:, MIX_W:2 * MIX_W])
    ybuf_ref[CONV_HALO + tm:2 * CONV_HALO + tm, :] = jnp.where(t < n_tiles - 1, glu_rows(xn_ref), 0.0)
    base = CONV_HALO - C_CONV // 2
    acc = jnp.zeros((tm, MIX_W), F32) + cb_ref[...]
    for k in range(C_CONV):
        acc = acc + cw_ref[k:k + 1, :] * ybuf_ref[base + k:base + k + tm, :]
    yn = _rms(acc, cg_ref[...])
    conf_ref[0] = (yn * _sigmoid(yn)).astype(BF16)

    lane = lax.broadcasted_iota(jnp.int32, (tm, MIX_W), 1)
    even_lane = (lane % 2) == 0

    def rope(tn):
        partner = jnp.where(even_lane, pltpu.roll(tn, MIX_W - 1, axis=1), pltpu.roll(tn, 1, axis=1))
        return tn * cos_ref[...] + partner * sin_ref[...]

    q = rope(_head_norm(proj[:, 2 * MIX_W:3 * MIX_W], qg_ref[...], seg_ref))
    k = rope(_head_norm(proj[:, 3 * MIX_W:4 * MIX_W], kg_ref[...], seg_ref))
    q_ref[0] = (q * (D_HEAD_DIM ** -0.5 * LOG2E)).astype(BF16)
    kt_ref[0] = k.T.astype(BF16)
    v_ref[0] = proj[:, 4 * MIX_W:5 * MIX_W].astype(BF16)


def _odd_in(x, ada, g1, win, cw, cb, cg, qg, kg, cos_t, sin_t, seg, *, tm):
    bsz, L, _ = x.shape
    n_tiles = L // tm
    hb = tm // CONV_HALO
    const = lambda *shape: pl.BlockSpec(shape, lambda b, t: (0,) * len(shape))
    tok = lambda w: pl.BlockSpec((1, tm, w), lambda b, t: (b, t, 0))
    return pl.pallas_call(
        functools.partial(_odd_in_kernel, n_tiles=n_tiles),
        grid=(bsz, n_tiles),
        in_specs=[tok(D_MODEL),
                  pl.BlockSpec((1, CONV_HALO, D_MODEL),
                               lambda b, t: (b, jnp.maximum(t * hb - 1, 0), 0)),
                  pl.BlockSpec((1, CONV_HALO, D_MODEL),
                               lambda b, t: (b, jnp.minimum((t + 1) * hb, L // CONV_HALO - 1), 0)),
                  pl.BlockSpec((1, N_ADA, D_MODEL), lambda b, t: (b, 0, 0)),
                  const(1, D_MODEL), const(D_MODEL, IN_W), const(C_CONV, MIX_W),
                  const(1, MIX_W), const(1, MIX_W), const(1, MIX_W), const(1, MIX_W),
                  pl.BlockSpec((tm, MIX_W), lambda b, t: (t, 0)),
                  pl.BlockSpec((tm, MIX_W), lambda b, t: (t, 0)),
                  const(MIX_W, MIX_W)],
        out_specs=[tok(MIX_W), tok(MIX_W),
                   pl.BlockSpec((1, MIX_W, tm), lambda b, t: (b, 0, t)),
                   tok(MIX_W)],
        out_shape=[jax.ShapeDtypeStruct((bsz, L, MIX_W), BF16),
                   jax.ShapeDtypeStruct((bsz, L, MIX_W), BF16),
                   jax.ShapeDtypeStruct((bsz, MIX_W, L), BF16),
                   jax.ShapeDtypeStruct((bsz, L, MIX_W), BF16)],
        scratch_shapes=[pltpu.VMEM((tm + 2 * CONV_HALO, MIX_W), F32)],
        compiler_params=pltpu.CompilerParams(
            dimension_semantics=("parallel", "parallel"), vmem_limit_bytes=VMEM_LIMIT),
        name="odd_in",
    )(x, x, x, ada, g1, win, cw, cb, cg, qg, kg, cos_t, sin_t, seg)


def _ctx_kv_kernel(x_ref, ada_ref, g1_ref, wkv_ref, kg_ref, seg_ref, kt_ref, v_ref):
    shift, scale = ada_ref[0, 0:1, :], ada_ref[0, 1:2, :]
    h = _norm_mod(x_ref[0], g1_ref[...], shift, scale).astype(BF16)
    proj = _dot(h, wkv_ref[...])
    k = _head_norm(proj[:, 0:MIX_W], kg_ref[...], seg_ref)
    kt_ref[0] = k.T.astype(BF16)
    v_ref[0] = proj[:, MIX_W:2 * MIX_W].astype(BF16)


def _ctx_kv(ctx, ada, g1, wkv, kg, seg):
    bsz, L, _ = ctx.shape
    const = lambda *shape: pl.BlockSpec(shape, lambda b: (0,) * len(shape))
    return pl.pallas_call(
        _ctx_kv_kernel,
        grid=(bsz,),
        in_specs=[pl.BlockSpec((1, L, D_MODEL), lambda b: (b, 0, 0)),
                  pl.BlockSpec((1, N_ADA, D_MODEL), lambda b: (b, 0, 0)),
                  const(1, D_MODEL), const(D_MODEL, 2 * MIX_W), const(1, MIX_W),
                  const(MIX_W, MIX_W)],
        out_specs=[pl.BlockSpec((1, MIX_W, L), lambda b: (b, 0, 0)),
                   pl.BlockSpec((1, L, MIX_W), lambda b: (b, 0, 0))],
        out_shape=[jax.ShapeDtypeStruct((bsz, MIX_W, L), BF16),
                   jax.ShapeDtypeStruct((bsz, L, MIX_W), BF16)],
        compiler_params=pltpu.CompilerParams(
            dimension_semantics=("parallel",), vmem_limit_bytes=VMEM_LIMIT),
        name="ctx_kv",
    )(ctx, ada, g1, wkv, kg, seg)


def _attn_kernel(q_ref, ktc_ref, ktx_ref, vc_ref, vx_ref, conf_ref, x_ref, ada_ref, wout_ref,
                 lam_ref, sg_ref, o_ref, *, lam_init):
    tq = q_ref.shape[1]
    vw = 2 * D_HEAD_DIM
    lam_p = lam_ref[...]
    lam = (jnp.exp(jnp.sum(lam_p[0:1] * lam_p[1:2], axis=-1, keepdims=True))
           - jnp.exp(jnp.sum(lam_p[2:3] * lam_p[3:4], axis=-1, keepdims=True)) + lam_init)
    first_map = lax.broadcasted_iota(jnp.int32, (tq, vw), 1) < D_HEAD_DIM
    outs = []
    for hd in range(D_HEADS):
        sl = slice(hd * vw, (hd + 1) * vw)
        qh = q_ref[0, :, sl]
        maps = []
        for m in range(2):
            qm = jnp.where(first_map if m == 0 else jnp.logical_not(first_map), qh, jnp.zeros_like(qh))
            sc = _dot(qm, ktc_ref[0, sl, :])
            sx = _dot(qm, ktx_ref[0, sl, :])
            mx = jnp.maximum(jnp.max(sc, axis=-1, keepdims=True), jnp.max(sx, axis=-1, keepdims=True))
            pc, px = jnp.exp2(sc - mx), jnp.exp2(sx - mx)
            inv = 1.0 / (jnp.sum(pc, axis=-1, keepdims=True) + jnp.sum(px, axis=-1, keepdims=True))
            pv = _dot(pc.astype(BF16), vc_ref[0, :, sl]) + _dot(px.astype(BF16), vx_ref[0, :, sl])
            maps.append(pv * inv)
        o = maps[0] - lam * maps[1]
        outs.append(_rms(o, sg_ref[...]) * (1.0 - lam_init))
    y = jnp.concatenate([conf_ref[0]] + [o.astype(BF16) for o in outs], axis=1)
    o_ref[0] = x_ref[0] + ada_ref[0, 2:3, :] * _dot(y, wout_ref[...])


def _attn(q, ktc, ktx, vc, vx, conf, x, ada, wout, lam_p, sg, *, lam_init, tq):
    bsz, L, _ = x.shape
    lc = vc.shape[1]
    const = lambda *shape: pl.BlockSpec(shape, lambda b, t: (0,) * len(shape))
    tok = lambda w: pl.BlockSpec((1, tq, w), lambda b, t: (b, t, 0))
    return pl.pallas_call(
        functools.partial(_attn_kernel, lam_init=lam_init),
        grid=(bsz, L // tq),
        in_specs=[tok(MIX_W),
                  pl.BlockSpec((1, MIX_W, lc), lambda b, t: (b, 0, 0)),
                  pl.BlockSpec((1, MIX_W, L), lambda b, t: (b, 0, 0)),
                  pl.BlockSpec((1, lc, MIX_W), lambda b, t: (b, 0, 0)),
                  pl.BlockSpec((1, L, MIX_W), lambda b, t: (b, 0, 0)),
                  tok(MIX_W), tok(D_MODEL),
                  pl.BlockSpec((1, N_ADA, D_MODEL), lambda b, t: (b, 0, 0)),
                  const(2 * MIX_W, D_MODEL), const(4, D_HEAD_DIM), const(1, 2 * D_HEAD_DIM)],
        out_specs=tok(D_MODEL),
        out_shape=jax.ShapeDtypeStruct(x.shape, F32),
        compiler_params=pltpu.CompilerParams(
            dimension_semantics=("parallel", "parallel"), vmem_limit_bytes=VMEM_LIMIT),
        name="diff_attn",
    )(q, ktc, ktx, vc, vx, conf, x, ada, wout, lam_p, sg)


def _exact_ranks(s, label):
    work = s
    rank = jnp.full(s.shape, float(PEER_TOPK), F32)
    for r in range(PEER_TOPK):
        m = jnp.max(work, axis=0, keepdims=True)
        first = jnp.min(jnp.where(work == m, label, 1e9), axis=0, keepdims=True)
        sel = label == first
        rank = jnp.where(sel, float(r), rank)
        work = jnp.where(sel, -jnp.inf, work)
    return rank


def _batcher_network(n):
    def merge(lo, hi, r):
        step = 2 * r
        if step < hi - lo:
            yield from merge(lo, hi, step)
            yield from merge(lo + r, hi, step)
            yield from [(i, i + r) for i in range(lo + r, hi - r, step)]
        else:
            yield (lo, lo + r)

    def sort(lo, hi):
        if hi > lo:
            mid = lo + (hi - lo) // 2
            yield from sort(lo, mid)
            yield from sort(mid + 1, hi)
            yield from merge(lo, hi, 1)

    return tuple(sort(0, n - 1))


_SORT16 = _batcher_network(PEER_TOPK)


def _sublane_allreduce(x, op):
    for shift in (4, 2, 1):
        x = op(x, pltpu.roll(x, shift, axis=0))
    return x


def _sorted_top16(rows):
    b = list(rows)
    for i, j in _SORT16:
        b[i], b[j] = jnp.maximum(b[i], b[j]), jnp.minimum(b[i], b[j])
    for shift in (4, 2, 1):
        m = [jnp.maximum(b[i], pltpu.roll(b[PEER_TOPK - 1 - i], shift, axis=0))
             for i in range(PEER_TOPK)]
        d = PEER_TOPK // 2
        while d:
            for i in range(PEER_TOPK):
                if not i & d:
                    m[i], m[i + d] = jnp.maximum(m[i], m[i + d]), jnp.minimum(m[i], m[i + d])
            d //= 2
        b = m
    return b


def _lookup_by_rank(rows, top, values):
    out = []
    for a in rows:
        r = jnp.zeros_like(a) + values[0]
        for k in range(PEER_TOPK):
            r = jnp.where(top[k] > a, values[k + 1], r)
        out.append(r)
    return jnp.concatenate(out, axis=0)


def _tie_flag(rows, top):
    flag = jnp.zeros_like(top[0])
    for k in range(PEER_TOPK - 1):
        flag = jnp.where(top[k] == top[k + 1], 1.0, flag)
    n_ge = None
    for a in rows:
        hit = jnp.where(a >= top[PEER_TOPK - 1], 1.0, 0.0)
        n_ge = hit if n_ge is None else n_ge + hit
    n_ge = _sublane_allreduce(n_ge, jnp.add)
    return jnp.where(n_ge != float(PEER_TOPK), 1.0, flag)


def _dup_bf16(v):
    bits = pltpu.bitcast(v.astype(BF16).astype(F32), jnp.uint32)
    return bits | (bits >> 16)


def _pack_bf16(v):
    n = v.shape[0] // 2
    bits = pltpu.bitcast(v.astype(BF16).astype(F32), jnp.uint32)
    return (bits[:n] >> 16) | bits[n:]


def _peer_query_kernel(x_ref, ada_ref, g2_ref, wq_ref, skh_ref, skl_ref,
                       h2t_ref, r2_ref, a2_ref, n1_ref, b1_ref, qt_ref):
    tb = x_ref.shape[0]
    shift, scale = ada_ref[0, 3:4, :], ada_ref[0, 4:5, :]
    h2 = _norm_mod(x_ref[...], g2_ref[...], shift, scale)
    h2t = h2.T
    hh, hl = _split(h2t)
    h2t_ref[...] = hh
    qt_ref[...] = _dot(wq_ref[...], hh) + _dot(wq_ref[...], hl)

    pos = lax.broadcasted_iota(jnp.int32, (PEER_KEYS, tb), 0)
    label1 = pos.astype(F32)
    label2 = jnp.where(pos < PEER_KEYS // 2, 2 * pos, 2 * pos - (PEER_KEYS - 1)).astype(F32)
    sub = lax.broadcasted_iota(jnp.int32, (SUBLANES, tb), 0)
    subf = sub.astype(F32)
    cand_label = ([subf, subf + 8.0] + [subf + 16.0 * k1 for k1, _ in _STAIR]
                  + [16.0 * (subf + 8.0)])
    cand_valid = [None, None] + [None if nv == SUBLANES else sub < nv for _, nv in _STAIR] + [None]

    def spread(vals):
        out = vals[0]
        for u in range(1, SUBLANES):
            out = jnp.where(sub == u, vals[u], out)
        return out

    def head(hd, carry):
        def scores(p):
            qp = qt_ref[pl.ds(pl.multiple_of((2 * hd + p) * PEER_KEYS, PEER_KEYS), PEER_KEYS), :]
            qh, ql = _split(qp)
            kh, kl = skh_ref[2 * hd + p], skl_ref[2 * hd + p]
            return _dot(kh, qh) + (_dot(kh, ql) + _dot(kl, qh))

        s1, s2 = scores(0), scores(1)
        rows1 = [s1[SUBLANES * r:SUBLANES * (r + 1)] for r in range(PEER_KEYS // SUBLANES)]
        rows2 = [s2[SUBLANES * r:SUBLANES * (r + 1)] for r in range(PEER_KEYS // SUBLANES)]
        top1, top2 = _sorted_top16(rows1), _sorted_top16(rows2)

        t2a, t2b, t1b = spread(top2[:SUBLANES]), spread(top2[SUBLANES:]), spread(top1[SUBLANES:])
        cand = ([t2a + top1[0], t2b + top1[0]] + [t2a + top1[k1] for k1, _ in _STAIR]
                + [t1b + top2[0]])
        cand = [c if v is None else jnp.where(v, c, -jnp.inf) for c, v in zip(cand, cand_valid)]
        pad = [jnp.full((SUBLANES, tb), -jnp.inf, F32)] * (PEER_TOPK - len(cand))
        best = _sorted_top16(cand + pad)
        z = functools.reduce(jnp.add, [jnp.exp(b - best[0]) for b in best])
        above = [jnp.where(c >= best[PEER_TOPK - 1], 1.0, 0.0) for c in cand]
        n_above = _sublane_allreduce(functools.reduce(jnp.add, above), jnp.add)

        def take_in_index_order():
            left = cand
            for r in range(PEER_TOPK):
                m = _sublane_allreduce(functools.reduce(jnp.maximum, left), jnp.maximum)
                first = functools.reduce(
                    jnp.minimum, [jnp.where(c == m, lab, 1e9) for c, lab in zip(left, cand_label)])
                first = _sublane_allreduce(first, jnp.minimum)
                left = [jnp.where(lab == first, -jnp.inf, c) for c, lab in zip(left, cand_label)]
            return [jnp.where(jnp.logical_and(c0 > -jnp.inf, c == -jnp.inf), 1.0, 0.0)
                    for c0, c in zip(cand, left)]

        taken = lax.cond(jnp.max(jnp.abs(n_above - float(PEER_TOPK))) > 0.0,
                         take_in_index_order, lambda: above)
        counts = [_sublane_allreduce(taken[0] + taken[1], jnp.add)]
        counts += [_sublane_allreduce(t, jnp.add) for t in taken[2:-1]]
        counts += [_sublane_allreduce(jnp.where(sub == u, taken[-1], 0.0), jnp.add)
                   for u in range(SUBLANES)]
        counts.append(0.0)
        ranks = [float(k) for k in range(PEER_TOPK + 1)]

        def fast():
            return _lookup_by_rank(rows2, top2, ranks), _lookup_by_rank(rows1, top1, counts)

        def exact():
            rank1 = _exact_ranks(s1, label1)
            n1 = jnp.zeros_like(s1)
            for k1 in range(PEER_TOPK):
                n1 = jnp.where(rank1 == float(k1), jnp.tile(counts[k1], (PEER_KEYS // SUBLANES, 1)), n1)
            return _exact_ranks(s2, label2), n1

        ties = jnp.max(jnp.maximum(_tie_flag(rows1, top1), _tie_flag(rows2, top2)))
        rank2, n1 = lax.cond(ties > 0.0, exact, fast)
        tile = lambda v: jnp.tile(v, (PEER_KEYS // SUBLANES, 1))
        r2_ref[hd] = _pack_bf16(rank2)
        a2_ref[hd] = _pack_bf16(jnp.exp(s2 - tile(top2[0])))
        n1_ref[hd] = _dup_bf16(n1)
        b1_ref[hd] = _dup_bf16(jnp.exp(s1 - tile(top1[0])) * tile(1.0 / z))
        return carry

    lax.fori_loop(0, PEER_HEADS, head, 0)


def _peer_query(x, ada, g2, wqt, skh, skl, *, tb, rows_per_ada):
    T = x.shape[0]
    qd = wqt.shape[0]
    const = lambda *shape: pl.BlockSpec(shape, lambda j: (0,) * len(shape))
    tab = lambda rows: pl.BlockSpec((PEER_HEADS, rows, tb), lambda j: (0, 0, j))
    tab_shape = lambda rows: jax.ShapeDtypeStruct((PEER_HEADS, rows, T), jnp.uint32)
    half = PEER_KEYS // 2
    return pl.pallas_call(
        _peer_query_kernel,
        grid=(T // tb,),
        in_specs=[pl.BlockSpec((tb, D_MODEL), lambda j: (j, 0)),
                  pl.BlockSpec((1, N_ADA, D_MODEL), lambda j: ((j * tb) // rows_per_ada, 0, 0)),
                  const(1, D_MODEL), const(qd, D_MODEL),
                  const(2 * PEER_HEADS, PEER_KEYS, PEER_KEYS),
                  const(2 * PEER_HEADS, PEER_KEYS, PEER_KEYS)],
        out_specs=[pl.BlockSpec((D_MODEL, tb), lambda j: (0, j)), tab(half), tab(half),
                   tab(PEER_KEYS), tab(PEER_KEYS)],
        out_shape=[jax.ShapeDtypeStruct((D_MODEL, T), BF16), tab_shape(half), tab_shape(half),
                   tab_shape(PEER_KEYS), tab_shape(PEER_KEYS)],
        scratch_shapes=[pltpu.VMEM((qd, tb), F32)],
        compiler_params=pltpu.CompilerParams(
            dimension_semantics=("parallel",), vmem_limit_bytes=VMEM_LIMIT),
        name="peer_query",
    )(x, ada, g2, wqt, skh, skl)


def _peer_dense_kernel(h2t_ref, u_ref, vt_ref, r2_ref, a2_ref, n1_ref, b1_ref, x_ref, ada_ref,
                       o_ref, acc_ref, p_ref, act_ref, *, n_et):
    i = pl.program_id(1)
    te, tb = p_ref.shape

    @pl.when(i == 0)
    def _():
        acc_ref[...] = jnp.zeros_like(acc_ref)
        p_ref[...] = jnp.zeros_like(p_ref)

    def packed_row(ref, hd, j, cs):
        word = jnp.broadcast_to(ref[hd, j:j + 1, cs], (SUBLANES, LANES))
        return pltpu.bitcast(word, BF16)

    def contract_previous():
        acc_ref[...] += _dot(vt_ref[...], p_ref[...])

    @pl.when(i < n_et)
    def _():
        act_ref[...] = _gelu(_dot(u_ref[...], h2t_ref[...])).astype(BF16)
        contract_previous()
        slabs = PEER_KEYS // BF16_ROWS
        for j in range(te // PEER_KEYS):
            for c0 in range(0, tb // LANES, GATE_CHUNKS):
                for k0 in range(0, slabs, GATE_SLABS):
                    g = [[None] * GATE_SLABS for _ in range(GATE_CHUNKS)]
                    for hd in range(PEER_HEADS):
                        for cc in range(GATE_CHUNKS):
                            cs = slice((c0 + cc) * LANES, (c0 + cc + 1) * LANES)
                            cnt = packed_row(n1_ref, hd, j, cs)
                            wgt = packed_row(b1_ref, hd, j, cs)
                            for kk in range(GATE_SLABS):
                                words = slice((k0 + kk) * SUBLANES, (k0 + kk + 1) * SUBLANES)
                                rank = pltpu.bitcast(r2_ref[hd, words, cs], BF16)
                                val = pltpu.bitcast(a2_ref[hd, words, cs], BF16)
                                term = jnp.where(rank < cnt, val, jnp.zeros_like(val)) * wgt
                                g[cc][kk] = term if hd == 0 else g[cc][kk] + term
                    for cc in range(GATE_CHUNKS):
                        cs = slice((c0 + cc) * LANES, (c0 + cc + 1) * LANES)
                        for kk in range(GATE_SLABS):
                            r0 = j * PEER_KEYS + (k0 + kk) * BF16_ROWS
                            p_ref[r0:r0 + BF16_ROWS, cs] = g[cc][kk] * act_ref[r0:r0 + BF16_ROWS, cs]

    @pl.when(i == n_et)
    def _():
        contract_previous()
        o_ref[...] = x_ref[...] + ada_ref[0, 5:6, :] * acc_ref[...].T


def _peer_dense(x, h2t, tabs, u, vt, ada, *, tb, te, rows_per_ada):
    T = x.shape[0]
    n_exp = u.shape[0]
    n_et = n_exp // te
    n1 = te // PEER_KEYS
    assert T % tb == 0 and rows_per_ada % tb == 0 and n_exp % te == 0 and n1 % SUBLANES == 0
    r2, a2, n1w, b1w = tabs
    tab = pl.BlockSpec((PEER_HEADS, PEER_KEYS // 2, tb), lambda j, i: (0, 0, j))
    tab1 = pl.BlockSpec((PEER_HEADS, n1, tb), lambda j, i: (0, jnp.minimum(i, n_et - 1), j))
    return pl.pallas_call(
        functools.partial(_peer_dense_kernel, n_et=n_et),
        grid=(T // tb, n_et + 1),
        in_specs=[pl.BlockSpec((D_MODEL, tb), lambda j, i: (0, j)),
                  pl.BlockSpec((te, D_MODEL), lambda j, i: (jnp.minimum(i, n_et - 1), 0)),
                  pl.BlockSpec((D_MODEL, te), lambda j, i: (0, jnp.maximum(i - 1, 0))),
                  tab, tab, tab1, tab1,
                  pl.BlockSpec((tb, D_MODEL), lambda j, i: (j, 0)),
                  pl.BlockSpec((1, N_ADA, D_MODEL), lambda j, i: ((j * tb) // rows_per_ada, 0, 0))],
        out_specs=pl.BlockSpec((tb, D_MODEL), lambda j, i: (j, 0)),
        out_shape=jax.ShapeDtypeStruct(x.shape, F32),
        scratch_shapes=[pltpu.VMEM((D_MODEL, tb), F32), pltpu.VMEM((te, tb), BF16),
                        pltpu.VMEM((te, tb), BF16)],
        compiler_params=pltpu.CompilerParams(
            dimension_semantics=("parallel", "arbitrary"), vmem_limit_bytes=VMEM_LIMIT),
        name="peer_dense",
    )(h2t, u, vt, r2, a2, n1w, b1w, x, ada)


def _peer(x, ada, g2, wq, subkeys, u_tab, v_tab, *, rows_per_ada):
    wqt = wq.T.astype(BF16)
    sk = jnp.stack([subkeys[:, 0], jnp.concatenate([subkeys[:, 1, 0::2], subkeys[:, 1, 1::2]], axis=1)],
                   axis=1)
    skh, skl = _split(sk.reshape(2 * PEER_HEADS, PEER_KEYS, -1))
    h2t, *tabs = _peer_query(x, ada, g2, wqt, skh, skl, tb=256, rows_per_ada=rows_per_ada)
    return _peer_dense(x, h2t, tabs, u_tab.astype(BF16), v_tab.T.astype(BF16), ada,
                       tb=1024, te=1024, rows_per_ada=rows_per_ada)


def _rope_tables(L):
    pos = jnp.arange(L, dtype=jnp.int32)
    r = (pos // GRID_W).astype(F32)
    col = (pos % GRID_W).astype(F32)
    n_freq = D_HEAD_DIM // 4
    inv = ROPE_THETA ** (-jnp.arange(n_freq, dtype=F32) / n_freq)
    ang = jnp.concatenate([r[:, None] * inv, col[:, None] * inv], axis=-1)
    cos = jnp.repeat(jnp.cos(ang), 2, axis=-1)
    sin = jnp.repeat(jnp.sin(ang), 2, axis=-1) * jnp.tile(jnp.array([-1.0, 1.0], F32), D_HEAD_DIM // 2)
    reps = MIX_W // D_HEAD_DIM
    return jnp.tile(cos, (1, reps)), jnp.tile(sin, (1, reps))


def kernel(x, c, ctx, c_ctx, w_ada, b_ada, norm1_g, norm2_g, w_in, w_out, a_conv_w, b_vnorm_g,
           b_spatial_w, b_spatial_b, c_conv_w, c_conv_b, c_norm_g, d_qk_norm_g, d_lambda,
           d_subln_g, peer_wq, peer_subkeys, peer_u, peer_v):
    bsz, L, _ = x.shape
    lc = ctx.shape[1]
    depth = w_ada.shape[0]
    assert depth == 2 and bsz + 1 <= ADA_ROWS

    cc = jnp.zeros((ADA_ROWS, D_MODEL), F32).at[:bsz].set(c).at[bsz].set(c_ctx)
    ada = _ada(cc, w_ada, b_ada).reshape(depth, ADA_ROWS, N_ADA, D_MODEL)
    row = lambda v: v.reshape(1, -1)

    ada_x, ada_c = ada[0, :bsz], ada[0, bsz:bsz + 1]
    bsb = jnp.repeat(b_spatial_b[0].T, MIX_W // B_GROUPS, axis=1)
    even = functools.partial(
        _even_mixer, g1=row(norm1_g[0]), win=w_in[0].astype(BF16), wout=w_out[0].astype(BF16),
        cw=a_conv_w[0], vg=row(b_vnorm_g[0]), ws=b_spatial_w[0].astype(BF16), bsb=bsb)
    peer0 = functools.partial(_peer, g2=row(norm2_g[0]), wq=peer_wq[0], subkeys=peer_subkeys[0],
                              u_tab=peer_u[0], v_tab=peer_v[0])
    x = even(x, ada_x, tm=512)
    x = peer0(x.reshape(bsz * L, D_MODEL), ada_x, rows_per_ada=L).reshape(bsz, L, D_MODEL)
    ctx = even(ctx, jnp.broadcast_to(ada_c, (bsz, N_ADA, D_MODEL)), tm=lc)
    ctx = peer0(ctx.reshape(bsz * lc, D_MODEL), ada_c, rows_per_ada=bsz * lc).reshape(bsz, lc, D_MODEL)

    ada_x, ada_c = ada[1, :bsz], ada[1, bsz:bsz + 1]
    lam_init = 0.8 - 0.6 * math.exp(-0.3 * 1)
    win = w_in[1].astype(BF16)
    cos_t, sin_t = _rope_tables(L)
    seg_id = jnp.arange(MIX_W, dtype=jnp.int32) // D_HEAD_DIM
    seg = (seg_id[:, None] == seg_id[None, :]).astype(BF16)
    reps = MIX_W // D_HEAD_DIM
    qg, kg = row(jnp.tile(d_qk_norm_g[0, 0], reps)), row(jnp.tile(d_qk_norm_g[0, 1], reps))
    conf, q, ktx, vx = _odd_in(x, ada_x, row(norm1_g[1]), win, c_conv_w[0], row(c_conv_b[0]),
                               row(c_norm_g[0]), qg, kg, cos_t, sin_t, seg, tm=512)
    ktc, vc = _ctx_kv(ctx, jnp.broadcast_to(ada_c, (bsz, N_ADA, D_MODEL)), row(norm1_g[1]),
                      win[:, 3 * MIX_W:], kg, seg)
    x = _attn(q, ktc, ktx, vc, vx, conf, x, ada_x, w_out[1].astype(BF16), d_lambda[0],
              row(d_subln_g[0]), lam_init=lam_init, tq=512)
    x = _peer(x.reshape(bsz * L, D_MODEL), ada_x, row(norm2_g[1]), peer_wq[1], peer_subkeys[1],
              peer_u[1], peer_v[1], rows_per_ada=L)
    return x.reshape(bsz, L, D_MODEL)
```

```python
import functools
import math

import jax
import jax.numpy as jnp
from jax import lax
from jax.experimental import pallas as pl
from jax.experimental.pallas import tpu as pltpu

F32 = jnp.float32
BF16 = jnp.bfloat16

D_MODEL = 1024
MIX_W = 512
IN_W = 5 * MIX_W
GRID_W = 64
CHUNK = 128
B_GROUPS = 8
A_CONV = 3
C_CONV = 31
D_HEADS = 4
D_HEAD_DIM = 64
ROPE_THETA = 10000.0
N_ADA = 6
PEER_HEADS = 8
PEER_KEYS = 128
PEER_TOPK = 16
EPS = 1e-6

SUBLANES, LANES = 8, 128
BF16_ROWS = 2 * SUBLANES
GATE_CHUNKS = 4
GATE_SLABS = 4

ADA_ROWS = 16
CONV_HALO = 16
VMEM_LIMIT = 56 * 1024 * 1024

_STAIR = ((1, 8), (2, 5), (3, 4), (4, 3), (5, 2), (6, 2), (7, 2))


LOG2E = math.log2(math.e)
_GELU_C0 = -2.0 * math.sqrt(2.0 / math.pi) * LOG2E
_GELU_C1 = 0.044715 * _GELU_C0


def _gelu(x):
    return x / (1.0 + jnp.exp2(x * (_GELU_C0 + _GELU_C1 * (x * x))))


def _sigmoid(x):
    return 1.0 / (1.0 + jnp.exp(-x))


def _rms(x, g):
    return x * lax.rsqrt(jnp.mean(x * x, axis=-1, keepdims=True) + EPS) * g


def _norm_mod(x, g, shift, scale):
    return _rms(x, g) * (1.0 + scale) + shift


def _split(x):
    hi = x.astype(BF16)
    lo = (x - hi.astype(F32)).astype(BF16)
    return hi, lo


def _dot(a, b):
    return jnp.dot(a, b, preferred_element_type=F32)


def _ada_kernel(c_ref, w_ref, b_ref, o_ref):
    cc = c_ref[...]
    s = cc * _sigmoid(cc)
    o_ref[0] = jnp.dot(s, w_ref[0], precision=lax.Precision.HIGHEST,
                       preferred_element_type=F32) + b_ref[0]


def _ada(cc, w_ada, b_ada):
    depth, _, n = w_ada.shape
    tn = 1536
    return pl.pallas_call(
        _ada_kernel,
        grid=(depth, n // tn),
        in_specs=[pl.BlockSpec((ADA_ROWS, D_MODEL), lambda l, j: (0, 0)),
                  pl.BlockSpec((1, D_MODEL, tn), lambda l, j: (l, 0, j)),
                  pl.BlockSpec((1, 1, tn), lambda l, j: (l, 0, j))],
        out_specs=pl.BlockSpec((1, ADA_ROWS, tn), lambda l, j: (l, 0, j)),
        out_shape=jax.ShapeDtypeStruct((depth, ADA_ROWS, n), F32),
        compiler_params=pltpu.CompilerParams(vmem_limit_bytes=VMEM_LIMIT),
        name="ada",
    )(cc, w_ada, b_ada.reshape(depth, 1, n))


def _even_kernel(x_ref, xp_ref, xn_ref, ada_ref, g1_ref, win_ref, wout_ref, cw_ref, vg_ref,
                 ws_ref, bsb_ref, o_ref, *, n_tiles):
    t = pl.program_id(1)
    tm = x_ref.shape[1]
    shift, scale, gate = ada_ref[0, 0:1, :], ada_ref[0, 1:2, :], ada_ref[0, 2:3, :]
    g1 = g1_ref[...]

    x = x_ref[0]
    h = _norm_mod(x, g1, shift, scale).astype(BF16)
    proj = _dot(h, win_ref[...])
    a_h, a_c, a_b = proj[:, 0:MIX_W], proj[:, MIX_W:2 * MIX_W], proj[:, 2 * MIX_W:3 * MIX_W]
    b_u, b_v = proj[:, 3 * MIX_W:4 * MIX_W], proj[:, 4 * MIX_W:5 * MIX_W]

    z = a_c * a_h

    def edge_z(ref, row):
        hh = _norm_mod(ref[0], g1, shift, scale).astype(BF16)
        pp = _dot(hh, win_ref[:, 0:2 * MIX_W])
        return (pp[:, 0:MIX_W] * pp[:, MIX_W:2 * MIX_W])[row:row + 1, :]

    zp = jnp.where(t > 0, edge_z(xp_ref, 7), 0.0)
    zn = jnp.where(t < n_tiles - 1, edge_z(xn_ref, 0), 0.0)
    row = lax.broadcasted_iota(jnp.int32, (tm, MIX_W), 0)
    z_m1 = jnp.where(row == 0, zp, pltpu.roll(z, 1, axis=0))
    z_p1 = jnp.where(row == tm - 1, zn, pltpu.roll(z, tm - 1, axis=0))
    ya = a_b * (cw_ref[0:1, :] * z_m1 + cw_ref[1:2, :] * z + cw_ref[2:3, :] * z_p1)

    u = _gelu(b_u)
    v = _rms(_gelu(b_v), vg_ref[...]).astype(BF16)
    lane = lax.broadcasted_iota(jnp.int32, (CHUNK, CHUNK), 1)
    group_w = MIX_W // B_GROUPS
    rows = []
    for n in range(tm // CHUNK):
        cols = []
        for j in range(MIX_W // CHUNK):
            vv = v[n * CHUNK:(n + 1) * CHUNK, j * CHUNK:(j + 1) * CHUNK]
            s0 = _dot(ws_ref[2 * j], vv)
            s1 = _dot(ws_ref[2 * j + 1], vv)
            cols.append(jnp.where(lane < group_w, s0, s1))
        rows.append(jnp.concatenate(cols, axis=1) + bsb_ref[...])
    s = jnp.concatenate(rows, axis=0)
    yb = u * s

    y = jnp.concatenate([ya, yb], axis=1).astype(BF16)
    o_ref[0] = x + gate * _dot(y, wout_ref[...])


def _even_mixer(x, ada, g1, win, wout, cw, vg, ws, bsb, *, tm):
    bsz, L, _ = x.shape
    n_tiles = L // tm
    hb = tm // 8
    const = lambda *shape: pl.BlockSpec(shape, lambda b, t: (0,) * len(shape))
    return pl.pallas_call(
        functools.partial(_even_kernel, n_tiles=n_tiles),
        grid=(bsz, n_tiles),
        in_specs=[pl.BlockSpec((1, tm, D_MODEL), lambda b, t: (b, t, 0)),
                  pl.BlockSpec((1, 8, D_MODEL), lambda b, t: (b, jnp.maximum(t * hb - 1, 0), 0)),
                  pl.BlockSpec((1, 8, D_MODEL),
                               lambda b, t: (b, jnp.minimum((t + 1) * hb, L // 8 - 1), 0)),
                  pl.BlockSpec((1, N_ADA, D_MODEL), lambda b, t: (b, 0, 0)),
                  const(1, D_MODEL), const(D_MODEL, IN_W), const(2 * MIX_W, D_MODEL),
                  const(A_CONV, MIX_W), const(1, MIX_W), const(B_GROUPS, CHUNK, CHUNK),
                  const(CHUNK, MIX_W)],
        out_specs=pl.BlockSpec((1, tm, D_MODEL), lambda b, t: (b, t, 0)),
        out_shape=jax.ShapeDtypeStruct(x.shape, F32),
        compiler_params=pltpu.CompilerParams(
            dimension_semantics=("parallel", "parallel"), vmem_limit_bytes=VMEM_LIMIT),
        name="even_mixer",
    )(x, x, x, ada, g1, win, wout, cw, vg, ws, bsb)


def _head_norm(t, gain, seg_ref):
    hi, lo = _split(t * t)
    ss = _dot(hi, seg_ref[...]) + _dot(lo, seg_ref[...])
    return t * lax.rsqrt(ss * (1.0 / D_HEAD_DIM) + EPS) * gain


def _odd_in_kernel(x_ref, xp_ref, xn_ref, ada_ref, g1_ref, win_ref, cw_ref, cb_ref, cg_ref,
                   qg_ref, kg_ref, cos_ref, sin_ref, seg_ref,
                   conf_ref, q_ref, kt_ref, v_ref, ybuf_ref, *, n_tiles):
    t = pl.program_id(1)
    tm = x_ref.shape[1]
    shift, scale = ada_ref[0, 0:1, :], ada_ref[0, 1:2, :]
    g1 = g1_ref[...]

    h = _norm_mod(x_ref[0], g1, shift, scale).astype(BF16)
    proj = _dot(h, win_ref[...])

    def glu_rows(ref):
        hh = _norm_mod(ref[0], g1, shift, scale).astype(BF16)
        pp = _dot(hh, win_ref[:, 0:2 * MIX_W])
        return pp[:, 0:MIX_W] * _sigmoid(pp[:, MIX_W:2 * MIX_W])

    ybuf_ref[0:CONV_HALO, :] = jnp.where(t > 0, glu_rows(xp_ref), 0.0)
    ybuf_ref[CONV_HALO:CONV_HALO + tm, :] = proj[:, 0:MIX_W] * _sigmoid(proj[:, MIX_W:2 * MIX_W])
    ybuf_ref[CONV_HALO + tm:2 * CONV_HALO + tm, :] = jnp.where(t < n_tiles - 1, glu_rows(xn_ref), 0.0)
    base = CONV_HALO - C_CONV // 2
    acc = jnp.zeros((tm, MIX_W), F32) + cb_ref[...]
    for k in range(C_CONV):
        acc = acc + cw_ref[k:k + 1, :] * ybuf_ref[base + k:base + k + tm, :]
    yn = _rms(acc, cg_ref[...])
    conf_ref[0] = (yn * _sigmoid(yn)).astype(BF16)

    lane = lax.broadcasted_iota(jnp.int32, (tm, MIX_W), 1)
    even_lane = (lane % 2) == 0

    def rope(tn):
        partner = jnp.where(even_lane, pltpu.roll(tn, MIX_W - 1, axis=1), pltpu.roll(tn, 1, axis=1))
        return tn * cos_ref[...] + partner * sin_ref[...]

    q = rope(_head_norm(proj[:, 2 * MIX_W:3 * MIX_W], qg_ref[...], seg_ref))
    k = rope(_head_norm(proj[:, 3 * MIX_W:4 * MIX_W], kg_ref[...], seg_ref))
    q_ref[0] = (q * (D_HEAD_DIM ** -0.5 * LOG2E)).astype(BF16)
    kt_ref[0] = k.T.astype(BF16)
    v_ref[0] = proj[:, 4 * MIX_W:5 * MIX_W].astype(BF16)


def _odd_in(x, ada, g1, win, cw, cb, cg, qg, kg, cos_t, sin_t, seg, *, tm):
    bsz, L, _ = x.shape
    n_tiles = L // tm
    hb = tm // CONV_HALO
    const = lambda *shape: pl.BlockSpec(shape, lambda b, t: (0,) * len(shape))
    tok = lambda w: pl.BlockSpec((1, tm, w), lambda b, t: (b, t, 0))
    return pl.pallas_call(
        functools.partial(_odd_in_kernel, n_tiles=n_tiles),
        grid=(bsz, n_tiles),
        in_specs=[tok(D_MODEL),
                  pl.BlockSpec((1, CONV_HALO, D_MODEL),
                               lambda b, t: (b, jnp.maximum(t * hb - 1, 0), 0)),
                  pl.BlockSpec((1, CONV_HALO, D_MODEL),
                               lambda b, t: (b, jnp.minimum((t + 1) * hb, L // CONV_HALO - 1), 0)),
                  pl.BlockSpec((1, N_ADA, D_MODEL), lambda b, t: (b, 0, 0)),
                  const(1, D_MODEL), const(D_MODEL, IN_W), const(C_CONV, MIX_W),
                  const(1, MIX_W), const(1, MIX_W), const(1, MIX_W), const(1, MIX_W),
                  pl.BlockSpec((tm, MIX_W), lambda b, t: (t, 0)),
                  pl.BlockSpec((tm, MIX_W), lambda b, t: (t, 0)),
                  const(MIX_W, MIX_W)],
        out_specs=[tok(MIX_W), tok(MIX_W),
                   pl.BlockSpec((1, MIX_W, tm), lambda b, t: (b, 0, t)),
                   tok(MIX_W)],
        out_shape=[jax.ShapeDtypeStruct((bsz, L, MIX_W), BF16),
                   jax.ShapeDtypeStruct((bsz, L, MIX_W), BF16),
                   jax.ShapeDtypeStruct((bsz, MIX_W, L), BF16),
                   jax.ShapeDtypeStruct((bsz, L, MIX_W), BF16)],
        scratch_shapes=[pltpu.VMEM((tm + 2 * CONV_HALO, MIX_W), F32)],
        compiler_params=pltpu.CompilerParams(
            dimension_semantics=("parallel", "parallel"), vmem_limit_bytes=VMEM_LIMIT),
        name="odd_in",
    )(x, x, x, ada, g1, win, cw, cb, cg, qg, kg, cos_t, sin_t, seg)


def _ctx_kv_kernel(x_ref, ada_ref, g1_ref, wkv_ref, kg_ref, seg_ref, kt_ref, v_ref):
    shift, scale = ada_ref[0, 0:1, :], ada_ref[0, 1:2, :]
    h = _norm_mod(x_ref[0], g1_ref[...], shift, scale).astype(BF16)
    proj = _dot(h, wkv_ref[...])
    k = _head_norm(proj[:, 0:MIX_W], kg_ref[...], seg_ref)
    kt_ref[0] = k.T.astype(BF16)
    v_ref[0] = proj[:, MIX_W:2 * MIX_W].astype(BF16)


def _ctx_kv(ctx, ada, g1, wkv, kg, seg):
    bsz, L, _ = ctx.shape
    const = lambda *shape: pl.BlockSpec(shape, lambda b: (0,) * len(shape))
    return pl.pallas_call(
        _ctx_kv_kernel,
        grid=(bsz,),
        in_specs=[pl.BlockSpec((1, L, D_MODEL), lambda b: (b, 0, 0)),
                  pl.BlockSpec((1, N_ADA, D_MODEL), lambda b: (b, 0, 0)),
                  const(1, D_MODEL), const(D_MODEL, 2 * MIX_W), const(1, MIX_W),
                  const(MIX_W, MIX_W)],
        out_specs=[pl.BlockSpec((1, MIX_W, L), lambda b: (b, 0, 0)),
                   pl.BlockSpec((1, L, MIX_W), lambda b: (b, 0, 0))],
        out_shape=[jax.ShapeDtypeStruct((bsz, MIX_W, L), BF16),
                   jax.ShapeDtypeStruct((bsz, L, MIX_W), BF16)],
        compiler_params=pltpu.CompilerParams(
            dimension_semantics=("parallel",), vmem_limit_bytes=VMEM_LIMIT),
        name="ctx_kv",
    )(ctx, ada, g1, wkv, kg, seg)


def _attn_kernel(q_ref, ktc_ref, ktx_ref, vc_ref, vx_ref, conf_ref, x_ref, ada_ref, wout_ref,
                 lam_ref, sg_ref, o_ref, *, lam_init):
    tq = q_ref.shape[1]
    vw = 2 * D_HEAD_DIM
    lam_p = lam_ref[...]
    lam = (jnp.exp(jnp.sum(lam_p[0:1] * lam_p[1:2], axis=-1, keepdims=True))
           - jnp.exp(jnp.sum(lam_p[2:3] * lam_p[3:4], axis=-1, keepdims=True)) + lam_init)
    first_map = lax.broadcasted_iota(jnp.int32, (tq, vw), 1) < D_HEAD_DIM
    outs = []
    for hd in range(D_HEADS):
        sl = slice(hd * vw, (hd + 1) * vw)
        qh = q_ref[0, :, sl]
        maps = []
        for m in range(2):
            qm = jnp.where(first_map if m == 0 else jnp.logical_not(first_map), qh, jnp.zeros_like(qh))
            sc = _dot(qm, ktc_ref[0, sl, :])
            sx = _dot(qm, ktx_ref[0, sl, :])
            mx = jnp.maximum(jnp.max(sc, axis=-1, keepdims=True), jnp.max(sx, axis=-1, keepdims=True))
            pc, px = jnp.exp2(sc - mx), jnp.exp2(sx - mx)
            inv = 1.0 / (jnp.sum(pc, axis=-1, keepdims=True) + jnp.sum(px, axis=-1, keepdims=True))
            pv = _dot(pc.astype(BF16), vc_ref[0, :, sl]) + _dot(px.astype(BF16), vx_ref[0, :, sl])
            maps.append(pv * inv)
        o = maps[0] - lam * maps[1]
        outs.append(_rms(o, sg_ref[...]) * (1.0 - lam_init))
    y = jnp.concatenate([conf_ref[0]] + [o.astype(BF16) for o in outs], axis=1)
    o_ref[0] = x_ref[0] + ada_ref[0, 2:3, :] * _dot(y, wout_ref[...])


def _attn(q, ktc, ktx, vc, vx, conf, x, ada, wout, lam_p, sg, *, lam_init, tq):
    bsz, L, _ = x.shape
    lc = vc.shape[1]
    const = lambda *shape: pl.BlockSpec(shape, lambda b, t: (0,) * len(shape))
    tok = lambda w: pl.BlockSpec((1, tq, w), lambda b, t: (b, t, 0))
    return pl.pallas_call(
        functools.partial(_attn_kernel, lam_init=lam_init),
        grid=(bsz, L // tq),
        in_specs=[tok(MIX_W),
                  pl.BlockSpec((1, MIX_W, lc), lambda b, t: (b, 0, 0)),
                  pl.BlockSpec((1, MIX_W, L), lambda b, t: (b, 0, 0)),
                  pl.BlockSpec((1, lc, MIX_W), lambda b, t: (b, 0, 0)),
                  pl.BlockSpec((1, L, MIX_W), lambda b, t: (b, 0, 0)),
                  tok(MIX_W), tok(D_MODEL),
                  pl.BlockSpec((1, N_ADA, D_MODEL), lambda b, t: (b, 0, 0)),
                  const(2 * MIX_W, D_MODEL), const(4, D_HEAD_DIM), const(1, 2 * D_HEAD_DIM)],
        out_specs=tok(D_MODEL),
        out_shape=jax.ShapeDtypeStruct(x.shape, F32),
        compiler_params=pltpu.CompilerParams(
            dimension_semantics=("parallel", "parallel"), vmem_limit_bytes=VMEM_LIMIT),
        name="diff_attn",
    )(q, ktc, ktx, vc, vx, conf, x, ada, wout, lam_p, sg)


def _exact_ranks(s, label):
    work = s
    rank = jnp.full(s.shape, float(PEER_TOPK), F32)
    for r in range(PEER_TOPK):
        m = jnp.max(work, axis=0, keepdims=True)
        first = jnp.min(jnp.where(work == m, label, 1e9), axis=0, keepdims=True)
        sel = label == first
        rank = jnp.where(sel, float(r), rank)
        work = jnp.where(sel, -jnp.inf, work)
    return rank


def _batcher_network(n):
    def merge(lo, hi, r):
        step = 2 * r
        if step < hi - lo:
            yield from merge(lo, hi, step)
            yield from merge(lo + r, hi, step)
            yield from [(i, i + r) for i in range(lo + r, hi - r, step)]
        else:
            yield (lo, lo + r)

    def sort(lo, hi):
        if hi > lo:
            mid = lo + (hi - lo) // 2
            yield from sort(lo, mid)
            yield from sort(mid + 1, hi)
            yield from merge(lo, hi, 1)

    return tuple(sort(0, n - 1))


_SORT16 = _batcher_network(PEER_TOPK)


def _sublane_allreduce(x, op):
    for shift in (4, 2, 1):
        x = op(x, pltpu.roll(x, shift, axis=0))
    return x


def _sorted_top16(rows):
    b = list(rows)
    for i, j in _SORT16:
        b[i], b[j] = jnp.maximum(b[i], b[j]), jnp.minimum(b[i], b[j])
    for shift in (4, 2, 1):
        m = [jnp.maximum(b[i], pltpu.roll(b[PEER_TOPK - 1 - i], shift, axis=0))
             for i in range(PEER_TOPK)]
        d = PEER_TOPK // 2
        while d:
            for i in range(PEER_TOPK):
                if not i & d:
                    m[i], m[i + d] = jnp.maximum(m[i], m[i + d]), jnp.minimum(m[i], m[i + d])
            d //= 2
        b = m
    return b


def _lookup_by_rank(rows, top, values):
    out = []
    for a in rows:
        r = jnp.zeros_like(a) + values[0]
        for k in range(PEER_TOPK):
            r = jnp.where(top[k] > a, values[k + 1], r)
        out.append(r)
    return jnp.concatenate(out, axis=0)


def _tie_flag(rows, top):
    flag = jnp.zeros_like(top[0])
    for k in range(PEER_TOPK - 1):
        flag = jnp.where(top[k] == top[k + 1], 1.0, flag)
    n_ge = None
    for a in rows:
        hit = jnp.where(a >= top[PEER_TOPK - 1], 1.0, 0.0)
        n_ge = hit if n_ge is None else n_ge + hit
    n_ge = _sublane_allreduce(n_ge, jnp.add)
    return jnp.where(n_ge != float(PEER_TOPK), 1.0, flag)


def _dup_bf16(v):
    bits = pltpu.bitcast(v.astype(BF16).astype(F32), jnp.uint32)
    return bits | (bits >> 16)


def _pack_bf16(v):
    n = v.shape[0] // 2
    bits = pltpu.bitcast(v.astype(BF16).astype(F32), jnp.uint32)
    return (bits[:n] >> 16) | bits[n:]


def _peer_query_kernel(x_ref, ada_ref, g2_ref, wq_ref, skh_ref, skl_ref,
                       h2t_ref, r2_ref, a2_ref, n1_ref, b1_ref, qt_ref):
    tb = x_ref.shape[0]
    shift, scale = ada_ref[0, 3:4, :], ada_ref[0, 4:5, :]
    h2 = _norm_mod(x_ref[...], g2_ref[...], shift, scale)
    h2t = h2.T
    hh = h2t.astype(BF16)
    h2t_ref[...] = hh
    qt_ref[...] = _dot(wq_ref[...], hh)

    pos = lax.broadcasted_iota(jnp.int32, (PEER_KEYS, tb), 0)
    label1 = pos.astype(F32)
    label2 = jnp.where(pos < PEER_KEYS // 2, 2 * pos, 2 * pos - (PEER_KEYS - 1)).astype(F32)
    sub = lax.broadcasted_iota(jnp.int32, (SUBLANES, tb), 0)
    subf = sub.astype(F32)
    cand_label = ([subf, subf + 8.0] + [subf + 16.0 * k1 for k1, _ in _STAIR]
                  + [16.0 * (subf + 8.0)])
    cand_valid = [None, None] + [None if nv == SUBLANES else sub < nv for _, nv in _STAIR] + [None]

    def spread(vals):
        out = vals[0]
        for u in range(1, SUBLANES):
            out = jnp.where(sub == u, vals[u], out)
        return out

    def head(hd, carry):
        def scores(p):
            qp = qt_ref[pl.ds(pl.multiple_of((2 * hd + p) * PEER_KEYS, PEER_KEYS), PEER_KEYS), :]
            qh, ql = _split(qp)
            kh, kl = skh_ref[2 * hd + p], skl_ref[2 * hd + p]
            return _dot(kh, qh) + (_dot(kh, ql) + _dot(kl, qh))

        s1, s2 = scores(0), scores(1)
        rows1 = [s1[SUBLANES * r:SUBLANES * (r + 1)] for r in range(PEER_KEYS // SUBLANES)]
        rows2 = [s2[SUBLANES * r:SUBLANES * (r + 1)] for r in range(PEER_KEYS // SUBLANES)]
        top1, top2 = _sorted_top16(rows1), _sorted_top16(rows2)

        t2a, t2b, t1b = spread(top2[:SUBLANES]), spread(top2[SUBLANES:]), spread(top1[SUBLANES:])
        cand = ([t2a + top1[0], t2b + top1[0]] + [t2a + top1[k1] for k1, _ in _STAIR]
                + [t1b + top2[0]])
        cand = [c if v is None else jnp.where(v, c, -jnp.inf) for c, v in zip(cand, cand_valid)]
        pad = [jnp.full((SUBLANES, tb), -jnp.inf, F32)] * (PEER_TOPK - len(cand))
        best = _sorted_top16(cand + pad)
        z = functools.reduce(jnp.add, [jnp.exp(b - best[0]) for b in best])
        above = [jnp.where(c >= best[PEER_TOPK - 1], 1.0, 0.0) for c in cand]
        n_above = _sublane_allreduce(functools.reduce(jnp.add, above), jnp.add)

        def take_in_index_order():
            left = cand
            for r in range(PEER_TOPK):
                m = _sublane_allreduce(functools.reduce(jnp.maximum, left), jnp.maximum)
                first = functools.reduce(
                    jnp.minimum, [jnp.where(c == m, lab, 1e9) for c, lab in zip(left, cand_label)])
                first = _sublane_allreduce(first, jnp.minimum)
                left = [jnp.where(lab == first, -jnp.inf, c) for c, lab in zip(left, cand_label)]
            return [jnp.where(jnp.logical_and(c0 > -jnp.inf, c == -jnp.inf), 1.0, 0.0)
                    for c0, c in zip(cand, left)]

        taken = lax.cond(jnp.max(jnp.abs(n_above - float(PEER_TOPK))) > 0.0,
                         take_in_index_order, lambda: above)
        counts = [_sublane_allreduce(taken[0] + taken[1], jnp.add)]
        counts += [_sublane_allreduce(t, jnp.add) for t in taken[2:-1]]
        counts += [_sublane_allreduce(jnp.where(sub == u, taken[-1], 0.0), jnp.add)
                   for u in range(SUBLANES)]
        counts.append(0.0)
        ranks = [float(k) for k in range(PEER_TOPK + 1)]

        def fast():
            return _lookup_by_rank(rows2, top2, ranks), _lookup_by_rank(rows1, top1, counts)

        def exact():
            rank1 = _exact_ranks(s1, label1)
            n1 = jnp.zeros_like(s1)
            for k1 in range(PEER_TOPK):
                n1 = jnp.where(rank1 == float(k1), jnp.tile(counts[k1], (PEER_KEYS // SUBLANES, 1)), n1)
            return _exact_ranks(s2, label2), n1

        ties = jnp.max(jnp.maximum(_tie_flag(rows1, top1), _tie_flag(rows2, top2)))
        rank2, n1 = lax.cond(ties > 0.0, exact, fast)
        tile = lambda v: jnp.tile(v, (PEER_KEYS // SUBLANES, 1))
        r2_ref[hd] = _pack_bf16(rank2)
        a2_ref[hd] = _pack_bf16(jnp.exp(s2 - tile(top2[0])))
        n1_ref[hd] = _dup_bf16(n1)
        b1_ref[hd] = _dup_bf16(jnp.exp(s1 - tile(top1[0])) * tile(1.0 / z))
        return carry

    lax.fori_loop(0, PEER_HEADS, head, 0)


def _peer_query(x, ada, g2, wqt, skh, skl, *, tb, rows_per_ada):
    T = x.shape[0]
    qd = wqt.shape[0]
    const = lambda *shape: pl.BlockSpec(shape, lambda j: (0,) * len(shape))
    tab = lambda rows: pl.BlockSpec((PEER_HEADS, rows, tb), lambda j: (0, 0, j))
    tab_shape = lambda rows: jax.ShapeDtypeStruct((PEER_HEADS, rows, T), jnp.uint32)
    half = PEER_KEYS // 2
    return pl.pallas_call(
        _peer_query_kernel,
        grid=(T // tb,),
        in_specs=[pl.BlockSpec((tb, D_MODEL), lambda j: (j, 0)),
                  pl.BlockSpec((1, N_ADA, D_MODEL), lambda j: ((j * tb) // rows_per_ada, 0, 0)),
                  const(1, D_MODEL), const(qd, D_MODEL),
                  const(2 * PEER_HEADS, PEER_KEYS, PEER_KEYS),
                  const(2 * PEER_HEADS, PEER_KEYS, PEER_KEYS)],
        out_specs=[pl.BlockSpec((D_MODEL, tb), lambda j: (0, j)), tab(half), tab(half),
                   tab(PEER_KEYS), tab(PEER_KEYS)],
        out_shape=[jax.ShapeDtypeStruct((D_MODEL, T), BF16), tab_shape(half), tab_shape(half),
                   tab_shape(PEER_KEYS), tab_shape(PEER_KEYS)],
        scratch_shapes=[pltpu.VMEM((qd, tb), F32)],
        compiler_params=pltpu.CompilerParams(
            dimension_semantics=("parallel",), vmem_limit_bytes=VMEM_LIMIT),
        name="peer_query",
    )(x, ada, g2, wqt, skh, skl)


def _peer_dense_kernel(h2t_ref, u_ref, vt_ref, r2_ref, a2_ref, n1_ref, b1_ref, x_ref, ada_ref,
                       o_ref, acc_ref, p_ref, act_ref, *, n_et):
    i = pl.program_id(1)
    te, tb = p_ref.shape

    @pl.when(i == 0)
    def _():
        acc_ref[...] = jnp.zeros_like(acc_ref)
        p_ref[...] = jnp.zeros_like(p_ref)

    def packed_row(ref, hd, j, cs):
        word = jnp.broadcast_to(ref[hd, j:j + 1, cs], (SUBLANES, LANES))
        return pltpu.bitcast(word, BF16)

    def contract_previous():
        acc_ref[...] += _dot(vt_ref[...], p_ref[...])

    @pl.when(i < n_et)
    def _():
        act_ref[...] = _gelu(_dot(u_ref[...], h2t_ref[...])).astype(BF16)
        contract_previous()
        slabs = PEER_KEYS // BF16_ROWS
        for j in range(te // PEER_KEYS):
            for c0 in range(0, tb // LANES, GATE_CHUNKS):
                for k0 in range(0, slabs, GATE_SLABS):
                    g = [[None] * GATE_SLABS for _ in range(GATE_CHUNKS)]
                    for hd in range(PEER_HEADS):
                        for cc in range(GATE_CHUNKS):
                            cs = slice((c0 + cc) * LANES, (c0 + cc + 1) * LANES)
                            cnt = packed_row(n1_ref, hd, j, cs)
                            wgt = packed_row(b1_ref, hd, j, cs)
                            for kk in range(GATE_SLABS):
                                words = slice((k0 + kk) * SUBLANES, (k0 + kk + 1) * SUBLANES)
                                rank = pltpu.bitcast(r2_ref[hd, words, cs], BF16)
                                val = pltpu.bitcast(a2_ref[hd, words, cs], BF16)
                                term = jnp.where(rank < cnt, val, jnp.zeros_like(val)) * wgt
                                g[cc][kk] = term if hd == 0 else g[cc][kk] + term
                    for cc in range(GATE_CHUNKS):
                        cs = slice((c0 + cc) * LANES, (c0 + cc + 1) * LANES)
                        for kk in range(GATE_SLABS):
                            r0 = j * PEER_KEYS + (k0 + kk) * BF16_ROWS
                            p_ref[r0:r0 + BF16_ROWS, cs] = g[cc][kk] * act_ref[r0:r0 + BF16_ROWS, cs]

    @pl.when(i == n_et)
    def _():
        contract_previous()
        o_ref[...] = x_ref[...] + ada_ref[0, 5:6, :] * acc_ref[...].T


def _peer_dense(x, h2t, tabs, u, vt, ada, *, tb, te, rows_per_ada):
    T = x.shape[0]
    n_exp = u.shape[0]
    n_et = n_exp // te
    n1 = te // PEER_KEYS
    assert T % tb == 0 and rows_per_ada % tb == 0 and n_exp % te == 0 and n1 % SUBLANES == 0
    r2, a2, n1w, b1w = tabs
    tab = pl.BlockSpec((PEER_HEADS, PEER_KEYS // 2, tb), lambda j, i: (0, 0, j))
    tab1 = pl.BlockSpec((PEER_HEADS, n1, tb), lambda j, i: (0, jnp.minimum(i, n_et - 1), j))
    return pl.pallas_call(
        functools.partial(_peer_dense_kernel, n_et=n_et),
        grid=(T // tb, n_et + 1),
        in_specs=[pl.BlockSpec((D_MODEL, tb), lambda j, i: (0, j)),
                  pl.BlockSpec((te, D_MODEL), lambda j, i: (jnp.minimum(i, n_et - 1), 0)),
                  pl.BlockSpec((D_MODEL, te), lambda j, i: (0, jnp.maximum(i - 1, 0))),
                  tab, tab, tab1, tab1,
                  pl.BlockSpec((tb, D_MODEL), lambda j, i: (j, 0)),
                  pl.BlockSpec((1, N_ADA, D_MODEL), lambda j, i: ((j * tb) // rows_per_ada, 0, 0))],
        out_specs=pl.BlockSpec((tb, D_MODEL), lambda j, i: (j, 0)),
        out_shape=jax.ShapeDtypeStruct(x.shape, F32),
        scratch_shapes=[pltpu.VMEM((D_MODEL, tb), F32), pltpu.VMEM((te, tb), BF16),
                        pltpu.VMEM((te, tb), BF16)],
        compiler_params=pltpu.CompilerParams(
            dimension_semantics=("parallel", "arbitrary"), vmem_limit_bytes=VMEM_LIMIT),
        name="peer_dense",
    )(h2t, u, vt, r2, a2, n1w, b1w, x, ada)


def _prep_tables_kernel(u_ref, v_ref, ub_ref, vt_ref):
    ub_ref[...] = u_ref[0].astype(BF16)
    vt_ref[...] = v_ref[0].T.astype(BF16)


def _prep_tables(peer_u, peer_v, layer, *, te):
    _, n_exp, d = peer_u.shape
    rows = pl.BlockSpec((1, te, d), lambda i: (layer, i, 0))
    return pl.pallas_call(
        _prep_tables_kernel,
        grid=(n_exp // te,),
        in_specs=[rows, rows],
        out_specs=[pl.BlockSpec((te, d), lambda i: (i, 0)), pl.BlockSpec((d, te), lambda i: (0, i))],
        out_shape=[jax.ShapeDtypeStruct((n_exp, d), BF16), jax.ShapeDtypeStruct((d, n_exp), BF16)],
        compiler_params=pltpu.CompilerParams(
            dimension_semantics=("parallel",), vmem_limit_bytes=VMEM_LIMIT),
        name="prep_tables",
    )(peer_u, peer_v)


def _peer(x, ada, g2, wq, subkeys, u_tab, vt_tab, *, rows_per_ada):
    wqt = wq.T.astype(BF16)
    sk = jnp.stack([subkeys[:, 0], jnp.concatenate([subkeys[:, 1, 0::2], subkeys[:, 1, 1::2]], axis=1)],
                   axis=1)
    skh, skl = _split(sk.reshape(2 * PEER_HEADS, PEER_KEYS, -1))
    h2t, *tabs = _peer_query(x, ada, g2, wqt, skh, skl, tb=256, rows_per_ada=rows_per_ada)
    return _peer_dense(x, h2t, tabs, u_tab, vt_tab, ada, tb=1024, te=1024, rows_per_ada=rows_per_ada)


def _rope_tables(L):
    pos = jnp.arange(L, dtype=jnp.int32)
    r = (pos // GRID_W).astype(F32)
    col = (pos % GRID_W).astype(F32)
    n_freq = D_HEAD_DIM // 4
    inv = ROPE_THETA ** (-jnp.arange(n_freq, dtype=F32) / n_freq)
    ang = jnp.concatenate([r[:, None] * inv, col[:, None] * inv], axis=-1)
    cos = jnp.repeat(jnp.cos(ang), 2, axis=-1)
    sin = jnp.repeat(jnp.sin(ang), 2, axis=-1) * jnp.tile(jnp.array([-1.0, 1.0], F32), D_HEAD_DIM // 2)
    reps = MIX_W // D_HEAD_DIM
    return jnp.tile(cos, (1, reps)), jnp.tile(sin, (1, reps))


def kernel(x, c, ctx, c_ctx, w_ada, b_ada, norm1_g, norm2_g, w_in, w_out, a_conv_w, b_vnorm_g,
           b_spatial_w, b_spatial_b, c_conv_w, c_conv_b, c_norm_g, d_qk_norm_g, d_lambda,
           d_subln_g, peer_wq, peer_subkeys, peer_u, peer_v):
    bsz, L, _ = x.shape
    lc = ctx.shape[1]
    depth = w_ada.shape[0]
    assert depth == 2 and bsz + 1 <= ADA_ROWS

    cc = jnp.zeros((ADA_ROWS, D_MODEL), F32).at[:bsz].set(c).at[bsz].set(c_ctx)
    ada = _ada(cc, w_ada, b_ada).reshape(depth, ADA_ROWS, N_ADA, D_MODEL)
    row = lambda v: v.reshape(1, -1)

    ada_x, ada_c = ada[0, :bsz], ada[0, bsz:bsz + 1]
    bsb = jnp.repeat(b_spatial_b[0].T, MIX_W // B_GROUPS, axis=1)
    even = functools.partial(
        _even_mixer, g1=row(norm1_g[0]), win=w_in[0].astype(BF16), wout=w_out[0].astype(BF16),
        cw=a_conv_w[0], vg=row(b_vnorm_g[0]), ws=b_spatial_w[0].astype(BF16), bsb=bsb)
    u_tab, vt_tab = _prep_tables(peer_u, peer_v, 0, te=512)
    peer0 = functools.partial(_peer, g2=row(norm2_g[0]), wq=peer_wq[0], subkeys=peer_subkeys[0],
                              u_tab=u_tab, vt_tab=vt_tab)
    x = even(x, ada_x, tm=512)
    x = peer0(x.reshape(bsz * L, D_MODEL), ada_x, rows_per_ada=L).reshape(bsz, L, D_MODEL)
    ctx = even(ctx, jnp.broadcast_to(ada_c, (bsz, N_ADA, D_MODEL)), tm=lc)
    ctx = peer0(ctx.reshape(bsz * lc, D_MODEL), ada_c, rows_per_ada=bsz * lc).reshape(bsz, lc, D_MODEL)

    ada_x, ada_c = ada[1, :bsz], ada[1, bsz:bsz + 1]
    lam_init = 0.8 - 0.6 * math.exp(-0.3 * 1)
    win = w_in[1].astype(BF16)
    cos_t, sin_t = _rope_tables(L)
    seg_id = jnp.arange(MIX_W, dtype=jnp.int32) // D_HEAD_DIM
    seg = (seg_id[:, None] == seg_id[None, :]).astype(BF16)
    reps = MIX_W // D_HEAD_DIM
    qg, kg = row(jnp.tile(d_qk_norm_g[0, 0], reps)), row(jnp.tile(d_qk_norm_g[0, 1], reps))
    conf, q, ktx, vx = _odd_in(x, ada_x, row(norm1_g[1]), win, c_conv_w[0], row(c_conv_b[0]),
                               row(c_norm_g[0]), qg, kg, cos_t, sin_t, seg, tm=512)
    ktc, vc = _ctx_kv(ctx, jnp.broadcast_to(ada_c, (bsz, N_ADA, D_MODEL)), row(norm1_g[1]),
                      win[:, 3 * MIX_W:], kg, seg)
    x = _attn(q, ktc, ktx, vc, vx, conf, x, ada_x, w_out[1].astype(BF16), d_lambda[0],
              row(d_subln_g[0]), lam_init=lam_init, tq=512)
    u_tab, vt_tab = _prep_tables(peer_u, peer_v, 1, te=512)
    x = _peer(x.reshape(bsz * L, D_MODEL), ada_x, row(norm2_g[1]), peer_wq[1], peer_subkeys[1],
              u_tab, vt_tab, rows_per_ada=L)
    return x.reshape(bsz, L, D_MODEL)
```

```python
import functools
import math

import jax
import jax.numpy as jnp
from jax import lax
from jax.experimental import pallas as pl
from jax.experimental.pallas import tpu as pltpu

F32 = jnp.float32
BF16 = jnp.bfloat16

D_MODEL = 1024
MIX_W = 512
IN_W = 5 * MIX_W
GRID_W = 64
CHUNK = 128
B_GROUPS = 8
A_CONV = 3
C_CONV = 31
D_HEADS = 4
D_HEAD_DIM = 64
ROPE_THETA = 10000.0
N_ADA = 6
PEER_HEADS = 8
PEER_KEYS = 128
PEER_TOPK = 16
EPS = 1e-6

SUBLANES, LANES = 8, 128
BF16_ROWS = 2 * SUBLANES
GATE_CHUNKS = 4
GATE_SLABS = 4

ADA_ROWS = 16
CONV_HALO = 16
VMEM_LIMIT = 56 * 1024 * 1024

_STAIR = ((1, 8), (2, 5), (3, 4), (4, 3), (5, 2), (6, 2), (7, 2))


LOG2E = math.log2(math.e)
_GELU_C0 = -2.0 * math.sqrt(2.0 / math.pi) * LOG2E
_GELU_C1 = 0.044715 * _GELU_C0


def _gelu(x):
    return x / (1.0 + jnp.exp2(x * (_GELU_C0 + _GELU_C1 * (x * x))))


def _sigmoid(x):
    return 1.0 / (1.0 + jnp.exp(-x))


def _rms(x, g):
    return x * lax.rsqrt(jnp.mean(x * x, axis=-1, keepdims=True) + EPS) * g


def _norm_mod(x, g, shift, scale):
    return _rms(x, g) * (1.0 + scale) + shift


def _split(x):
    hi = x.astype(BF16)
    lo = (x - hi.astype(F32)).astype(BF16)
    return hi, lo


def _dot(a, b):
    return jnp.dot(a, b, preferred_element_type=F32)


def _ada_kernel(c_ref, w_ref, b_ref, o_ref):
    cc = c_ref[...]
    s = cc * _sigmoid(cc)
    o_ref[0] = jnp.dot(s, w_ref[0], precision=lax.Precision.HIGHEST,
                       preferred_element_type=F32) + b_ref[0]


def _ada(cc, w_ada, b_ada):
    depth, _, n = w_ada.shape
    tn = 1536
    return pl.pallas_call(
        _ada_kernel,
        grid=(depth, n // tn),
        in_specs=[pl.BlockSpec((ADA_ROWS, D_MODEL), lambda l, j: (0, 0)),
                  pl.BlockSpec((1, D_MODEL, tn), lambda l, j: (l, 0, j)),
                  pl.BlockSpec((1, 1, tn), lambda l, j: (l, 0, j))],
        out_specs=pl.BlockSpec((1, ADA_ROWS, tn), lambda l, j: (l, 0, j)),
        out_shape=jax.ShapeDtypeStruct((depth, ADA_ROWS, n), F32),
        compiler_params=pltpu.CompilerParams(vmem_limit_bytes=VMEM_LIMIT),
        name="ada",
    )(cc, w_ada, b_ada.reshape(depth, 1, n))


def _even_kernel(x_ref, xp_ref, xn_ref, ada_ref, g1_ref, win_ref, wout_ref, cw_ref, vg_ref,
                 ws_ref, bsb_ref, o_ref, *, n_tiles):
    t = pl.program_id(1)
    tm = x_ref.shape[1]
    shift, scale, gate = ada_ref[0, 0:1, :], ada_ref[0, 1:2, :], ada_ref[0, 2:3, :]
    g1 = g1_ref[...]

    x = x_ref[0]
    xs = jnp.concatenate([xp_ref[0], x, xn_ref[0]], axis=0)
    h = _norm_mod(xs, g1, shift, scale).astype(BF16)
    proj_all = _dot(h, win_ref[...])
    proj = proj_all[SUBLANES:SUBLANES + tm]
    a_h, a_c, a_b = proj[:, 0:MIX_W], proj[:, MIX_W:2 * MIX_W], proj[:, 2 * MIX_W:3 * MIX_W]
    b_u, b_v = proj[:, 3 * MIX_W:4 * MIX_W], proj[:, 4 * MIX_W:5 * MIX_W]

    z = a_c * a_h

    def edge_z(r):
        return proj_all[r:r + 1, 0:MIX_W] * proj_all[r:r + 1, MIX_W:2 * MIX_W]

    zp = jnp.where(t > 0, edge_z(SUBLANES - 1), 0.0)
    zn = jnp.where(t < n_tiles - 1, edge_z(SUBLANES + tm), 0.0)
    row = lax.broadcasted_iota(jnp.int32, (tm, MIX_W), 0)
    z_m1 = jnp.where(row == 0, zp, pltpu.roll(z, 1, axis=0))
    z_p1 = jnp.where(row == tm - 1, zn, pltpu.roll(z, tm - 1, axis=0))
    ya = a_b * (cw_ref[0:1, :] * z_m1 + cw_ref[1:2, :] * z + cw_ref[2:3, :] * z_p1)

    u = _gelu(b_u)
    v = _rms(_gelu(b_v), vg_ref[...]).astype(BF16)
    lane = lax.broadcasted_iota(jnp.int32, (CHUNK, CHUNK), 1)
    group_w = MIX_W // B_GROUPS
    rows = []
    for n in range(tm // CHUNK):
        cols = []
        for j in range(MIX_W // CHUNK):
            vv = v[n * CHUNK:(n + 1) * CHUNK, j * CHUNK:(j + 1) * CHUNK]
            s0 = _dot(ws_ref[2 * j], vv)
            s1 = _dot(ws_ref[2 * j + 1], vv)
            cols.append(jnp.where(lane < group_w, s0, s1))
        rows.append(jnp.concatenate(cols, axis=1) + bsb_ref[...])
    s = jnp.concatenate(rows, axis=0)
    yb = u * s

    y = jnp.concatenate([ya, yb], axis=1).astype(BF16)
    o_ref[0] = x + gate * _dot(y, wout_ref[...])


def _even_mixer(x, ada, g1, win, wout, cw, vg, ws, bsb, *, tm):
    bsz, L, _ = x.shape
    n_tiles = L // tm
    hb = tm // 8
    const = lambda *shape: pl.BlockSpec(shape, lambda b, t: (0,) * len(shape))
    return pl.pallas_call(
        functools.partial(_even_kernel, n_tiles=n_tiles),
        grid=(bsz, n_tiles),
        in_specs=[pl.BlockSpec((1, tm, D_MODEL), lambda b, t: (b, t, 0)),
                  pl.BlockSpec((1, 8, D_MODEL), lambda b, t: (b, jnp.maximum(t * hb - 1, 0), 0)),
                  pl.BlockSpec((1, 8, D_MODEL),
                               lambda b, t: (b, jnp.minimum((t + 1) * hb, L // 8 - 1), 0)),
                  pl.BlockSpec((1, N_ADA, D_MODEL), lambda b, t: (b, 0, 0)),
                  const(1, D_MODEL), const(D_MODEL, IN_W), const(2 * MIX_W, D_MODEL),
                  const(A_CONV, MIX_W), const(1, MIX_W), const(B_GROUPS, CHUNK, CHUNK),
                  const(CHUNK, MIX_W)],
        out_specs=pl.BlockSpec((1, tm, D_MODEL), lambda b, t: (b, t, 0)),
        out_shape=jax.ShapeDtypeStruct(x.shape, F32),
        compiler_params=pltpu.CompilerParams(
            dimension_semantics=("parallel", "parallel"), vmem_limit_bytes=VMEM_LIMIT),
        name="even_mixer",
    )(x, x, x, ada, g1, win, wout, cw, vg, ws, bsb)


def _head_norm(t, gain, seg_ref):
    hi, lo = _split(t * t)
    ss = _dot(hi, seg_ref[...]) + _dot(lo, seg_ref[...])
    return t * lax.rsqrt(ss * (1.0 / D_HEAD_DIM) + EPS) * gain


def _odd_in_kernel(x_ref, xp_ref, xn_ref, ada_ref, g1_ref, win_ref, cw_ref, cb_ref, cg_ref,
                   qg_ref, kg_ref, cos_ref, sin_ref, seg_ref,
                   conf_ref, q_ref, kt_ref, v_ref, *, n_tiles):
    t = pl.program_id(1)
    tm = x_ref.shape[1]
    rows_all = tm + 2 * CONV_HALO
    shift, scale = ada_ref[0, 0:1, :], ada_ref[0, 1:2, :]
    g1 = g1_ref[...]

    xs = jnp.concatenate([xp_ref[0], x_ref[0], xn_ref[0]], axis=0)
    h = _norm_mod(xs, g1, shift, scale).astype(BF16)
    proj_all = _dot(h, win_ref[...])
    proj = proj_all[CONV_HALO:CONV_HALO + tm]

    glu = proj_all[:, 0:MIX_W] * _sigmoid(proj_all[:, MIX_W:2 * MIX_W])
    row = lax.broadcasted_iota(jnp.int32, (rows_all, MIX_W), 0)
    outside = jnp.logical_or(jnp.logical_and(row < CONV_HALO, t == 0),
                             jnp.logical_and(row >= CONV_HALO + tm, t == n_tiles - 1))
    glu = jnp.where(outside, 0.0, glu)
    first = CONV_HALO - C_CONV // 2
    acc = jnp.zeros((tm, MIX_W), F32) + cb_ref[...]
    for b in range(SUBLANES):
        shifted = glu if b == 0 else pltpu.roll(glu, rows_all - b, axis=0)
        for a in range(rows_all // SUBLANES):
            k = SUBLANES * a + b - first
            if 0 <= k < C_CONV:
                acc = acc + cw_ref[k:k + 1, :] * shifted[SUBLANES * a:SUBLANES * a + tm]
    yn = _rms(acc, cg_ref[...])
    conf_ref[0] = (yn * _sigmoid(yn)).astype(BF16)

    lane = lax.broadcasted_iota(jnp.int32, (tm, MIX_W), 1)
    even_lane = (lane % 2) == 0

    def rope(tn):
        partner = jnp.where(even_lane, pltpu.roll(tn, MIX_W - 1, axis=1), pltpu.roll(tn, 1, axis=1))
        return tn * cos_ref[...] + partner * sin_ref[...]

    q = rope(_head_norm(proj[:, 2 * MIX_W:3 * MIX_W], qg_ref[...], seg_ref))
    k = rope(_head_norm(proj[:, 3 * MIX_W:4 * MIX_W], kg_ref[...], seg_ref))
    q_ref[0] = (q * (D_HEAD_DIM ** -0.5 * LOG2E)).astype(BF16)
    kt_ref[0] = k.T.astype(BF16)
    v_ref[0] = proj[:, 4 * MIX_W:5 * MIX_W].astype(BF16)


def _odd_in(x, ada, g1, win, cw, cb, cg, qg, kg, cos_t, sin_t, seg, *, tm):
    bsz, L, _ = x.shape
    n_tiles = L // tm
    hb = tm // CONV_HALO
    const = lambda *shape: pl.BlockSpec(shape, lambda b, t: (0,) * len(shape))
    tok = lambda w: pl.BlockSpec((1, tm, w), lambda b, t: (b, t, 0))
    return pl.pallas_call(
        functools.partial(_odd_in_kernel, n_tiles=n_tiles),
        grid=(bsz, n_tiles),
        in_specs=[tok(D_MODEL),
                  pl.BlockSpec((1, CONV_HALO, D_MODEL),
                               lambda b, t: (b, jnp.maximum(t * hb - 1, 0), 0)),
                  pl.BlockSpec((1, CONV_HALO, D_MODEL),
                               lambda b, t: (b, jnp.minimum((t + 1) * hb, L // CONV_HALO - 1), 0)),
                  pl.BlockSpec((1, N_ADA, D_MODEL), lambda b, t: (b, 0, 0)),
                  const(1, D_MODEL), const(D_MODEL, IN_W), const(C_CONV, MIX_W),
                  const(1, MIX_W), const(1, MIX_W), const(1, MIX_W), const(1, MIX_W),
                  pl.BlockSpec((tm, MIX_W), lambda b, t: (t, 0)),
                  pl.BlockSpec((tm, MIX_W), lambda b, t: (t, 0)),
                  const(MIX_W, MIX_W)],
        out_specs=[tok(MIX_W), tok(MIX_W),
                   pl.BlockSpec((1, MIX_W, tm), lambda b, t: (b, 0, t)),
                   tok(MIX_W)],
        out_shape=[jax.ShapeDtypeStruct((bsz, L, MIX_W), BF16),
                   jax.ShapeDtypeStruct((bsz, L, MIX_W), BF16),
                   jax.ShapeDtypeStruct((bsz, MIX_W, L), BF16),
                   jax.ShapeDtypeStruct((bsz, L, MIX_W), BF16)],
        compiler_params=pltpu.CompilerParams(
            dimension_semantics=("parallel", "parallel"), vmem_limit_bytes=VMEM_LIMIT),
        name="odd_in",
    )(x, x, x, ada, g1, win, cw, cb, cg, qg, kg, cos_t, sin_t, seg)


def _ctx_kv_kernel(x_ref, ada_ref, g1_ref, wkv_ref, kg_ref, seg_ref, kt_ref, v_ref):
    shift, scale = ada_ref[0, 0:1, :], ada_ref[0, 1:2, :]
    h = _norm_mod(x_ref[0], g1_ref[...], shift, scale).astype(BF16)
    proj = _dot(h, wkv_ref[...])
    k = _head_norm(proj[:, 0:MIX_W], kg_ref[...], seg_ref)
    kt_ref[0] = k.T.astype(BF16)
    v_ref[0] = proj[:, MIX_W:2 * MIX_W].astype(BF16)


def _ctx_kv(ctx, ada, g1, wkv, kg, seg):
    bsz, L, _ = ctx.shape
    const = lambda *shape: pl.BlockSpec(shape, lambda b: (0,) * len(shape))
    return pl.pallas_call(
        _ctx_kv_kernel,
        grid=(bsz,),
        in_specs=[pl.BlockSpec((1, L, D_MODEL), lambda b: (b, 0, 0)),
                  pl.BlockSpec((1, N_ADA, D_MODEL), lambda b: (b, 0, 0)),
                  const(1, D_MODEL), const(D_MODEL, 2 * MIX_W), const(1, MIX_W),
                  const(MIX_W, MIX_W)],
        out_specs=[pl.BlockSpec((1, MIX_W, L), lambda b: (b, 0, 0)),
                   pl.BlockSpec((1, L, MIX_W), lambda b: (b, 0, 0))],
        out_shape=[jax.ShapeDtypeStruct((bsz, MIX_W, L), BF16),
                   jax.ShapeDtypeStruct((bsz, L, MIX_W), BF16)],
        compiler_params=pltpu.CompilerParams(
            dimension_semantics=("parallel",), vmem_limit_bytes=VMEM_LIMIT),
        name="ctx_kv",
    )(ctx, ada, g1, wkv, kg, seg)


def _attn_kernel(q_ref, ktc_ref, ktx_ref, vc_ref, vx_ref, conf_ref, x_ref, ada_ref, wout_ref,
                 lam_ref, sg_ref, o_ref, *, lam_init):
    tq = q_ref.shape[1]
    vw = 2 * D_HEAD_DIM
    lam_p = lam_ref[...]
    lam = (jnp.exp(jnp.sum(lam_p[0:1] * lam_p[1:2], axis=-1, keepdims=True))
           - jnp.exp(jnp.sum(lam_p[2:3] * lam_p[3:4], axis=-1, keepdims=True)) + lam_init)
    first_map = lax.broadcasted_iota(jnp.int32, (tq, vw), 1) < D_HEAD_DIM
    outs = []
    for hd in range(D_HEADS):
        sl = slice(hd * vw, (hd + 1) * vw)
        qh = q_ref[0, :, sl]
        maps = []
        for m in range(2):
            qm = jnp.where(first_map if m == 0 else jnp.logical_not(first_map), qh, jnp.zeros_like(qh))
            sc = _dot(qm, ktc_ref[0, sl, :])
            sx = _dot(qm, ktx_ref[0, sl, :])
            mx = jnp.maximum(jnp.max(sc, axis=-1, keepdims=True), jnp.max(sx, axis=-1, keepdims=True))
            pc, px = jnp.exp2(sc - mx), jnp.exp2(sx - mx)
            inv = 1.0 / (jnp.sum(pc, axis=-1, keepdims=True) + jnp.sum(px, axis=-1, keepdims=True))
            pv = _dot(pc.astype(BF16), vc_ref[0, :, sl]) + _dot(px.astype(BF16), vx_ref[0, :, sl])
            maps.append(pv * inv)
        o = maps[0] - lam * maps[1]
        outs.append(_rms(o, sg_ref[...]) * (1.0 - lam_init))
    y = jnp.concatenate([conf_ref[0]] + [o.astype(BF16) for o in outs], axis=1)
    o_ref[0] = x_ref[0] + ada_ref[0, 2:3, :] * _dot(y, wout_ref[...])


def _attn(q, ktc, ktx, vc, vx, conf, x, ada, wout, lam_p, sg, *, lam_init, tq):
    bsz, L, _ = x.shape
    lc = vc.shape[1]
    const = lambda *shape: pl.BlockSpec(shape, lambda b, t: (0,) * len(shape))
    tok = lambda w: pl.BlockSpec((1, tq, w), lambda b, t: (b, t, 0))
    return pl.pallas_call(
        functools.partial(_attn_kernel, lam_init=lam_init),
        grid=(bsz, L // tq),
        in_specs=[tok(MIX_W),
                  pl.BlockSpec((1, MIX_W, lc), lambda b, t: (b, 0, 0)),
                  pl.BlockSpec((1, MIX_W, L), lambda b, t: (b, 0, 0)),
                  pl.BlockSpec((1, lc, MIX_W), lambda b, t: (b, 0, 0)),
                  pl.BlockSpec((1, L, MIX_W), lambda b, t: (b, 0, 0)),
                  tok(MIX_W), tok(D_MODEL),
                  pl.BlockSpec((1, N_ADA, D_MODEL), lambda b, t: (b, 0, 0)),
                  const(2 * MIX_W, D_MODEL), const(4, D_HEAD_DIM), const(1, 2 * D_HEAD_DIM)],
        out_specs=tok(D_MODEL),
        out_shape=jax.ShapeDtypeStruct(x.shape, F32),
        compiler_params=pltpu.CompilerParams(
            dimension_semantics=("parallel", "parallel"), vmem_limit_bytes=VMEM_LIMIT),
        name="diff_attn",
    )(q, ktc, ktx, vc, vx, conf, x, ada, wout, lam_p, sg)


def _exact_ranks(s, label):
    work = s
    rank = jnp.full(s.shape, float(PEER_TOPK), F32)
    for r in range(PEER_TOPK):
        m = jnp.max(work, axis=0, keepdims=True)
        first = jnp.min(jnp.where(work == m, label, 1e9), axis=0, keepdims=True)
        sel = label == first
        rank = jnp.where(sel, float(r), rank)
        work = jnp.where(sel, -jnp.inf, work)
    return rank


def _batcher_network(n):
    def merge(lo, hi, r):
        step = 2 * r
        if step < hi - lo:
            yield from merge(lo, hi, step)
            yield from merge(lo + r, hi, step)
            yield from [(i, i + r) for i in range(lo + r, hi - r, step)]
        else:
            yield (lo, lo + r)

    def sort(lo, hi):
        if hi > lo:
            mid = lo + (hi - lo) // 2
            yield from sort(lo, mid)
            yield from sort(mid + 1, hi)
            yield from merge(lo, hi, 1)

    return tuple(sort(0, n - 1))


_SORT16 = _batcher_network(PEER_TOPK)


def _sublane_allreduce(x, op):
    for shift in (4, 2, 1):
        x = op(x, pltpu.roll(x, shift, axis=0))
    return x


def _sorted_top16(rows):
    b = list(rows)
    for i, j in _SORT16:
        b[i], b[j] = jnp.maximum(b[i], b[j]), jnp.minimum(b[i], b[j])
    for shift in (4, 2, 1):
        m = [jnp.maximum(b[i], pltpu.roll(b[PEER_TOPK - 1 - i], shift, axis=0))
             for i in range(PEER_TOPK)]
        d = PEER_TOPK // 2
        while d:
            for i in range(PEER_TOPK):
                if not i & d:
                    m[i], m[i + d] = jnp.maximum(m[i], m[i + d]), jnp.minimum(m[i], m[i + d])
            d //= 2
        b = m
    return b


def _lookup_by_rank(rows, top, values):
    out = []
    for a in rows:
        r = jnp.zeros_like(a) + values[0]
        for k in range(PEER_TOPK):
            r = jnp.where(top[k] > a, values[k + 1], r)
        out.append(r)
    return jnp.concatenate(out, axis=0)


def _tie_flag(rows, top):
    flag = jnp.zeros_like(top[0])
    for k in range(PEER_TOPK - 1):
        flag = jnp.where(top[k] == top[k + 1], 1.0, flag)
    n_ge = None
    for a in rows:
        hit = jnp.where(a >= top[PEER_TOPK - 1], 1.0, 0.0)
        n_ge = hit if n_ge is None else n_ge + hit
    n_ge = _sublane_allreduce(n_ge, jnp.add)
    return jnp.where(n_ge != float(PEER_TOPK), 1.0, flag)


def _dup_bf16(v):
    bits = pltpu.bitcast(v.astype(BF16).astype(F32), jnp.uint32)
    return bits | (bits >> 16)


def _pack_bf16(v):
    n = v.shape[0] // 2
    bits = pltpu.bitcast(v.astype(BF16).astype(F32), jnp.uint32)
    return (bits[:n] >> 16) | bits[n:]


def _peer_query_kernel(x_ref, ada_ref, g2_ref, wq_ref, skh_ref, skl_ref,
                       h2t_ref, r2_ref, a2_ref, n1_ref, b1_ref, qt_ref):
    tb = x_ref.shape[0]
    shift, scale = ada_ref[0, 3:4, :], ada_ref[0, 4:5, :]
    h2 = _norm_mod(x_ref[...], g2_ref[...], shift, scale)
    h2t = h2.T
    hh = h2t.astype(BF16)
    h2t_ref[...] = hh
    qt_ref[...] = _dot(wq_ref[...], hh)

    pos = lax.broadcasted_iota(jnp.int32, (PEER_KEYS, tb), 0)
    label1 = pos.astype(F32)
    label2 = jnp.where(pos < PEER_KEYS // 2, 2 * pos, 2 * pos - (PEER_KEYS - 1)).astype(F32)
    sub = lax.broadcasted_iota(jnp.int32, (SUBLANES, tb), 0)
    subf = sub.astype(F32)
    cand_label = ([subf, subf + 8.0] + [subf + 16.0 * k1 for k1, _ in _STAIR]
                  + [16.0 * (subf + 8.0)])
    cand_valid = [None, None] + [None if nv == SUBLANES else sub < nv for _, nv in _STAIR] + [None]

    def spread(vals):
        out = vals[0]
        for u in range(1, SUBLANES):
            out = jnp.where(sub == u, vals[u], out)
        return out

    def head(hd, carry):
        def scores(p):
            qp = qt_ref[pl.ds(pl.multiple_of((2 * hd + p) * PEER_KEYS, PEER_KEYS), PEER_KEYS), :]
            qh, ql = _split(qp)
            kh, kl = skh_ref[2 * hd + p], skl_ref[2 * hd + p]
            return _dot(kh, qh) + (_dot(kh, ql) + _dot(kl, qh))

        s1, s2 = scores(0), scores(1)
        rows1 = [s1[SUBLANES * r:SUBLANES * (r + 1)] for r in range(PEER_KEYS // SUBLANES)]
        rows2 = [s2[SUBLANES * r:SUBLANES * (r + 1)] for r in range(PEER_KEYS // SUBLANES)]
        top1, top2 = _sorted_top16(rows1), _sorted_top16(rows2)

        t2a, t2b, t1b = spread(top2[:SUBLANES]), spread(top2[SUBLANES:]), spread(top1[SUBLANES:])
        cand = ([t2a + top1[0], t2b + top1[0]] + [t2a + top1[k1] for k1, _ in _STAIR]
                + [t1b + top2[0]])
        cand = [c if v is None else jnp.where(v, c, -jnp.inf) for c, v in zip(cand, cand_valid)]
        pad = [jnp.full((SUBLANES, tb), -jnp.inf, F32)] * (PEER_TOPK - len(cand))
        best = _sorted_top16(cand + pad)
        z = functools.reduce(jnp.add, [jnp.exp(b - best[0]) for b in best])
        above = [jnp.where(c >= best[PEER_TOPK - 1], 1.0, 0.0) for c in cand]
        n_above = _sublane_allreduce(functools.reduce(jnp.add, above), jnp.add)

        def take_in_index_order():
            left = cand
            for r in range(PEER_TOPK):
                m = _sublane_allreduce(functools.reduce(jnp.maximum, left), jnp.maximum)
                first = functools.reduce(
                    jnp.minimum, [jnp.where(c == m, lab, 1e9) for c, lab in zip(left, cand_label)])
                first = _sublane_allreduce(first, jnp.minimum)
                left = [jnp.where(lab == first, -jnp.inf, c) for c, lab in zip(left, cand_label)]
            return [jnp.where(jnp.logical_and(c0 > -jnp.inf, c == -jnp.inf), 1.0, 0.0)
                    for c0, c in zip(cand, left)]

        taken = lax.cond(jnp.max(jnp.abs(n_above - float(PEER_TOPK))) > 0.0,
                         take_in_index_order, lambda: above)
        counts = [_sublane_allreduce(taken[0] + taken[1], jnp.add)]
        counts += [_sublane_allreduce(t, jnp.add) for t in taken[2:-1]]
        counts += [_sublane_allreduce(jnp.where(sub == u, taken[-1], 0.0), jnp.add)
                   for u in range(SUBLANES)]
        counts.append(0.0)
        ranks = [float(k) for k in range(PEER_TOPK + 1)]

        def fast():
            return _lookup_by_rank(rows2, top2, ranks), _lookup_by_rank(rows1, top1, counts)

        def exact():
            rank1 = _exact_ranks(s1, label1)
            n1 = jnp.zeros_like(s1)
            for k1 in range(PEER_TOPK):
                n1 = jnp.where(rank1 == float(k1), jnp.tile(counts[k1], (PEER_KEYS // SUBLANES, 1)), n1)
            return _exact_ranks(s2, label2), n1

        ties = jnp.max(jnp.maximum(_tie_flag(rows1, top1), _tie_flag(rows2, top2)))
        rank2, n1 = lax.cond(ties > 0.0, exact, fast)
        tile = lambda v: jnp.tile(v, (PEER_KEYS // SUBLANES, 1))
        r2_ref[hd] = _pack_bf16(rank2)
        a2_ref[hd] = _pack_bf16(jnp.exp(s2 - tile(top2[0])))
        n1_ref[hd] = _dup_bf16(n1)
        b1_ref[hd] = _dup_bf16(jnp.exp(s1 - tile(top1[0])) * tile(1.0 / z))
        return carry

    lax.fori_loop(0, PEER_HEADS, head, 0)


def _peer_query(x, ada, g2, wqt, skh, skl, *, tb, rows_per_ada):
    T = x.shape[0]
    qd = wqt.shape[0]
    const = lambda *shape: pl.BlockSpec(shape, lambda j: (0,) * len(shape))
    tab = lambda rows: pl.BlockSpec((PEER_HEADS, rows, tb), lambda j: (0, 0, j))
    tab_shape = lambda rows: jax.ShapeDtypeStruct((PEER_HEADS, rows, T), jnp.uint32)
    half = PEER_KEYS // 2
    return pl.pallas_call(
        _peer_query_kernel,
        grid=(T // tb,),
        in_specs=[pl.BlockSpec((tb, D_MODEL), lambda j: (j, 0)),
                  pl.BlockSpec((1, N_ADA, D_MODEL), lambda j: ((j * tb) // rows_per_ada, 0, 0)),
                  const(1, D_MODEL), const(qd, D_MODEL),
                  const(2 * PEER_HEADS, PEER_KEYS, PEER_KEYS),
                  const(2 * PEER_HEADS, PEER_KEYS, PEER_KEYS)],
        out_specs=[pl.BlockSpec((D_MODEL, tb), lambda j: (0, j)), tab(half), tab(half),
                   tab(PEER_KEYS), tab(PEER_KEYS)],
        out_shape=[jax.ShapeDtypeStruct((D_MODEL, T), BF16), tab_shape(half), tab_shape(half),
                   tab_shape(PEER_KEYS), tab_shape(PEER_KEYS)],
        scratch_shapes=[pltpu.VMEM((qd, tb), F32)],
        compiler_params=pltpu.CompilerParams(
            dimension_semantics=("parallel",), vmem_limit_bytes=VMEM_LIMIT),
        name="peer_query",
    )(x, ada, g2, wqt, skh, skl)


def _peer_dense_kernel(h2t_ref, u_ref, vt_ref, r2_ref, a2_ref, n1_ref, b1_ref, x_ref, ada_ref,
                       o_ref, acc_ref, p_ref, act_ref, *, n_et):
    i = pl.program_id(1)
    te, tb = p_ref.shape

    @pl.when(i == 0)
    def _():
        acc_ref[...] = jnp.zeros_like(acc_ref)
        p_ref[...] = jnp.zeros_like(p_ref)

    def packed_row(ref, hd, j, cs):
        word = jnp.broadcast_to(ref[hd, j:j + 1, cs], (SUBLANES, LANES))
        return pltpu.bitcast(word, BF16)

    def contract_previous():
        acc_ref[...] += _dot(vt_ref[...], p_ref[...])

    @pl.when(i < n_et)
    def _():
        act_ref[...] = _gelu(_dot(u_ref[...], h2t_ref[...])).astype(BF16)
        contract_previous()
        slabs = PEER_KEYS // BF16_ROWS
        for j in range(te // PEER_KEYS):
            for c0 in range(0, tb // LANES, GATE_CHUNKS):
                for k0 in range(0, slabs, GATE_SLABS):
                    g = [[None] * GATE_SLABS for _ in range(GATE_CHUNKS)]
                    for hd in range(PEER_HEADS):
                        for cc in range(GATE_CHUNKS):
                            cs = slice((c0 + cc) * LANES, (c0 + cc + 1) * LANES)
                            cnt = packed_row(n1_ref, hd, j, cs)
                            wgt = packed_row(b1_ref, hd, j, cs)
                            for kk in range(GATE_SLABS):
                                words = slice((k0 + kk) * SUBLANES, (k0 + kk + 1) * SUBLANES)
                                rank = pltpu.bitcast(r2_ref[hd, words, cs], BF16)
                                val = pltpu.bitcast(a2_ref[hd, words, cs], BF16)
                                term = jnp.where(rank < cnt, val, jnp.zeros_like(val)) * wgt
                                g[cc][kk] = term if hd == 0 else g[cc][kk] + term
                    for cc in range(GATE_CHUNKS):
                        cs = slice((c0 + cc) * LANES, (c0 + cc + 1) * LANES)
                        for kk in range(GATE_SLABS):
                            r0 = j * PEER_KEYS + (k0 + kk) * BF16_ROWS
                            p_ref[r0:r0 + BF16_ROWS, cs] = g[cc][kk] * act_ref[r0:r0 + BF16_ROWS, cs]

    @pl.when(i == n_et)
    def _():
        contract_previous()
        o_ref[...] = x_ref[...] + ada_ref[0, 5:6, :] * acc_ref[...].T


def _peer_dense(x, h2t, tabs, u, vt, ada, *, tb, te, rows_per_ada):
    T = x.shape[0]
    n_exp = u.shape[0]
    n_et = n_exp // te
    n1 = te // PEER_KEYS
    assert T % tb == 0 and rows_per_ada % tb == 0 and n_exp % te == 0 and n1 % SUBLANES == 0
    r2, a2, n1w, b1w = tabs
    tab = pl.BlockSpec((PEER_HEADS, PEER_KEYS // 2, tb), lambda j, i: (0, 0, j))
    tab1 = pl.BlockSpec((PEER_HEADS, n1, tb), lambda j, i: (0, jnp.minimum(i, n_et - 1), j))
    return pl.pallas_call(
        functools.partial(_peer_dense_kernel, n_et=n_et),
        grid=(T // tb, n_et + 1),
        in_specs=[pl.BlockSpec((D_MODEL, tb), lambda j, i: (0, j)),
                  pl.BlockSpec((te, D_MODEL), lambda j, i: (jnp.minimum(i, n_et - 1), 0)),
                  pl.BlockSpec((D_MODEL, te), lambda j, i: (0, jnp.maximum(i - 1, 0))),
                  tab, tab, tab1, tab1,
                  pl.BlockSpec((tb, D_MODEL), lambda j, i: (j, 0)),
                  pl.BlockSpec((1, N_ADA, D_MODEL), lambda j, i: ((j * tb) // rows_per_ada, 0, 0))],
        out_specs=pl.BlockSpec((tb, D_MODEL), lambda j, i: (j, 0)),
        out_shape=jax.ShapeDtypeStruct(x.shape, F32),
        scratch_shapes=[pltpu.VMEM((D_MODEL, tb), F32), pltpu.VMEM((te, tb), BF16),
                        pltpu.VMEM((te, tb), BF16)],
        compiler_params=pltpu.CompilerParams(
            dimension_semantics=("parallel", "arbitrary"), vmem_limit_bytes=VMEM_LIMIT),
        name="peer_dense",
    )(h2t, u, vt, r2, a2, n1w, b1w, x, ada)


def _prep_tables_kernel(u_ref, v_ref, ub_ref, vt_ref):
    ub_ref[...] = u_ref[0].astype(BF16)
    vt_ref[...] = v_ref[0].T.astype(BF16)


def _prep_tables(peer_u, peer_v, layer, *, te):
    _, n_exp, d = peer_u.shape
    rows = pl.BlockSpec((1, te, d), lambda i: (layer, i, 0))
    return pl.pallas_call(
        _prep_tables_kernel,
        grid=(n_exp // te,),
        in_specs=[rows, rows],
        out_specs=[pl.BlockSpec((te, d), lambda i: (i, 0)), pl.BlockSpec((d, te), lambda i: (0, i))],
        out_shape=[jax.ShapeDtypeStruct((n_exp, d), BF16), jax.ShapeDtypeStruct((d, n_exp), BF16)],
        compiler_params=pltpu.CompilerParams(
            dimension_semantics=("parallel",), vmem_limit_bytes=VMEM_LIMIT),
        name="prep_tables",
    )(peer_u, peer_v)


def _peer(x, ada, g2, wq, subkeys, u_tab, vt_tab, *, rows_per_ada):
    wqt = wq.T.astype(BF16)
    sk = jnp.stack([subkeys[:, 0], jnp.concatenate([subkeys[:, 1, 0::2], subkeys[:, 1, 1::2]], axis=1)],
                   axis=1)
    skh, skl = _split(sk.reshape(2 * PEER_HEADS, PEER_KEYS, -1))
    h2t, *tabs = _peer_query(x, ada, g2, wqt, skh, skl, tb=256, rows_per_ada=rows_per_ada)
    return _peer_dense(x, h2t, tabs, u_tab, vt_tab, ada, tb=1024, te=1024, rows_per_ada=rows_per_ada)


def _rope_tables(L):
    pos = jnp.arange(L, dtype=jnp.int32)
    r = (pos // GRID_W).astype(F32)
    col = (pos % GRID_W).astype(F32)
    n_freq = D_HEAD_DIM // 4
    inv = ROPE_THETA ** (-jnp.arange(n_freq, dtype=F32) / n_freq)
    ang = jnp.concatenate([r[:, None] * inv, col[:, None] * inv], axis=-1)
    cos = jnp.repeat(jnp.cos(ang), 2, axis=-1)
    sin = jnp.repeat(jnp.sin(ang), 2, axis=-1) * jnp.tile(jnp.array([-1.0, 1.0], F32), D_HEAD_DIM // 2)
    reps = MIX_W // D_HEAD_DIM
    return jnp.tile(cos, (1, reps)), jnp.tile(sin, (1, reps))


def kernel(x, c, ctx, c_ctx, w_ada, b_ada, norm1_g, norm2_g, w_in, w_out, a_conv_w, b_vnorm_g,
           b_spatial_w, b_spatial_b, c_conv_w, c_conv_b, c_norm_g, d_qk_norm_g, d_lambda,
           d_subln_g, peer_wq, peer_subkeys, peer_u, peer_v):
    bsz, L, _ = x.shape
    lc = ctx.shape[1]
    depth = w_ada.shape[0]
    assert depth == 2 and bsz + 1 <= ADA_ROWS

    cc = jnp.zeros((ADA_ROWS, D_MODEL), F32).at[:bsz].set(c).at[bsz].set(c_ctx)
    ada = _ada(cc, w_ada, b_ada).reshape(depth, ADA_ROWS, N_ADA, D_MODEL)
    row = lambda v: v.reshape(1, -1)

    ada_x, ada_c = ada[0, :bsz], ada[0, bsz:bsz + 1]
    bsb = jnp.repeat(b_spatial_b[0].T, MIX_W // B_GROUPS, axis=1)
    even = functools.partial(
        _even_mixer, g1=row(norm1_g[0]), win=w_in[0].astype(BF16), wout=w_out[0].astype(BF16),
        cw=a_conv_w[0], vg=row(b_vnorm_g[0]), ws=b_spatial_w[0].astype(BF16), bsb=bsb)
    u_tab, vt_tab = _prep_tables(peer_u, peer_v, 0, te=512)
    peer0 = functools.partial(_peer, g2=row(norm2_g[0]), wq=peer_wq[0], subkeys=peer_subkeys[0],
                              u_tab=u_tab, vt_tab=vt_tab)
    x = even(x, ada_x, tm=512)
    x = peer0(x.reshape(bsz * L, D_MODEL), ada_x, rows_per_ada=L).reshape(bsz, L, D_MODEL)
    ctx = even(ctx, jnp.broadcast_to(ada_c, (bsz, N_ADA, D_MODEL)), tm=lc)
    ctx = peer0(ctx.reshape(bsz * lc, D_MODEL), ada_c, rows_per_ada=bsz * lc).reshape(bsz, lc, D_MODEL)

    ada_x, ada_c = ada[1, :bsz], ada[1, bsz:bsz + 1]
    lam_init = 0.8 - 0.6 * math.exp(-0.3 * 1)
    win = w_in[1].astype(BF16)
    cos_t, sin_t = _rope_tables(L)
    seg_id = jnp.arange(MIX_W, dtype=jnp.int32) // D_HEAD_DIM
    seg = (seg_id[:, None] == seg_id[None, :]).astype(BF16)
    reps = MIX_W // D_HEAD_DIM
    qg, kg = row(jnp.tile(d_qk_norm_g[0, 0], reps)), row(jnp.tile(d_qk_norm_g[0, 1], reps))
    conf, q, ktx, vx = _odd_in(x, ada_x, row(norm1_g[1]), win, c_conv_w[0], row(c_conv_b[0]),
                               row(c_norm_g[0]), qg, kg, cos_t, sin_t, seg, tm=512)
    ktc, vc = _ctx_kv(ctx, jnp.broadcast_to(ada_c, (bsz, N_ADA, D_MODEL)), row(norm1_g[1]),
                      win[:, 3 * MIX_W:], kg, seg)
    x = _attn(q, ktc, ktx, vc, vx, conf, x, ada_x, w_out[1].astype(BF16), d_lambda[0],
              row(d_subln_g[0]), lam_init=lam_init, tq=512)
    u_tab, vt_tab = _prep_tables(peer_u, peer_v, 1, te=512)
    x = _peer(x.reshape(bsz * L, D_MODEL), ada_x, row(norm2_g[1]), peer_wq[1], peer_subkeys[1],
              u_tab, vt_tab, rows_per_ada=L)
    return x.reshape(bsz, L, D_MODEL)
```

```python
import functools
import math

import jax
import jax.numpy as jnp
from jax import lax
from jax.experimental import pallas as pl
from jax.experimental.pallas import tpu as pltpu

F32 = jnp.float32
BF16 = jnp.bfloat16

D_MODEL = 1024
MIX_W = 512
IN_W = 5 * MIX_W
GRID_W = 64
CHUNK = 128
B_GROUPS = 8
A_CONV = 3
C_CONV = 31
D_HEADS = 4
D_HEAD_DIM = 64
ROPE_THETA = 10000.0
N_ADA = 6
PEER_HEADS = 8
PEER_KEYS = 128
PEER_TOPK = 16
EPS = 1e-6

SUBLANES, LANES = 8, 128
BF16_ROWS = 2 * SUBLANES
GATE_CHUNKS = 4
GATE_SLABS = 4
B1_ROW_SHIFT = 4

ADA_ROWS = 16
CONV_HALO = 16
VMEM_LIMIT = 56 * 1024 * 1024

_STAIR = ((1, 8), (2, 5), (3, 4), (4, 3), (5, 2), (6, 2), (7, 2))


LOG2E = math.log2(math.e)
_GELU_C0 = -2.0 * math.sqrt(2.0 / math.pi) * LOG2E
_GELU_C1 = 0.044715 * _GELU_C0


def _gelu(x):
    return x / (1.0 + jnp.exp2(x * (_GELU_C0 + _GELU_C1 * (x * x))))


def _sigmoid(x):
    return 1.0 / (1.0 + jnp.exp(-x))


def _rms(x, g):
    return x * lax.rsqrt(jnp.mean(x * x, axis=-1, keepdims=True) + EPS) * g


def _norm_mod(x, g, shift, scale):
    return _rms(x, g) * (1.0 + scale) + shift


def _split(x):
    hi = x.astype(BF16)
    lo = (x - hi.astype(F32)).astype(BF16)
    return hi, lo


def _dot(a, b):
    return jnp.dot(a, b, preferred_element_type=F32)


def _ada_kernel(c_ref, w_ref, b_ref, o_ref):
    cc = c_ref[...]
    s = cc * _sigmoid(cc)
    o_ref[0] = jnp.dot(s, w_ref[0], precision=lax.Precision.HIGHEST,
                       preferred_element_type=F32) + b_ref[0]


def _ada(cc, w_ada, b_ada):
    depth, _, n = w_ada.shape
    tn = 1536
    return pl.pallas_call(
        _ada_kernel,
        grid=(depth, n // tn),
        in_specs=[pl.BlockSpec((ADA_ROWS, D_MODEL), lambda l, j: (0, 0)),
                  pl.BlockSpec((1, D_MODEL, tn), lambda l, j: (l, 0, j)),
                  pl.BlockSpec((1, 1, tn), lambda l, j: (l, 0, j))],
        out_specs=pl.BlockSpec((1, ADA_ROWS, tn), lambda l, j: (l, 0, j)),
        out_shape=jax.ShapeDtypeStruct((depth, ADA_ROWS, n), F32),
        compiler_params=pltpu.CompilerParams(vmem_limit_bytes=VMEM_LIMIT),
        name="ada",
    )(cc, w_ada, b_ada.reshape(depth, 1, n))


def _even_kernel(x_ref, xp_ref, xn_ref, ada_ref, g1_ref, win_ref, wout_ref, cw_ref, vg_ref,
                 ws_ref, bsb_ref, o_ref, *, n_tiles):
    t = pl.program_id(1)
    tm = x_ref.shape[1]
    shift, scale, gate = ada_ref[0, 0:1, :], ada_ref[0, 1:2, :], ada_ref[0, 2:3, :]
    g1 = g1_ref[...]

    x = x_ref[0]
    xs = jnp.concatenate([xp_ref[0], x, xn_ref[0]], axis=0)
    h = _norm_mod(xs, g1, shift, scale).astype(BF16)
    proj_all = _dot(h, win_ref[...])
    proj = proj_all[SUBLANES:SUBLANES + tm]
    a_h, a_c, a_b = proj[:, 0:MIX_W], proj[:, MIX_W:2 * MIX_W], proj[:, 2 * MIX_W:3 * MIX_W]
    b_u, b_v = proj[:, 3 * MIX_W:4 * MIX_W], proj[:, 4 * MIX_W:5 * MIX_W]

    z = a_c * a_h

    def edge_z(r):
        return proj_all[r:r + 1, 0:MIX_W] * proj_all[r:r + 1, MIX_W:2 * MIX_W]

    zp = jnp.where(t > 0, edge_z(SUBLANES - 1), 0.0)
    zn = jnp.where(t < n_tiles - 1, edge_z(SUBLANES + tm), 0.0)
    row = lax.broadcasted_iota(jnp.int32, (tm, MIX_W), 0)
    z_m1 = jnp.where(row == 0, zp, pltpu.roll(z, 1, axis=0))
    z_p1 = jnp.where(row == tm - 1, zn, pltpu.roll(z, tm - 1, axis=0))
    ya = a_b * (cw_ref[0:1, :] * z_m1 + cw_ref[1:2, :] * z + cw_ref[2:3, :] * z_p1)

    u = _gelu(b_u)
    v = _rms(_gelu(b_v), vg_ref[...]).astype(BF16)
    lane = lax.broadcasted_iota(jnp.int32, (CHUNK, CHUNK), 1)
    group_w = MIX_W // B_GROUPS
    rows = []
    for n in range(tm // CHUNK):
        cols = []
        for j in range(MIX_W // CHUNK):
            vv = v[n * CHUNK:(n + 1) * CHUNK, j * CHUNK:(j + 1) * CHUNK]
            s0 = _dot(ws_ref[2 * j], vv)
            s1 = _dot(ws_ref[2 * j + 1], vv)
            cols.append(jnp.where(lane < group_w, s0, s1))
        rows.append(jnp.concatenate(cols, axis=1) + bsb_ref[...])
    s = jnp.concatenate(rows, axis=0)
    yb = u * s

    y = jnp.concatenate([ya, yb], axis=1).astype(BF16)
    o_ref[0] = x + gate * _dot(y, wout_ref[...])


def _even_mixer(x, ada, g1, win, wout, cw, vg, ws, bsb, *, tm):
    bsz, L, _ = x.shape
    n_tiles = L // tm
    hb = tm // 8
    const = lambda *shape: pl.BlockSpec(shape, lambda b, t: (0,) * len(shape))
    return pl.pallas_call(
        functools.partial(_even_kernel, n_tiles=n_tiles),
        grid=(bsz, n_tiles),
        in_specs=[pl.BlockSpec((1, tm, D_MODEL), lambda b, t: (b, t, 0)),
                  pl.BlockSpec((1, 8, D_MODEL), lambda b, t: (b, jnp.maximum(t * hb - 1, 0), 0)),
                  pl.BlockSpec((1, 8, D_MODEL),
                               lambda b, t: (b, jnp.minimum((t + 1) * hb, L // 8 - 1), 0)),
                  pl.BlockSpec((1, N_ADA, D_MODEL), lambda b, t: (b, 0, 0)),
                  const(1, D_MODEL), const(D_MODEL, IN_W), const(2 * MIX_W, D_MODEL),
                  const(A_CONV, MIX_W), const(1, MIX_W), const(B_GROUPS, CHUNK, CHUNK),
                  const(CHUNK, MIX_W)],
        out_specs=pl.BlockSpec((1, tm, D_MODEL), lambda b, t: (b, t, 0)),
        out_shape=jax.ShapeDtypeStruct(x.shape, F32),
        compiler_params=pltpu.CompilerParams(
            dimension_semantics=("parallel", "parallel"), vmem_limit_bytes=VMEM_LIMIT),
        name="even_mixer",
    )(x, x, x, ada, g1, win, wout, cw, vg, ws, bsb)


def _head_norm(t, gain, seg_ref):
    hi, lo = _split(t * t)
    ss = _dot(hi, seg_ref[...]) + _dot(lo, seg_ref[...])
    return t * lax.rsqrt(ss * (1.0 / D_HEAD_DIM) + EPS) * gain


def _odd_in_kernel(x_ref, xp_ref, xn_ref, ada_ref, g1_ref, win_ref, cw_ref, cb_ref, cg_ref,
                   qg_ref, kg_ref, cos_ref, sin_ref, seg_ref,
                   conf_ref, q_ref, kt_ref, v_ref, *, n_tiles):
    t = pl.program_id(1)
    tm = x_ref.shape[1]
    rows_all = tm + 2 * CONV_HALO
    shift, scale = ada_ref[0, 0:1, :], ada_ref[0, 1:2, :]
    g1 = g1_ref[...]

    xs = jnp.concatenate([xp_ref[0], x_ref[0], xn_ref[0]], axis=0)
    h = _norm_mod(xs, g1, shift, scale).astype(BF16)
    proj_all = _dot(h, win_ref[...])
    proj = proj_all[CONV_HALO:CONV_HALO + tm]

    glu = proj_all[:, 0:MIX_W] * _sigmoid(proj_all[:, MIX_W:2 * MIX_W])
    row = lax.broadcasted_iota(jnp.int32, (rows_all, MIX_W), 0)
    outside = jnp.logical_or(jnp.logical_and(row < CONV_HALO, t == 0),
                             jnp.logical_and(row >= CONV_HALO + tm, t == n_tiles - 1))
    glu = jnp.where(outside, 0.0, glu)
    first = CONV_HALO - C_CONV // 2
    acc = jnp.zeros((tm, MIX_W), F32) + cb_ref[...]
    for b in range(SUBLANES):
        shifted = glu if b == 0 else pltpu.roll(glu, rows_all - b, axis=0)
        for a in range(rows_all // SUBLANES):
            k = SUBLANES * a + b - first
            if 0 <= k < C_CONV:
                acc = acc + cw_ref[k:k + 1, :] * shifted[SUBLANES * a:SUBLANES * a + tm]
    yn = _rms(acc, cg_ref[...])
    conf_ref[0] = (yn * _sigmoid(yn)).astype(BF16)

    lane = lax.broadcasted_iota(jnp.int32, (tm, MIX_W), 1)
    even_lane = (lane % 2) == 0

    def rope(tn):
        partner = jnp.where(even_lane, pltpu.roll(tn, MIX_W - 1, axis=1), pltpu.roll(tn, 1, axis=1))
        return tn * cos_ref[...] + partner * sin_ref[...]

    q = rope(_head_norm(proj[:, 2 * MIX_W:3 * MIX_W], qg_ref[...], seg_ref))
    k = rope(_head_norm(proj[:, 3 * MIX_W:4 * MIX_W], kg_ref[...], seg_ref))
    q_ref[0] = (q * (D_HEAD_DIM ** -0.5 * LOG2E)).astype(BF16)
    kt_ref[0] = k.T.astype(BF16)
    v_ref[0] = proj[:, 4 * MIX_W:5 * MIX_W].astype(BF16)


def _odd_in(x, ada, g1, win, cw, cb, cg, qg, kg, cos_t, sin_t, seg, *, tm):
    bsz, L, _ = x.shape
    n_tiles = L // tm
    hb = tm // CONV_HALO
    const = lambda *shape: pl.BlockSpec(shape, lambda b, t: (0,) * len(shape))
    tok = lambda w: pl.BlockSpec((1, tm, w), lambda b, t: (b, t, 0))
    return pl.pallas_call(
        functools.partial(_odd_in_kernel, n_tiles=n_tiles),
        grid=(bsz, n_tiles),
        in_specs=[tok(D_MODEL),
                  pl.BlockSpec((1, CONV_HALO, D_MODEL),
                               lambda b, t: (b, jnp.maximum(t * hb - 1, 0), 0)),
                  pl.BlockSpec((1, CONV_HALO, D_MODEL),
                               lambda b, t: (b, jnp.minimum((t + 1) * hb, L // CONV_HALO - 1), 0)),
                  pl.BlockSpec((1, N_ADA, D_MODEL), lambda b, t: (b, 0, 0)),
                  const(1, D_MODEL), const(D_MODEL, IN_W), const(C_CONV, MIX_W),
                  const(1, MIX_W), const(1, MIX_W), const(1, MIX_W), const(1, MIX_W),
                  pl.BlockSpec((tm, MIX_W), lambda b, t: (t, 0)),
                  pl.BlockSpec((tm, MIX_W), lambda b, t: (t, 0)),
                  const(MIX_W, MIX_W)],
        out_specs=[tok(MIX_W), tok(MIX_W),
                   pl.BlockSpec((1, MIX_W, tm), lambda b, t: (b, 0, t)),
                   tok(MIX_W)],
        out_shape=[jax.ShapeDtypeStruct((bsz, L, MIX_W), BF16),
                   jax.ShapeDtypeStruct((bsz, L, MIX_W), BF16),
                   jax.ShapeDtypeStruct((bsz, MIX_W, L), BF16),
                   jax.ShapeDtypeStruct((bsz, L, MIX_W), BF16)],
        compiler_params=pltpu.CompilerParams(
            dimension_semantics=("parallel", "parallel"), vmem_limit_bytes=VMEM_LIMIT),
        name="odd_in",
    )(x, x, x, ada, g1, win, cw, cb, cg, qg, kg, cos_t, sin_t, seg)


def _ctx_kv_kernel(x_ref, ada_ref, g1_ref, wkv_ref, kg_ref, seg_ref, kt_ref, v_ref):
    shift, scale = ada_ref[0, 0:1, :], ada_ref[0, 1:2, :]
    h = _norm_mod(x_ref[0], g1_ref[...], shift, scale).astype(BF16)
    proj = _dot(h, wkv_ref[...])
    k = _head_norm(proj[:, 0:MIX_W], kg_ref[...], seg_ref)
    kt_ref[0] = k.T.astype(BF16)
    v_ref[0] = proj[:, MIX_W:2 * MIX_W].astype(BF16)


def _ctx_kv(ctx, ada, g1, wkv, kg, seg):
    bsz, L, _ = ctx.shape
    const = lambda *shape: pl.BlockSpec(shape, lambda b: (0,) * len(shape))
    return pl.pallas_call(
        _ctx_kv_kernel,
        grid=(bsz,),
        in_specs=[pl.BlockSpec((1, L, D_MODEL), lambda b: (b, 0, 0)),
                  pl.BlockSpec((1, N_ADA, D_MODEL), lambda b: (b, 0, 0)),
                  const(1, D_MODEL), const(D_MODEL, 2 * MIX_W), const(1, MIX_W),
                  const(MIX_W, MIX_W)],
        out_specs=[pl.BlockSpec((1, MIX_W, L), lambda b: (b, 0, 0)),
                   pl.BlockSpec((1, L, MIX_W), lambda b: (b, 0, 0))],
        out_shape=[jax.ShapeDtypeStruct((bsz, MIX_W, L), BF16),
                   jax.ShapeDtypeStruct((bsz, L, MIX_W), BF16)],
        compiler_params=pltpu.CompilerParams(
            dimension_semantics=("parallel",), vmem_limit_bytes=VMEM_LIMIT),
        name="ctx_kv",
    )(ctx, ada, g1, wkv, kg, seg)


def _attn_kernel(q_ref, ktc_ref, ktx_ref, vc_ref, vx_ref, conf_ref, x_ref, ada_ref, wout_ref,
                 lam_ref, sg_ref, o_ref, *, lam_init):
    tq = q_ref.shape[1]
    vw = 2 * D_HEAD_DIM
    lam_p = lam_ref[...]
    lam = (jnp.exp(jnp.sum(lam_p[0:1] * lam_p[1:2], axis=-1, keepdims=True))
           - jnp.exp(jnp.sum(lam_p[2:3] * lam_p[3:4], axis=-1, keepdims=True)) + lam_init)
    first_map = lax.broadcasted_iota(jnp.int32, (tq, vw), 1) < D_HEAD_DIM
    outs = []
    for hd in range(D_HEADS):
        sl = slice(hd * vw, (hd + 1) * vw)
        qh = q_ref[0, :, sl]
        maps = []
        for m in range(2):
            qm = jnp.where(first_map if m == 0 else jnp.logical_not(first_map), qh, jnp.zeros_like(qh))
            sc = _dot(qm, ktc_ref[0, sl, :])
            sx = _dot(qm, ktx_ref[0, sl, :])
            mx = jnp.maximum(jnp.max(sc, axis=-1, keepdims=True), jnp.max(sx, axis=-1, keepdims=True))
            pc, px = jnp.exp2(sc - mx), jnp.exp2(sx - mx)
            inv = 1.0 / (jnp.sum(pc, axis=-1, keepdims=True) + jnp.sum(px, axis=-1, keepdims=True))
            pv = _dot(pc.astype(BF16), vc_ref[0, :, sl]) + _dot(px.astype(BF16), vx_ref[0, :, sl])
            maps.append(pv * inv)
        o = maps[0] - lam * maps[1]
        outs.append(_rms(o, sg_ref[...]) * (1.0 - lam_init))
    y = jnp.concatenate([conf_ref[0]] + [o.astype(BF16) for o in outs], axis=1)
    o_ref[0] = x_ref[0] + ada_ref[0, 2:3, :] * _dot(y, wout_ref[...])


def _attn(q, ktc, ktx, vc, vx, conf, x, ada, wout, lam_p, sg, *, lam_init, tq):
    bsz, L, _ = x.shape
    lc = vc.shape[1]
    const = lambda *shape: pl.BlockSpec(shape, lambda b, t: (0,) * len(shape))
    tok = lambda w: pl.BlockSpec((1, tq, w), lambda b, t: (b, t, 0))
    return pl.pallas_call(
        functools.partial(_attn_kernel, lam_init=lam_init),
        grid=(bsz, L // tq),
        in_specs=[tok(MIX_W),
                  pl.BlockSpec((1, MIX_W, lc), lambda b, t: (b, 0, 0)),
                  pl.BlockSpec((1, MIX_W, L), lambda b, t: (b, 0, 0)),
                  pl.BlockSpec((1, lc, MIX_W), lambda b, t: (b, 0, 0)),
                  pl.BlockSpec((1, L, MIX_W), lambda b, t: (b, 0, 0)),
                  tok(MIX_W), tok(D_MODEL),
                  pl.BlockSpec((1, N_ADA, D_MODEL), lambda b, t: (b, 0, 0)),
                  const(2 * MIX_W, D_MODEL), const(4, D_HEAD_DIM), const(1, 2 * D_HEAD_DIM)],
        out_specs=tok(D_MODEL),
        out_shape=jax.ShapeDtypeStruct(x.shape, F32),
        compiler_params=pltpu.CompilerParams(
            dimension_semantics=("parallel", "parallel"), vmem_limit_bytes=VMEM_LIMIT),
        name="diff_attn",
    )(q, ktc, ktx, vc, vx, conf, x, ada, wout, lam_p, sg)


def _exact_ranks(s, label):
    work = s
    rank = jnp.full(s.shape, float(PEER_TOPK), F32)
    for r in range(PEER_TOPK):
        m = jnp.max(work, axis=0, keepdims=True)
        first = jnp.min(jnp.where(work == m, label, 1e9), axis=0, keepdims=True)
        sel = label == first
        rank = jnp.where(sel, float(r), rank)
        work = jnp.where(sel, -jnp.inf, work)
    return rank


def _batcher_network(n):
    def merge(lo, hi, r):
        step = 2 * r
        if step < hi - lo:
            yield from merge(lo, hi, step)
            yield from merge(lo + r, hi, step)
            yield from [(i, i + r) for i in range(lo + r, hi - r, step)]
        else:
            yield (lo, lo + r)

    def sort(lo, hi):
        if hi > lo:
            mid = lo + (hi - lo) // 2
            yield from sort(lo, mid)
            yield from sort(mid + 1, hi)
            yield from merge(lo, hi, 1)

    return tuple(sort(0, n - 1))


_SORT16 = _batcher_network(PEER_TOPK)


def _sublane_allreduce(x, op):
    for shift in (4, 2, 1):
        x = op(x, pltpu.roll(x, shift, axis=0))
    return x


def _sorted_top16(rows):
    b = list(rows)
    for i, j in _SORT16:
        b[i], b[j] = jnp.maximum(b[i], b[j]), jnp.minimum(b[i], b[j])
    for shift in (4, 2, 1):
        m = [jnp.maximum(b[i], pltpu.roll(b[PEER_TOPK - 1 - i], shift, axis=0))
             for i in range(PEER_TOPK)]
        d = PEER_TOPK // 2
        while d:
            for i in range(PEER_TOPK):
                if not i & d:
                    m[i], m[i + d] = jnp.maximum(m[i], m[i + d]), jnp.minimum(m[i], m[i + d])
            d //= 2
        b = m
    return b


def _lookup_by_rank(rows, top, values):
    out = []
    for a in rows:
        r = jnp.zeros_like(a) + values[0]
        for k in range(PEER_TOPK):
            r = jnp.where(top[k] > a, values[k + 1], r)
        out.append(r)
    return jnp.concatenate(out, axis=0)


def _tie_flag(rows, top):
    flag = jnp.zeros_like(top[0])
    for k in range(PEER_TOPK - 1):
        flag = jnp.where(top[k] == top[k + 1], 1.0, flag)
    n_ge = None
    for a in rows:
        hit = jnp.where(a >= top[PEER_TOPK - 1], 1.0, 0.0)
        n_ge = hit if n_ge is None else n_ge + hit
    n_ge = _sublane_allreduce(n_ge, jnp.add)
    return jnp.where(n_ge != float(PEER_TOPK), 1.0, flag)


def _dup_bf16(v):
    bits = pltpu.bitcast(v.astype(BF16).astype(F32), jnp.uint32)
    return bits | (bits >> 16)


def _pack_bf16(v):
    n = v.shape[0] // 2
    bits = pltpu.bitcast(v.astype(BF16).astype(F32), jnp.uint32)
    return (bits[:n] >> 16) | bits[n:]


def _peer_query_kernel(x_ref, ada_ref, g2_ref, wq_ref, skh_ref, skl_ref,
                       h2t_ref, r2_ref, a2_ref, n1_ref, b1_ref, qt_ref):
    tb = x_ref.shape[0]
    shift, scale = ada_ref[0, 3:4, :], ada_ref[0, 4:5, :]
    h2 = _norm_mod(x_ref[...], g2_ref[...], shift, scale)
    h2t = h2.T
    hh = h2t.astype(BF16)
    h2t_ref[...] = hh
    qt_ref[...] = _dot(wq_ref[...], hh)

    pos = lax.broadcasted_iota(jnp.int32, (PEER_KEYS, tb), 0)
    label1 = pos.astype(F32)
    label2 = jnp.where(pos < PEER_KEYS // 2, 2 * pos, 2 * pos - (PEER_KEYS - 1)).astype(F32)
    sub = lax.broadcasted_iota(jnp.int32, (SUBLANES, tb), 0)
    subf = sub.astype(F32)
    cand_label = ([subf, subf + 8.0] + [subf + 16.0 * k1 for k1, _ in _STAIR]
                  + [16.0 * (subf + 8.0)])
    cand_valid = [None, None] + [None if nv == SUBLANES else sub < nv for _, nv in _STAIR] + [None]

    def spread(vals):
        out = vals[0]
        for u in range(1, SUBLANES):
            out = jnp.where(sub == u, vals[u], out)
        return out

    def head(hd, carry):
        def scores(p):
            qp = qt_ref[pl.ds(pl.multiple_of((2 * hd + p) * PEER_KEYS, PEER_KEYS), PEER_KEYS), :]
            qh, ql = _split(qp)
            kh, kl = skh_ref[2 * hd + p], skl_ref[2 * hd + p]
            return _dot(kh, qh) + (_dot(kh, ql) + _dot(kl, qh))

        s1, s2 = scores(0), scores(1)
        rows1 = [s1[SUBLANES * r:SUBLANES * (r + 1)] for r in range(PEER_KEYS // SUBLANES)]
        rows2 = [s2[SUBLANES * r:SUBLANES * (r + 1)] for r in range(PEER_KEYS // SUBLANES)]
        top1, top2 = _sorted_top16(rows1), _sorted_top16(rows2)

        t2a, t2b, t1b = spread(top2[:SUBLANES]), spread(top2[SUBLANES:]), spread(top1[SUBLANES:])
        cand = ([t2a + top1[0], t2b + top1[0]] + [t2a + top1[k1] for k1, _ in _STAIR]
                + [t1b + top2[0]])
        cand = [c if v is None else jnp.where(v, c, -jnp.inf) for c, v in zip(cand, cand_valid)]
        pad = [jnp.full((SUBLANES, tb), -jnp.inf, F32)] * (PEER_TOPK - len(cand))
        best = _sorted_top16(cand + pad)
        z = functools.reduce(jnp.add, [jnp.exp(b - best[0]) for b in best])
        above = [jnp.where(c >= best[PEER_TOPK - 1], 1.0, 0.0) for c in cand]
        n_above = _sublane_allreduce(functools.reduce(jnp.add, above), jnp.add)

        def take_in_index_order():
            left = cand
            for r in range(PEER_TOPK):
                m = _sublane_allreduce(functools.reduce(jnp.maximum, left), jnp.maximum)
                first = functools.reduce(
                    jnp.minimum, [jnp.where(c == m, lab, 1e9) for c, lab in zip(left, cand_label)])
                first = _sublane_allreduce(first, jnp.minimum)
                left = [jnp.where(lab == first, -jnp.inf, c) for c, lab in zip(left, cand_label)]
            return [jnp.where(jnp.logical_and(c0 > -jnp.inf, c == -jnp.inf), 1.0, 0.0)
                    for c0, c in zip(cand, left)]

        taken = lax.cond(jnp.max(jnp.abs(n_above - float(PEER_TOPK))) > 0.0,
                         take_in_index_order, lambda: above)
        counts = [_sublane_allreduce(taken[0] + taken[1], jnp.add)]
        counts += [_sublane_allreduce(t, jnp.add) for t in taken[2:-1]]
        counts += [_sublane_allreduce(jnp.where(sub == u, taken[-1], 0.0), jnp.add)
                   for u in range(SUBLANES)]
        counts.append(0.0)
        ranks = [float(k) for k in range(PEER_TOPK + 1)]

        def fast():
            return _lookup_by_rank(rows2, top2, ranks), _lookup_by_rank(rows1, top1, counts)

        def exact():
            rank1 = _exact_ranks(s1, label1)
            n1 = jnp.zeros_like(s1)
            for k1 in range(PEER_TOPK):
                n1 = jnp.where(rank1 == float(k1), jnp.tile(counts[k1], (PEER_KEYS // SUBLANES, 1)), n1)
            return _exact_ranks(s2, label2), n1

        ties = jnp.max(jnp.maximum(_tie_flag(rows1, top1), _tie_flag(rows2, top2)))
        rank2, n1 = lax.cond(ties > 0.0, exact, fast)
        tile = lambda v: jnp.tile(v, (PEER_KEYS // SUBLANES, 1))
        r2_ref[hd] = _pack_bf16(rank2)
        a2_ref[hd] = _pack_bf16(jnp.exp(s2 - tile(top2[0])))
        n1_ref[hd] = _dup_bf16(n1)
        wrows = [pltpu.roll(jnp.exp(a - top1[0]) * (1.0 / z), B1_ROW_SHIFT, axis=0) for a in rows1]
        b1_ref[hd] = _dup_bf16(jnp.concatenate(wrows, axis=0))
        return carry

    lax.fori_loop(0, PEER_HEADS, head, 0)


def _peer_query(x, ada, g2, wqt, skh, skl, *, tb, rows_per_ada):
    T = x.shape[0]
    qd = wqt.shape[0]
    const = lambda *shape: pl.BlockSpec(shape, lambda j: (0,) * len(shape))
    tab = lambda rows: pl.BlockSpec((PEER_HEADS, rows, tb), lambda j: (0, 0, j))
    tab_shape = lambda rows: jax.ShapeDtypeStruct((PEER_HEADS, rows, T), jnp.uint32)
    half = PEER_KEYS // 2
    return pl.pallas_call(
        _peer_query_kernel,
        grid=(T // tb,),
        in_specs=[pl.BlockSpec((tb, D_MODEL), lambda j: (j, 0)),
                  pl.BlockSpec((1, N_ADA, D_MODEL), lambda j: ((j * tb) // rows_per_ada, 0, 0)),
                  const(1, D_MODEL), const(qd, D_MODEL),
                  const(2 * PEER_HEADS, PEER_KEYS, PEER_KEYS),
                  const(2 * PEER_HEADS, PEER_KEYS, PEER_KEYS)],
        out_specs=[pl.BlockSpec((D_MODEL, tb), lambda j: (0, j)), tab(half), tab(half),
                   tab(PEER_KEYS), tab(PEER_KEYS)],
        out_shape=[jax.ShapeDtypeStruct((D_MODEL, T), BF16), tab_shape(half), tab_shape(half),
                   tab_shape(PEER_KEYS), tab_shape(PEER_KEYS)],
        scratch_shapes=[pltpu.VMEM((qd, tb), F32)],
        compiler_params=pltpu.CompilerParams(
            dimension_semantics=("parallel",), vmem_limit_bytes=VMEM_LIMIT),
        name="peer_query",
    )(x, ada, g2, wqt, skh, skl)


def _peer_dense_kernel(h2t_ref, u_ref, vt_ref, r2_ref, a2_ref, n1_ref, b1_ref, x_ref, ada_ref,
                       o_ref, acc_ref, p_ref, act_ref, *, n_et):
    i = pl.program_id(1)
    te, tb = p_ref.shape

    @pl.when(i == 0)
    def _():
        acc_ref[...] = jnp.zeros_like(acc_ref)
        p_ref[...] = jnp.zeros_like(p_ref)

    def packed_row(ref, hd, j, cs):
        word = jnp.broadcast_to(ref[hd, j:j + 1, cs], (SUBLANES, LANES))
        return pltpu.bitcast(word, BF16)

    def contract_previous():
        acc_ref[...] += _dot(vt_ref[...], p_ref[...])

    @pl.when(i < n_et)
    def _():
        act_ref[...] = _gelu(_dot(u_ref[...], h2t_ref[...])).astype(BF16)
        contract_previous()
        slabs = PEER_KEYS // BF16_ROWS
        for j in range(te // PEER_KEYS):
            for c0 in range(0, tb // LANES, GATE_CHUNKS):
                for k0 in range(0, slabs, GATE_SLABS):
                    g = [[None] * GATE_SLABS for _ in range(GATE_CHUNKS)]
                    for hd in range(PEER_HEADS):
                        for cc in range(GATE_CHUNKS):
                            cs = slice((c0 + cc) * LANES, (c0 + cc + 1) * LANES)
                            cnt = packed_row(n1_ref, hd, j, cs)
                            wgt = packed_row(b1_ref, hd, (j + B1_ROW_SHIFT) % SUBLANES, cs)
                            for kk in range(GATE_SLABS):
                                words = slice((k0 + kk) * SUBLANES, (k0 + kk + 1) * SUBLANES)
                                rank = pltpu.bitcast(r2_ref[hd, words, cs], BF16)
                                val = pltpu.bitcast(a2_ref[hd, words, cs], BF16)
                                term = jnp.where(rank < cnt, val, jnp.zeros_like(val)) * wgt
                                g[cc][kk] = term if hd == 0 else g[cc][kk] + term
                    for cc in range(GATE_CHUNKS):
                        cs = slice((c0 + cc) * LANES, (c0 + cc + 1) * LANES)
                        for kk in range(GATE_SLABS):
                            r0 = j * PEER_KEYS + (k0 + kk) * BF16_ROWS
                            p_ref[r0:r0 + BF16_ROWS, cs] = g[cc][kk] * act_ref[r0:r0 + BF16_ROWS, cs]

    @pl.when(i == n_et)
    def _():
        contract_previous()
        o_ref[...] = x_ref[...] + ada_ref[0, 5:6, :] * acc_ref[...].T


def _peer_dense(x, h2t, tabs, u, vt, ada, *, tb, te, rows_per_ada):
    T = x.shape[0]
    n_exp = u.shape[0]
    n_et = n_exp // te
    n1 = te // PEER_KEYS
    assert T % tb == 0 and rows_per_ada % tb == 0 and n_exp % te == 0 and n1 == SUBLANES
    r2, a2, n1w, b1w = tabs
    tab = pl.BlockSpec((PEER_HEADS, PEER_KEYS // 2, tb), lambda j, i: (0, 0, j))
    tab1 = pl.BlockSpec((PEER_HEADS, n1, tb), lambda j, i: (0, jnp.minimum(i, n_et - 1), j))
    return pl.pallas_call(
        functools.partial(_peer_dense_kernel, n_et=n_et),
        grid=(T // tb, n_et + 1),
        in_specs=[pl.BlockSpec((D_MODEL, tb), lambda j, i: (0, j)),
                  pl.BlockSpec((te, D_MODEL), lambda j, i: (jnp.minimum(i, n_et - 1), 0)),
                  pl.BlockSpec((D_MODEL, te), lambda j, i: (0, jnp.maximum(i - 1, 0))),
                  tab, tab, tab1, tab1,
                  pl.BlockSpec((tb, D_MODEL), lambda j, i: (j, 0)),
                  pl.BlockSpec((1, N_ADA, D_MODEL), lambda j, i: ((j * tb) // rows_per_ada, 0, 0))],
        out_specs=pl.BlockSpec((tb, D_MODEL), lambda j, i: (j, 0)),
        out_shape=jax.ShapeDtypeStruct(x.shape, F32),
        scratch_shapes=[pltpu.VMEM((D_MODEL, tb), F32), pltpu.VMEM((te, tb), BF16),
                        pltpu.VMEM((te, tb), BF16)],
        compiler_params=pltpu.CompilerParams(
            dimension_semantics=("parallel", "arbitrary"), vmem_limit_bytes=VMEM_LIMIT),
        name="peer_dense",
    )(h2t, u, vt, r2, a2, n1w, b1w, x, ada)


def _prep_tables_kernel(u_ref, v_ref, ub_ref, vt_ref):
    ub_ref[...] = u_ref[0].astype(BF16)
    vt_ref[...] = v_ref[0].T.astype(BF16)


def _prep_tables(peer_u, peer_v, layer, *, te):
    _, n_exp, d = peer_u.shape
    rows = pl.BlockSpec((1, te, d), lambda i: (layer, i, 0))
    return pl.pallas_call(
        _prep_tables_kernel,
        grid=(n_exp // te,),
        in_specs=[rows, rows],
        out_specs=[pl.BlockSpec((te, d), lambda i: (i, 0)), pl.BlockSpec((d, te), lambda i: (0, i))],
        out_shape=[jax.ShapeDtypeStruct((n_exp, d), BF16), jax.ShapeDtypeStruct((d, n_exp), BF16)],
        compiler_params=pltpu.CompilerParams(
            dimension_semantics=("parallel",), vmem_limit_bytes=VMEM_LIMIT),
        name="prep_tables",
    )(peer_u, peer_v)


def _peer(x, ada, g2, wq, subkeys, u_tab, vt_tab, *, rows_per_ada):
    wqt = wq.T.astype(BF16)
    sk = jnp.stack([subkeys[:, 0], jnp.concatenate([subkeys[:, 1, 0::2], subkeys[:, 1, 1::2]], axis=1)],
                   axis=1)
    skh, skl = _split(sk.reshape(2 * PEER_HEADS, PEER_KEYS, -1))
    h2t, *tabs = _peer_query(x, ada, g2, wqt, skh, skl, tb=256, rows_per_ada=rows_per_ada)
    return _peer_dense(x, h2t, tabs, u_tab, vt_tab, ada, tb=1024, te=1024, rows_per_ada=rows_per_ada)


def _rope_tables(L):
    pos = jnp.arange(L, dtype=jnp.int32)
    r = (pos // GRID_W).astype(F32)
    col = (pos % GRID_W).astype(F32)
    n_freq = D_HEAD_DIM // 4
    inv = ROPE_THETA ** (-jnp.arange(n_freq, dtype=F32) / n_freq)
    ang = jnp.concatenate([r[:, None] * inv, col[:, None] * inv], axis=-1)
    cos = jnp.repeat(jnp.cos(ang), 2, axis=-1)
    sin = jnp.repeat(jnp.sin(ang), 2, axis=-1) * jnp.tile(jnp.array([-1.0, 1.0], F32), D_HEAD_DIM // 2)
    reps = MIX_W // D_HEAD_DIM
    return jnp.tile(cos, (1, reps)), jnp.tile(sin, (1, reps))


def kernel(x, c, ctx, c_ctx, w_ada, b_ada, norm1_g, norm2_g, w_in, w_out, a_conv_w, b_vnorm_g,
           b_spatial_w, b_spatial_b, c_conv_w, c_conv_b, c_norm_g, d_qk_norm_g, d_lambda,
           d_subln_g, peer_wq, peer_subkeys, peer_u, peer_v):
    bsz, L, _ = x.shape
    lc = ctx.shape[1]
    depth = w_ada.shape[0]
    assert depth == 2 and bsz + 1 <= ADA_ROWS

    cc = jnp.zeros((ADA_ROWS, D_MODEL), F32).at[:bsz].set(c).at[bsz].set(c_ctx)
    ada = _ada(cc, w_ada, b_ada).reshape(depth, ADA_ROWS, N_ADA, D_MODEL)
    row = lambda v: v.reshape(1, -1)

    ada_x, ada_c = ada[0, :bsz], ada[0, bsz:bsz + 1]
    bsb = jnp.repeat(b_spatial_b[0].T, MIX_W // B_GROUPS, axis=1)
    even = functools.partial(
        _even_mixer, g1=row(norm1_g[0]), win=w_in[0].astype(BF16), wout=w_out[0].astype(BF16),
        cw=a_conv_w[0], vg=row(b_vnorm_g[0]), ws=b_spatial_w[0].astype(BF16), bsb=bsb)
    u_tab, vt_tab = _prep_tables(peer_u, peer_v, 0, te=512)
    peer0 = functools.partial(_peer, g2=row(norm2_g[0]), wq=peer_wq[0], subkeys=peer_subkeys[0],
                              u_tab=u_tab, vt_tab=vt_tab)
    x = even(x, ada_x, tm=512)
    x = peer0(x.reshape(bsz * L, D_MODEL), ada_x, rows_per_ada=L).reshape(bsz, L, D_MODEL)
    ctx = even(ctx, jnp.broadcast_to(ada_c, (bsz, N_ADA, D_MODEL)), tm=lc)
    ctx = peer0(ctx.reshape(bsz * lc, D_MODEL), ada_c, rows_per_ada=bsz * lc).reshape(bsz, lc, D_MODEL)

    ada_x, ada_c = ada[1, :bsz], ada[1, bsz:bsz + 1]
    lam_init = 0.8 - 0.6 * math.exp(-0.3 * 1)
    win = w_in[1].astype(BF16)
    cos_t, sin_t = _rope_tables(L)
    seg_id = jnp.arange(MIX_W, dtype=jnp.int32) // D_HEAD_DIM
    seg = (seg_id[:, None] == seg_id[None, :]).astype(BF16)
    reps = MIX_W // D_HEAD_DIM
    qg, kg = row(jnp.tile(d_qk_norm_g[0, 0], reps)), row(jnp.tile(d_qk_norm_g[0, 1], reps))
    conf, q, ktx, vx = _odd_in(x, ada_x, row(norm1_g[1]), win, c_conv_w[0], row(c_conv_b[0]),
                               row(c_norm_g[0]), qg, kg, cos_t, sin_t, seg, tm=512)
    ktc, vc = _ctx_kv(ctx, jnp.broadcast_to(ada_c, (bsz, N_ADA, D_MODEL)), row(norm1_g[1]),
                      win[:, 3 * MIX_W:], kg, seg)
    x = _attn(q, ktc, ktx, vc, vx, conf, x, ada_x, w_out[1].astype(BF16), d_lambda[0],
              row(d_subln_g[0]), lam_init=lam_init, tq=512)
    u_tab, vt_tab = _prep_tables(peer_u, peer_v, 1, te=512)
    x = _peer(x.reshape(bsz * L, D_MODEL), ada_x, row(norm2_g[1]), peer_wq[1], peer_subkeys[1],
              u_tab, vt_tab, rows_per_ada=L)
    return x.reshape(bsz, L, D_MODEL)
```

```python
import functools
import math

import jax
import jax.numpy as jnp
from jax import lax
from jax.experimental import pallas as pl
from jax.experimental.pallas import tpu as pltpu

F32 = jnp.float32
BF16 = jnp.bfloat16

D_MODEL = 1024
MIX_W = 512
IN_W = 5 * MIX_W
GRID_W = 64
CHUNK = 128
B_GROUPS = 8
A_CONV = 3
C_CONV = 31
D_HEADS = 4
D_HEAD_DIM = 64
ROPE_THETA = 10000.0
N_ADA = 6
PEER_HEADS = 8
PEER_KEYS = 128
PEER_TOPK = 16
EPS = 1e-6

SUBLANES, LANES = 8, 128
BF16_ROWS = 2 * SUBLANES
GATE_CHUNKS = 4
GATE_SLABS = 4
B1_ROW_SHIFT = 4

ADA_ROWS = 16
CONV_HALO = 16
VMEM_LIMIT = 56 * 1024 * 1024

_STAIR = ((1, 8), (2, 5), (3, 4), (4, 3), (5, 2), (6, 2), (7, 2))


LOG2E = math.log2(math.e)
_GELU_C0 = -2.0 * math.sqrt(2.0 / math.pi) * LOG2E
_GELU_C1 = 0.044715 * _GELU_C0


def _gelu(x):
    return x / (1.0 + jnp.exp2(x * (_GELU_C0 + _GELU_C1 * (x * x))))


def _sigmoid(x):
    return 1.0 / (1.0 + jnp.exp(-x))


def _rms(x, g):
    return x * lax.rsqrt(jnp.mean(x * x, axis=-1, keepdims=True) + EPS) * g


def _norm_mod(x, g, shift, scale):
    return _rms(x, g) * (1.0 + scale) + shift


def _split(x):
    hi = x.astype(BF16)
    lo = (x - hi.astype(F32)).astype(BF16)
    return hi, lo


def _dot(a, b):
    return jnp.dot(a, b, preferred_element_type=F32)


def _ada_kernel(c_ref, w_ref, b_ref, o_ref):
    cc = c_ref[...]
    s = cc * _sigmoid(cc)
    o_ref[0] = jnp.dot(s, w_ref[0], precision=lax.Precision.HIGHEST,
                       preferred_element_type=F32) + b_ref[0]


def _ada(cc, w_ada, b_ada):
    depth, _, n = w_ada.shape
    tn = 1536
    return pl.pallas_call(
        _ada_kernel,
        grid=(depth, n // tn),
        in_specs=[pl.BlockSpec((ADA_ROWS, D_MODEL), lambda l, j: (0, 0)),
                  pl.BlockSpec((1, D_MODEL, tn), lambda l, j: (l, 0, j)),
                  pl.BlockSpec((1, 1, tn), lambda l, j: (l, 0, j))],
        out_specs=pl.BlockSpec((1, ADA_ROWS, tn), lambda l, j: (l, 0, j)),
        out_shape=jax.ShapeDtypeStruct((depth, ADA_ROWS, n), F32),
        compiler_params=pltpu.CompilerParams(vmem_limit_bytes=VMEM_LIMIT),
        name="ada",
    )(cc, w_ada, b_ada.reshape(depth, 1, n))


def _even_kernel(x_ref, xp_ref, xn_ref, ada_ref, g1_ref, win_ref, wout_ref, cw_ref, vg_ref,
                 ws_ref, bsb_ref, o_ref, *, n_tiles):
    t = pl.program_id(1)
    tm = x_ref.shape[1]
    shift, scale, gate = ada_ref[0, 0:1, :], ada_ref[0, 1:2, :], ada_ref[0, 2:3, :]
    g1 = g1_ref[...]

    x = x_ref[0]
    xs = jnp.concatenate([xp_ref[0], x, xn_ref[0]], axis=0)
    h = _norm_mod(xs, g1, shift, scale).astype(BF16)
    proj_all = _dot(h, win_ref[...])
    proj = proj_all[SUBLANES:SUBLANES + tm]
    a_h, a_c, a_b = proj[:, 0:MIX_W], proj[:, MIX_W:2 * MIX_W], proj[:, 2 * MIX_W:3 * MIX_W]
    b_u, b_v = proj[:, 3 * MIX_W:4 * MIX_W], proj[:, 4 * MIX_W:5 * MIX_W]

    z = a_c * a_h

    def edge_z(r):
        return proj_all[r:r + 1, 0:MIX_W] * proj_all[r:r + 1, MIX_W:2 * MIX_W]

    zp = jnp.where(t > 0, edge_z(SUBLANES - 1), 0.0)
    zn = jnp.where(t < n_tiles - 1, edge_z(SUBLANES + tm), 0.0)
    row = lax.broadcasted_iota(jnp.int32, (tm, MIX_W), 0)
    z_m1 = jnp.where(row == 0, zp, pltpu.roll(z, 1, axis=0))
    z_p1 = jnp.where(row == tm - 1, zn, pltpu.roll(z, tm - 1, axis=0))
    ya = a_b * (cw_ref[0:1, :] * z_m1 + cw_ref[1:2, :] * z + cw_ref[2:3, :] * z_p1)

    u = _gelu(b_u)
    v = _rms(_gelu(b_v), vg_ref[...]).astype(BF16)
    lane = lax.broadcasted_iota(jnp.int32, (CHUNK, CHUNK), 1)
    group_w = MIX_W // B_GROUPS
    rows = []
    for n in range(tm // CHUNK):
        cols = []
        for j in range(MIX_W // CHUNK):
            vv = v[n * CHUNK:(n + 1) * CHUNK, j * CHUNK:(j + 1) * CHUNK]
            s0 = _dot(ws_ref[2 * j], vv)
            s1 = _dot(ws_ref[2 * j + 1], vv)
            cols.append(jnp.where(lane < group_w, s0, s1))
        rows.append(jnp.concatenate(cols, axis=1) + bsb_ref[...])
    s = jnp.concatenate(rows, axis=0)
    yb = u * s

    y = jnp.concatenate([ya, yb], axis=1).astype(BF16)
    o_ref[0] = x + gate * _dot(y, wout_ref[...])


def _even_mixer(x, ada, g1, win, wout, cw, vg, ws, bsb, *, tm):
    bsz, L, _ = x.shape
    n_tiles = L // tm
    hb = tm // 8
    const = lambda *shape: pl.BlockSpec(shape, lambda b, t: (0,) * len(shape))
    return pl.pallas_call(
        functools.partial(_even_kernel, n_tiles=n_tiles),
        grid=(bsz, n_tiles),
        in_specs=[pl.BlockSpec((1, tm, D_MODEL), lambda b, t: (b, t, 0)),
                  pl.BlockSpec((1, 8, D_MODEL), lambda b, t: (b, jnp.maximum(t * hb - 1, 0), 0)),
                  pl.BlockSpec((1, 8, D_MODEL),
                               lambda b, t: (b, jnp.minimum((t + 1) * hb, L // 8 - 1), 0)),
                  pl.BlockSpec((1, N_ADA, D_MODEL), lambda b, t: (b, 0, 0)),
                  const(1, D_MODEL), const(D_MODEL, IN_W), const(2 * MIX_W, D_MODEL),
                  const(A_CONV, MIX_W), const(1, MIX_W), const(B_GROUPS, CHUNK, CHUNK),
                  const(CHUNK, MIX_W)],
        out_specs=pl.BlockSpec((1, tm, D_MODEL), lambda b, t: (b, t, 0)),
        out_shape=jax.ShapeDtypeStruct(x.shape, F32),
        compiler_params=pltpu.CompilerParams(
            dimension_semantics=("parallel", "parallel"), vmem_limit_bytes=VMEM_LIMIT),
        name="even_mixer",
    )(x, x, x, ada, g1, win, wout, cw, vg, ws, bsb)


def _head_norm(t, gain, seg_ref):
    hi, lo = _split(t * t)
    ss = _dot(hi, seg_ref[...]) + _dot(lo, seg_ref[...])
    return t * lax.rsqrt(ss * (1.0 / D_HEAD_DIM) + EPS) * gain


def _odd_in_kernel(x_ref, xp_ref, xn_ref, ada_ref, g1_ref, win_ref, cw_ref, cb_ref, cg_ref,
                   qg_ref, kg_ref, cos_ref, sin_ref, seg_ref,
                   conf_ref, q_ref, kt_ref, v_ref, *, n_tiles):
    t = pl.program_id(1)
    tm = x_ref.shape[1]
    rows_all = tm + 2 * CONV_HALO
    shift, scale = ada_ref[0, 0:1, :], ada_ref[0, 1:2, :]
    g1 = g1_ref[...]

    xs = jnp.concatenate([xp_ref[0], x_ref[0], xn_ref[0]], axis=0)
    h = _norm_mod(xs, g1, shift, scale).astype(BF16)
    proj_all = _dot(h, win_ref[...])
    proj = proj_all[CONV_HALO:CONV_HALO + tm]

    glu = proj_all[:, 0:MIX_W] * _sigmoid(proj_all[:, MIX_W:2 * MIX_W])
    row = lax.broadcasted_iota(jnp.int32, (rows_all, MIX_W), 0)
    outside = jnp.logical_or(jnp.logical_and(row < CONV_HALO, t == 0),
                             jnp.logical_and(row >= CONV_HALO + tm, t == n_tiles - 1))
    glu = jnp.where(outside, 0.0, glu)
    first = CONV_HALO - C_CONV // 2
    acc = jnp.zeros((tm, MIX_W), F32) + cb_ref[...]
    for b in range(SUBLANES):
        shifted = glu if b == 0 else pltpu.roll(glu, rows_all - b, axis=0)
        for a in range(rows_all // SUBLANES):
            k = SUBLANES * a + b - first
            if 0 <= k < C_CONV:
                acc = acc + cw_ref[k:k + 1, :] * shifted[SUBLANES * a:SUBLANES * a + tm]
    yn = _rms(acc, cg_ref[...])
    conf_ref[0] = (yn * _sigmoid(yn)).astype(BF16)

    lane = lax.broadcasted_iota(jnp.int32, (tm, MIX_W), 1)
    even_lane = (lane % 2) == 0

    def rope(tn):
        partner = jnp.where(even_lane, pltpu.roll(tn, MIX_W - 1, axis=1), pltpu.roll(tn, 1, axis=1))
        return tn * cos_ref[...] + partner * sin_ref[...]

    q = rope(_head_norm(proj[:, 2 * MIX_W:3 * MIX_W], qg_ref[...], seg_ref))
    k = rope(_head_norm(proj[:, 3 * MIX_W:4 * MIX_W], kg_ref[...], seg_ref))
    q_ref[0] = (q * (D_HEAD_DIM ** -0.5 * LOG2E)).astype(BF16)
    kt_ref[0] = k.T.astype(BF16)
    v_ref[0] = proj[:, 4 * MIX_W:5 * MIX_W].astype(BF16)


def _odd_in(x, ada, g1, win, cw, cb, cg, qg, kg, cos_t, sin_t, seg, *, tm):
    bsz, L, _ = x.shape
    n_tiles = L // tm
    hb = tm // CONV_HALO
    const = lambda *shape: pl.BlockSpec(shape, lambda b, t: (0,) * len(shape))
    tok = lambda w: pl.BlockSpec((1, tm, w), lambda b, t: (b, t, 0))
    return pl.pallas_call(
        functools.partial(_odd_in_kernel, n_tiles=n_tiles),
        grid=(bsz, n_tiles),
        in_specs=[tok(D_MODEL),
                  pl.BlockSpec((1, CONV_HALO, D_MODEL),
                               lambda b, t: (b, jnp.maximum(t * hb - 1, 0), 0)),
                  pl.BlockSpec((1, CONV_HALO, D_MODEL),
                               lambda b, t: (b, jnp.minimum((t + 1) * hb, L // CONV_HALO - 1), 0)),
                  pl.BlockSpec((1, N_ADA, D_MODEL), lambda b, t: (b, 0, 0)),
                  const(1, D_MODEL), const(D_MODEL, IN_W), const(C_CONV, MIX_W),
                  const(1, MIX_W), const(1, MIX_W), const(1, MIX_W), const(1, MIX_W),
                  pl.BlockSpec((tm, MIX_W), lambda b, t: (t, 0)),
                  pl.BlockSpec((tm, MIX_W), lambda b, t: (t, 0)),
                  const(MIX_W, MIX_W)],
        out_specs=[tok(MIX_W), tok(MIX_W),
                   pl.BlockSpec((1, MIX_W, tm), lambda b, t: (b, 0, t)),
                   tok(MIX_W)],
        out_shape=[jax.ShapeDtypeStruct((bsz, L, MIX_W), BF16),
                   jax.ShapeDtypeStruct((bsz, L, MIX_W), BF16),
                   jax.ShapeDtypeStruct((bsz, MIX_W, L), BF16),
                   jax.ShapeDtypeStruct((bsz, L, MIX_W), BF16)],
        compiler_params=pltpu.CompilerParams(
            dimension_semantics=("parallel", "parallel"), vmem_limit_bytes=VMEM_LIMIT),
        name="odd_in",
    )(x, x, x, ada, g1, win, cw, cb, cg, qg, kg, cos_t, sin_t, seg)


def _ctx_kv_kernel(x_ref, ada_ref, g1_ref, wkv_ref, kg_ref, seg_ref, kt_ref, v_ref):
    shift, scale = ada_ref[0, 0:1, :], ada_ref[0, 1:2, :]
    h = _norm_mod(x_ref[0], g1_ref[...], shift, scale).astype(BF16)
    proj = _dot(h, wkv_ref[...])
    k = _head_norm(proj[:, 0:MIX_W], kg_ref[...], seg_ref)
    kt_ref[0] = k.T.astype(BF16)
    v_ref[0] = proj[:, MIX_W:2 * MIX_W].astype(BF16)


def _ctx_kv(ctx, ada, g1, wkv, kg, seg):
    bsz, L, _ = ctx.shape
    const = lambda *shape: pl.BlockSpec(shape, lambda b: (0,) * len(shape))
    return pl.pallas_call(
        _ctx_kv_kernel,
        grid=(bsz,),
        in_specs=[pl.BlockSpec((1, L, D_MODEL), lambda b: (b, 0, 0)),
                  pl.BlockSpec((1, N_ADA, D_MODEL), lambda b: (b, 0, 0)),
                  const(1, D_MODEL), const(D_MODEL, 2 * MIX_W), const(1, MIX_W),
                  const(MIX_W, MIX_W)],
        out_specs=[pl.BlockSpec((1, MIX_W, L), lambda b: (b, 0, 0)),
                   pl.BlockSpec((1, L, MIX_W), lambda b: (b, 0, 0))],
        out_shape=[jax.ShapeDtypeStruct((bsz, MIX_W, L), BF16),
                   jax.ShapeDtypeStruct((bsz, L, MIX_W), BF16)],
        compiler_params=pltpu.CompilerParams(
            dimension_semantics=("parallel",), vmem_limit_bytes=VMEM_LIMIT),
        name="ctx_kv",
    )(ctx, ada, g1, wkv, kg, seg)


def _attn_kernel(q_ref, ktc_ref, ktx_ref, vc_ref, vx_ref, conf_ref, x_ref, ada_ref, wout_ref,
                 lam_ref, sg_ref, o_ref, *, lam_init):
    tq = q_ref.shape[1]
    vw = 2 * D_HEAD_DIM
    lam_p = lam_ref[...]
    lam = (jnp.exp(jnp.sum(lam_p[0:1] * lam_p[1:2], axis=-1, keepdims=True))
           - jnp.exp(jnp.sum(lam_p[2:3] * lam_p[3:4], axis=-1, keepdims=True)) + lam_init)
    first_map = lax.broadcasted_iota(jnp.int32, (tq, vw), 1) < D_HEAD_DIM
    outs = []
    for hd in range(D_HEADS):
        sl = slice(hd * vw, (hd + 1) * vw)
        qh = q_ref[0, :, sl]
        maps = []
        for m in range(2):
            qm = jnp.where(first_map if m == 0 else jnp.logical_not(first_map), qh, jnp.zeros_like(qh))
            sc = _dot(qm, ktc_ref[0, sl, :])
            sx = _dot(qm, ktx_ref[0, sl, :])
            mx = jnp.maximum(jnp.max(sc, axis=-1, keepdims=True), jnp.max(sx, axis=-1, keepdims=True))
            pc, px = jnp.exp2(sc - mx), jnp.exp2(sx - mx)
            inv = 1.0 / (jnp.sum(pc, axis=-1, keepdims=True) + jnp.sum(px, axis=-1, keepdims=True))
            pv = _dot(pc.astype(BF16), vc_ref[0, :, sl]) + _dot(px.astype(BF16), vx_ref[0, :, sl])
            maps.append(pv * inv)
        o = maps[0] - lam * maps[1]
        outs.append(_rms(o, sg_ref[...]) * (1.0 - lam_init))
    y = jnp.concatenate([conf_ref[0]] + [o.astype(BF16) for o in outs], axis=1)
    o_ref[0] = x_ref[0] + ada_ref[0, 2:3, :] * _dot(y, wout_ref[...])


def _attn(q, ktc, ktx, vc, vx, conf, x, ada, wout, lam_p, sg, *, lam_init, tq):
    bsz, L, _ = x.shape
    lc = vc.shape[1]
    const = lambda *shape: pl.BlockSpec(shape, lambda b, t: (0,) * len(shape))
    tok = lambda w: pl.BlockSpec((1, tq, w), lambda b, t: (b, t, 0))
    return pl.pallas_call(
        functools.partial(_attn_kernel, lam_init=lam_init),
        grid=(bsz, L // tq),
        in_specs=[tok(MIX_W),
                  pl.BlockSpec((1, MIX_W, lc), lambda b, t: (b, 0, 0)),
                  pl.BlockSpec((1, MIX_W, L), lambda b, t: (b, 0, 0)),
                  pl.BlockSpec((1, lc, MIX_W), lambda b, t: (b, 0, 0)),
                  pl.BlockSpec((1, L, MIX_W), lambda b, t: (b, 0, 0)),
                  tok(MIX_W), tok(D_MODEL),
                  pl.BlockSpec((1, N_ADA, D_MODEL), lambda b, t: (b, 0, 0)),
                  const(2 * MIX_W, D_MODEL), const(4, D_HEAD_DIM), const(1, 2 * D_HEAD_DIM)],
        out_specs=tok(D_MODEL),
        out_shape=jax.ShapeDtypeStruct(x.shape, F32),
        compiler_params=pltpu.CompilerParams(
            dimension_semantics=("parallel", "parallel"), vmem_limit_bytes=VMEM_LIMIT),
        name="diff_attn",
    )(q, ktc, ktx, vc, vx, conf, x, ada, wout, lam_p, sg)


def _exact_ranks(s, label):
    work = s
    rank = jnp.full(s.shape, float(PEER_TOPK), F32)
    for r in range(PEER_TOPK):
        m = jnp.max(work, axis=0, keepdims=True)
        first = jnp.min(jnp.where(work == m, label, 1e9), axis=0, keepdims=True)
        sel = label == first
        rank = jnp.where(sel, float(r), rank)
        work = jnp.where(sel, -jnp.inf, work)
    return rank


def _batcher_network(n):
    def merge(lo, hi, r):
        step = 2 * r
        if step < hi - lo:
            yield from merge(lo, hi, step)
            yield from merge(lo + r, hi, step)
            yield from [(i, i + r) for i in range(lo + r, hi - r, step)]
        else:
            yield (lo, lo + r)

    def sort(lo, hi):
        if hi > lo:
            mid = lo + (hi - lo) // 2
            yield from sort(lo, mid)
            yield from sort(mid + 1, hi)
            yield from merge(lo, hi, 1)

    return tuple(sort(0, n - 1))


_SORT16 = _batcher_network(PEER_TOPK)


def _sublane_allreduce(x, op):
    for shift in (4, 2, 1):
        x = op(x, pltpu.roll(x, shift, axis=0))
    return x


def _sorted_top16(rows):
    b = list(rows)
    for i, j in _SORT16:
        b[i], b[j] = jnp.maximum(b[i], b[j]), jnp.minimum(b[i], b[j])
    for shift in (4, 2, 1):
        m = [jnp.maximum(b[i], pltpu.roll(b[PEER_TOPK - 1 - i], shift, axis=0))
             for i in range(PEER_TOPK)]
        d = PEER_TOPK // 2
        while d:
            for i in range(PEER_TOPK):
                if not i & d:
                    m[i], m[i + d] = jnp.maximum(m[i], m[i + d]), jnp.minimum(m[i], m[i + d])
            d //= 2
        b = m
    return b


def _lookup_by_rank(rows, top, values):
    out = []
    for a in rows:
        r = jnp.zeros_like(a) + values[0]
        for k in range(PEER_TOPK):
            r = jnp.where(top[k] > a, values[k + 1], r)
        out.append(r)
    return jnp.concatenate(out, axis=0)


def _tie_flag(rows, top):
    flag = jnp.zeros_like(top[0])
    for k in range(PEER_TOPK - 1):
        flag = jnp.where(top[k] == top[k + 1], 1.0, flag)
    n_ge = None
    for a in rows:
        hit = jnp.where(a >= top[PEER_TOPK - 1], 1.0, 0.0)
        n_ge = hit if n_ge is None else n_ge + hit
    n_ge = _sublane_allreduce(n_ge, jnp.add)
    return jnp.where(n_ge != float(PEER_TOPK), 1.0, flag)


def _dup_bf16(v):
    bits = pltpu.bitcast(v.astype(BF16).astype(F32), jnp.uint32)
    return bits | (bits >> 16)


def _pack_bf16(v):
    n = v.shape[0] // 2
    bits = pltpu.bitcast(v.astype(BF16).astype(F32), jnp.uint32)
    return (bits[:n] >> 16) | bits[n:]


def _peer_query_kernel(x_ref, ada_ref, g2_ref, wq_ref, skh_ref, skl_ref,
                       h2t_ref, r2_ref, a2_ref, n1_ref, b1_ref, qt_ref):
    tb = x_ref.shape[0]
    shift, scale = ada_ref[0, 3:4, :], ada_ref[0, 4:5, :]
    h2 = _norm_mod(x_ref[...], g2_ref[...], shift, scale)
    h2t = h2.T
    hh = h2t.astype(BF16)
    h2t_ref[...] = hh
    qt_ref[...] = _dot(wq_ref[...], hh)

    pos = lax.broadcasted_iota(jnp.int32, (PEER_KEYS, tb), 0)
    label1 = pos.astype(F32)
    label2 = jnp.where(pos < PEER_KEYS // 2, 2 * pos, 2 * pos - (PEER_KEYS - 1)).astype(F32)
    sub = lax.broadcasted_iota(jnp.int32, (SUBLANES, tb), 0)
    subf = sub.astype(F32)
    cand_label = ([subf, subf + 8.0] + [subf + 16.0 * k1 for k1, _ in _STAIR]
                  + [16.0 * (subf + 8.0)])
    cand_valid = [None, None] + [None if nv == SUBLANES else sub < nv for _, nv in _STAIR] + [None]

    def spread(vals):
        out = vals[0]
        for u in range(1, SUBLANES):
            out = jnp.where(sub == u, vals[u], out)
        return out

    def head(hd, carry):
        def scores(p):
            qp = qt_ref[pl.ds(pl.multiple_of((2 * hd + p) * PEER_KEYS, PEER_KEYS), PEER_KEYS), :]
            qh, ql = _split(qp)
            kh, kl = skh_ref[2 * hd + p], skl_ref[2 * hd + p]
            return _dot(kh, qh) + (_dot(kh, ql) + _dot(kl, qh))

        s1, s2 = scores(0), scores(1)
        rows1 = [s1[SUBLANES * r:SUBLANES * (r + 1)] for r in range(PEER_KEYS // SUBLANES)]
        rows2 = [s2[SUBLANES * r:SUBLANES * (r + 1)] for r in range(PEER_KEYS // SUBLANES)]
        top1, top2 = _sorted_top16(rows1), _sorted_top16(rows2)

        t2a, t2b, t1b = spread(top2[:SUBLANES]), spread(top2[SUBLANES:]), spread(top1[SUBLANES:])
        cand = ([t2a + top1[0], t2b + top1[0]] + [t2a + top1[k1] for k1, _ in _STAIR]
                + [t1b + top2[0]])
        cand = [c if v is None else jnp.where(v, c, -jnp.inf) for c, v in zip(cand, cand_valid)]
        pad = [jnp.full((SUBLANES, tb), -jnp.inf, F32)] * (PEER_TOPK - len(cand))
        best = _sorted_top16(cand + pad)
        z = functools.reduce(jnp.add, [jnp.exp(b - best[0]) for b in best])
        above = [jnp.where(c >= best[PEER_TOPK - 1], 1.0, 0.0) for c in cand]
        n_above = _sublane_allreduce(functools.reduce(jnp.add, above), jnp.add)

        def take_in_index_order():
            left = cand
            for r in range(PEER_TOPK):
                m = _sublane_allreduce(functools.reduce(jnp.maximum, left), jnp.maximum)
                first = functools.reduce(
                    jnp.minimum, [jnp.where(c == m, lab, 1e9) for c, lab in zip(left, cand_label)])
                first = _sublane_allreduce(first, jnp.minimum)
                left = [jnp.where(lab == first, -jnp.inf, c) for c, lab in zip(left, cand_label)]
            return [jnp.where(jnp.logical_and(c0 > -jnp.inf, c == -jnp.inf), 1.0, 0.0)
                    for c0, c in zip(cand, left)]

        taken = lax.cond(jnp.max(jnp.abs(n_above - float(PEER_TOPK))) > 0.0,
                         take_in_index_order, lambda: above)
        counts = [_sublane_allreduce(taken[0] + taken[1], jnp.add)]
        counts += [_sublane_allreduce(t, jnp.add) for t in taken[2:-1]]
        counts += [_sublane_allreduce(jnp.where(sub == u, taken[-1], 0.0), jnp.add)
                   for u in range(SUBLANES)]
        counts.append(0.0)
        ranks = [float(k) for k in range(PEER_TOPK + 1)]

        def fast():
            return _lookup_by_rank(rows2, top2, ranks), _lookup_by_rank(rows1, top1, counts)

        def exact():
            rank1 = _exact_ranks(s1, label1)
            n1 = jnp.zeros_like(s1)
            for k1 in range(PEER_TOPK):
                n1 = jnp.where(rank1 == float(k1), jnp.tile(counts[k1], (PEER_KEYS // SUBLANES, 1)), n1)
            return _exact_ranks(s2, label2), n1

        ties = jnp.max(jnp.maximum(_tie_flag(rows1, top1), _tie_flag(rows2, top2)))
        rank2, n1 = lax.cond(ties > 0.0, exact, fast)
        tile = lambda v: jnp.tile(v, (PEER_KEYS // SUBLANES, 1))
        r2_ref[hd] = _pack_bf16(rank2)
        vals = _pack_bf16(jnp.exp(s2 - tile(top2[0])))
        a2_ref[hd] = jnp.concatenate(
            [vals[:, (c ^ 1) * LANES:((c ^ 1) + 1) * LANES] for c in range(tb // LANES)], axis=1)
        n1_ref[hd] = _dup_bf16(n1)
        wrows = [pltpu.roll(jnp.exp(a - top1[0]) * (1.0 / z), B1_ROW_SHIFT, axis=0) for a in rows1]
        b1_ref[hd] = _dup_bf16(jnp.concatenate(wrows, axis=0))
        return carry

    lax.fori_loop(0, PEER_HEADS, head, 0)


def _peer_query(x, ada, g2, wqt, skh, skl, *, tb, rows_per_ada):
    T = x.shape[0]
    qd = wqt.shape[0]
    const = lambda *shape: pl.BlockSpec(shape, lambda j: (0,) * len(shape))
    tab = lambda rows: pl.BlockSpec((PEER_HEADS, rows, tb), lambda j: (0, 0, j))
    tab_shape = lambda rows: jax.ShapeDtypeStruct((PEER_HEADS, rows, T), jnp.uint32)
    half = PEER_KEYS // 2
    return pl.pallas_call(
        _peer_query_kernel,
        grid=(T // tb,),
        in_specs=[pl.BlockSpec((tb, D_MODEL), lambda j: (j, 0)),
                  pl.BlockSpec((1, N_ADA, D_MODEL), lambda j: ((j * tb) // rows_per_ada, 0, 0)),
                  const(1, D_MODEL), const(qd, D_MODEL),
                  const(2 * PEER_HEADS, PEER_KEYS, PEER_KEYS),
                  const(2 * PEER_HEADS, PEER_KEYS, PEER_KEYS)],
        out_specs=[pl.BlockSpec((D_MODEL, tb), lambda j: (0, j)), tab(half), tab(half),
                   tab(PEER_KEYS), tab(PEER_KEYS)],
        out_shape=[jax.ShapeDtypeStruct((D_MODEL, T), BF16), tab_shape(half), tab_shape(half),
                   tab_shape(PEER_KEYS), tab_shape(PEER_KEYS)],
        scratch_shapes=[pltpu.VMEM((qd, tb), F32)],
        compiler_params=pltpu.CompilerParams(
            dimension_semantics=("parallel",), vmem_limit_bytes=VMEM_LIMIT),
        name="peer_query",
    )(x, ada, g2, wqt, skh, skl)


def _peer_dense_kernel(h2t_ref, u_ref, vt_ref, r2_ref, a2_ref, n1_ref, b1_ref, x_ref, ada_ref,
                       o_ref, acc_ref, p_ref, act_ref, *, n_et):
    i = pl.program_id(1)
    te, tb = p_ref.shape

    @pl.when(i == 0)
    def _():
        acc_ref[...] = jnp.zeros_like(acc_ref)
        p_ref[...] = jnp.zeros_like(p_ref)

    def packed_row(ref, hd, j, cs):
        word = jnp.broadcast_to(ref[hd, j:j + 1, cs], (SUBLANES, LANES))
        return pltpu.bitcast(word, BF16)

    def contract_previous():
        acc_ref[...] += _dot(vt_ref[...], p_ref[...])

    @pl.when(i < n_et)
    def _():
        act_ref[...] = _gelu(_dot(u_ref[...], h2t_ref[...])).astype(BF16)
        contract_previous()
        slabs = PEER_KEYS // BF16_ROWS
        for j in range(te // PEER_KEYS):
            for c0 in range(0, tb // LANES, GATE_CHUNKS):
                for k0 in range(0, slabs, GATE_SLABS):
                    g = [[None] * GATE_SLABS for _ in range(GATE_CHUNKS)]
                    for hd in range(PEER_HEADS):
                        for cc in range(GATE_CHUNKS):
                            cs = slice((c0 + cc) * LANES, (c0 + cc + 1) * LANES)
                            cs_val = slice(((c0 + cc) ^ 1) * LANES, (((c0 + cc) ^ 1) + 1) * LANES)
                            cnt = packed_row(n1_ref, hd, j, cs)
                            wgt = packed_row(b1_ref, hd, (j + B1_ROW_SHIFT) % SUBLANES, cs)
                            for kk in range(GATE_SLABS):
                                words = slice((k0 + kk) * SUBLANES, (k0 + kk + 1) * SUBLANES)
                                rank = pltpu.bitcast(r2_ref[hd, words, cs], BF16)
                                val = pltpu.bitcast(a2_ref[hd, words, cs_val], BF16)
                                term = jnp.where(rank < cnt, val, jnp.zeros_like(val)) * wgt
                                g[cc][kk] = term if hd == 0 else g[cc][kk] + term
                    for cc in range(GATE_CHUNKS):
                        cs = slice((c0 + cc) * LANES, (c0 + cc + 1) * LANES)
                        for kk in range(GATE_SLABS):
                            r0 = j * PEER_KEYS + (k0 + kk) * BF16_ROWS
                            p_ref[r0:r0 + BF16_ROWS, cs] = g[cc][kk] * act_ref[r0:r0 + BF16_ROWS, cs]

    @pl.when(i == n_et)
    def _():
        contract_previous()
        o_ref[...] = x_ref[...] + ada_ref[0, 5:6, :] * acc_ref[...].T


def _peer_dense(x, h2t, tabs, u, vt, ada, *, tb, te, rows_per_ada):
    T = x.shape[0]
    n_exp = u.shape[0]
    n_et = n_exp // te
    n1 = te // PEER_KEYS
    assert T % tb == 0 and rows_per_ada % tb == 0 and n_exp % te == 0 and n1 == SUBLANES
    r2, a2, n1w, b1w = tabs
    tab = pl.BlockSpec((PEER_HEADS, PEER_KEYS // 2, tb), lambda j, i: (0, 0, j))
    tab1 = pl.BlockSpec((PEER_HEADS, n1, tb), lambda j, i: (0, jnp.minimum(i, n_et - 1), j))
    return pl.pallas_call(
        functools.partial(_peer_dense_kernel, n_et=n_et),
        grid=(T // tb, n_et + 1),
        in_specs=[pl.BlockSpec((D_MODEL, tb), lambda j, i: (0, j)),
                  pl.BlockSpec((te, D_MODEL), lambda j, i: (jnp.minimum(i, n_et - 1), 0)),
                  pl.BlockSpec((D_MODEL, te), lambda j, i: (0, jnp.maximum(i - 1, 0))),
                  tab, tab, tab1, tab1,
                  pl.BlockSpec((tb, D_MODEL), lambda j, i: (j, 0)),
                  pl.BlockSpec((1, N_ADA, D_MODEL), lambda j, i: ((j * tb) // rows_per_ada, 0, 0))],
        out_specs=pl.BlockSpec((tb, D_MODEL), lambda j, i: (j, 0)),
        out_shape=jax.ShapeDtypeStruct(x.shape, F32),
        scratch_shapes=[pltpu.VMEM((D_MODEL, tb), F32), pltpu.VMEM((te, tb), BF16),
                        pltpu.VMEM((te, tb), BF16)],
        compiler_params=pltpu.CompilerParams(
            dimension_semantics=("parallel", "arbitrary"), vmem_limit_bytes=VMEM_LIMIT),
        name="peer_dense",
    )(h2t, u, vt, r2, a2, n1w, b1w, x, ada)


def _prep_tables_kernel(u_ref, v_ref, ub_ref, vt_ref):
    ub_ref[...] = u_ref[0].astype(BF16)
    vt_ref[...] = v_ref[0].T.astype(BF16)


def _prep_tables(peer_u, peer_v, layer, *, te):
    _, n_exp, d = peer_u.shape
    rows = pl.BlockSpec((1, te, d), lambda i: (layer, i, 0))
    return pl.pallas_call(
        _prep_tables_kernel,
        grid=(n_exp // te,),
        in_specs=[rows, rows],
        out_specs=[pl.BlockSpec((te, d), lambda i: (i, 0)), pl.BlockSpec((d, te), lambda i: (0, i))],
        out_shape=[jax.ShapeDtypeStruct((n_exp, d), BF16), jax.ShapeDtypeStruct((d, n_exp), BF16)],
        compiler_params=pltpu.CompilerParams(
            dimension_semantics=("parallel",), vmem_limit_bytes=VMEM_LIMIT),
        name="prep_tables",
    )(peer_u, peer_v)


def _peer(x, ada, g2, wq, subkeys, u_tab, vt_tab, *, rows_per_ada):
    wqt = wq.T.astype(BF16)
    sk = jnp.stack([subkeys[:, 0], jnp.concatenate([subkeys[:, 1, 0::2], subkeys[:, 1, 1::2]], axis=1)],
                   axis=1)
    skh, skl = _split(sk.reshape(2 * PEER_HEADS, PEER_KEYS, -1))
    h2t, *tabs = _peer_query(x, ada, g2, wqt, skh, skl, tb=256, rows_per_ada=rows_per_ada)
    return _peer_dense(x, h2t, tabs, u_tab, vt_tab, ada, tb=1024, te=1024, rows_per_ada=rows_per_ada)


def _rope_tables(L):
    pos = jnp.arange(L, dtype=jnp.int32)
    r = (pos // GRID_W).astype(F32)
    col = (pos % GRID_W).astype(F32)
    n_freq = D_HEAD_DIM // 4
    inv = ROPE_THETA ** (-jnp.arange(n_freq, dtype=F32) / n_freq)
    ang = jnp.concatenate([r[:, None] * inv, col[:, None] * inv], axis=-1)
    cos = jnp.repeat(jnp.cos(ang), 2, axis=-1)
    sin = jnp.repeat(jnp.sin(ang), 2, axis=-1) * jnp.tile(jnp.array([-1.0, 1.0], F32), D_HEAD_DIM // 2)
    reps = MIX_W // D_HEAD_DIM
    return jnp.tile(cos, (1, reps)), jnp.tile(sin, (1, reps))


def kernel(x, c, ctx, c_ctx, w_ada, b_ada, norm1_g, norm2_g, w_in, w_out, a_conv_w, b_vnorm_g,
           b_spatial_w, b_spatial_b, c_conv_w, c_conv_b, c_norm_g, d_qk_norm_g, d_lambda,
           d_subln_g, peer_wq, peer_subkeys, peer_u, peer_v):
    bsz, L, _ = x.shape
    lc = ctx.shape[1]
    depth = w_ada.shape[0]
    assert depth == 2 and bsz + 1 <= ADA_ROWS

    cc = jnp.zeros((ADA_ROWS, D_MODEL), F32).at[:bsz].set(c).at[bsz].set(c_ctx)
    ada = _ada(cc, w_ada, b_ada).reshape(depth, ADA_ROWS, N_ADA, D_MODEL)
    row = lambda v: v.reshape(1, -1)

    ada_x, ada_c = ada[0, :bsz], ada[0, bsz:bsz + 1]
    bsb = jnp.repeat(b_spatial_b[0].T, MIX_W // B_GROUPS, axis=1)
    even = functools.partial(
        _even_mixer, g1=row(norm1_g[0]), win=w_in[0].astype(BF16), wout=w_out[0].astype(BF16),
        cw=a_conv_w[0], vg=row(b_vnorm_g[0]), ws=b_spatial_w[0].astype(BF16), bsb=bsb)
    u_tab, vt_tab = _prep_tables(peer_u, peer_v, 0, te=512)
    peer0 = functools.partial(_peer, g2=row(norm2_g[0]), wq=peer_wq[0], subkeys=peer_subkeys[0],
                              u_tab=u_tab, vt_tab=vt_tab)
    x = even(x, ada_x, tm=512)
    x = peer0(x.reshape(bsz * L, D_MODEL), ada_x, rows_per_ada=L).reshape(bsz, L, D_MODEL)
    ctx = even(ctx, jnp.broadcast_to(ada_c, (bsz, N_ADA, D_MODEL)), tm=lc)
    ctx = peer0(ctx.reshape(bsz * lc, D_MODEL), ada_c, rows_per_ada=bsz * lc).reshape(bsz, lc, D_MODEL)

    ada_x, ada_c = ada[1, :bsz], ada[1, bsz:bsz + 1]
    lam_init = 0.8 - 0.6 * math.exp(-0.3 * 1)
    win = w_in[1].astype(BF16)
    cos_t, sin_t = _rope_tables(L)
    seg_id = jnp.arange(MIX_W, dtype=jnp.int32) // D_HEAD_DIM
    seg = (seg_id[:, None] == seg_id[None, :]).astype(BF16)
    reps = MIX_W // D_HEAD_DIM
    qg, kg = row(jnp.tile(d_qk_norm_g[0, 0], reps)), row(jnp.tile(d_qk_norm_g[0, 1], reps))
    conf, q, ktx, vx = _odd_in(x, ada_x, row(norm1_g[1]), win, c_conv_w[0], row(c_conv_b[0]),
                               row(c_norm_g[0]), qg, kg, cos_t, sin_t, seg, tm=512)
    ktc, vc = _ctx_kv(ctx, jnp.broadcast_to(ada_c, (bsz, N_ADA, D_MODEL)), row(norm1_g[1]),
                      win[:, 3 * MIX_W:], kg, seg)
    x = _attn(q, ktc, ktx, vc, vx, conf, x, ada_x, w_out[1].astype(BF16), d_lambda[0],
              row(d_subln_g[0]), lam_init=lam_init, tq=512)
    u_tab, vt_tab = _prep_tables(peer_u, peer_v, 1, te=512)
    x = _peer(x.reshape(bsz * L, D_MODEL), ada_x, row(norm2_g[1]), peer_wq[1], peer_subkeys[1],
              u_tab, vt_tab, rows_per_ada=L)
    return x.reshape(bsz, L, D_MODEL)
```

```python
import functools
import math

import jax
import jax.numpy as jnp
from jax import lax
from jax.experimental import pallas as pl
from jax.experimental.pallas import tpu as pltpu

F32 = jnp.float32
BF16 = jnp.bfloat16

D_MODEL = 1024
MIX_W = 512
IN_W = 5 * MIX_W
GRID_W = 64
CHUNK = 128
B_GROUPS = 8
A_CONV = 3
C_CONV = 31
D_HEADS = 4
D_HEAD_DIM = 64
ROPE_THETA = 10000.0
N_ADA = 6
PEER_HEADS = 8
PEER_KEYS = 128
PEER_TOPK = 16
EPS = 1e-6

SUBLANES, LANES = 8, 128
BF16_ROWS = 2 * SUBLANES
GATE_CHUNKS = 4
GATE_SLABS = 8
B1_ROW_SHIFT = 4

ADA_ROWS = 16
CONV_HALO = 16
VMEM_LIMIT = 56 * 1024 * 1024

_STAIR = ((1, 8), (2, 5), (3, 4), (4, 3), (5, 2), (6, 2), (7, 2))


LOG2E = math.log2(math.e)
_GELU_C0 = -2.0 * math.sqrt(2.0 / math.pi) * LOG2E
_GELU_C1 = 0.044715 * _GELU_C0


def _gelu(x):
    return x / (1.0 + jnp.exp2(x * (_GELU_C0 + _GELU_C1 * (x * x))))


def _sigmoid(x):
    return 1.0 / (1.0 + jnp.exp(-x))


def _rms(x, g):
    return x * lax.rsqrt(jnp.mean(x * x, axis=-1, keepdims=True) + EPS) * g


def _norm_mod(x, g, shift, scale):
    return _rms(x, g) * (1.0 + scale) + shift


def _split(x):
    hi = x.astype(BF16)
    lo = (x - hi.astype(F32)).astype(BF16)
    return hi, lo


def _dot(a, b):
    return jnp.dot(a, b, preferred_element_type=F32)


def _ada_kernel(c_ref, w_ref, b_ref, o_ref):
    cc = c_ref[...]
    s = cc * _sigmoid(cc)
    o_ref[0] = jnp.dot(s, w_ref[0], precision=lax.Precision.HIGHEST,
                       preferred_element_type=F32) + b_ref[0]


def _ada(cc, w_ada, b_ada):
    depth, _, n = w_ada.shape
    tn = 1536
    return pl.pallas_call(
        _ada_kernel,
        grid=(depth, n // tn),
        in_specs=[pl.BlockSpec((ADA_ROWS, D_MODEL), lambda l, j: (0, 0)),
                  pl.BlockSpec((1, D_MODEL, tn), lambda l, j: (l, 0, j)),
                  pl.BlockSpec((1, 1, tn), lambda l, j: (l, 0, j))],
        out_specs=pl.BlockSpec((1, ADA_ROWS, tn), lambda l, j: (l, 0, j)),
        out_shape=jax.ShapeDtypeStruct((depth, ADA_ROWS, n), F32),
        compiler_params=pltpu.CompilerParams(vmem_limit_bytes=VMEM_LIMIT),
        name="ada",
    )(cc, w_ada, b_ada.reshape(depth, 1, n))


def _even_kernel(x_ref, xp_ref, xn_ref, ada_ref, g1_ref, win_ref, wout_ref, cw_ref, vg_ref,
                 ws_ref, bsb_ref, o_ref, *, n_tiles):
    t = pl.program_id(1)
    tm = x_ref.shape[1]
    shift, scale, gate = ada_ref[0, 0:1, :], ada_ref[0, 1:2, :], ada_ref[0, 2:3, :]
    g1 = g1_ref[...]

    x = x_ref[0]
    xs = jnp.concatenate([xp_ref[0], x, xn_ref[0]], axis=0)
    h = _norm_mod(xs, g1, shift, scale).astype(BF16)
    proj_all = _dot(h, win_ref[...])
    proj = proj_all[SUBLANES:SUBLANES + tm]
    a_h, a_c, a_b = proj[:, 0:MIX_W], proj[:, MIX_W:2 * MIX_W], proj[:, 2 * MIX_W:3 * MIX_W]
    b_u, b_v = proj[:, 3 * MIX_W:4 * MIX_W], proj[:, 4 * MIX_W:5 * MIX_W]

    z = a_c * a_h

    def edge_z(r):
        return proj_all[r:r + 1, 0:MIX_W] * proj_all[r:r + 1, MIX_W:2 * MIX_W]

    zp = jnp.where(t > 0, edge_z(SUBLANES - 1), 0.0)
    zn = jnp.where(t < n_tiles - 1, edge_z(SUBLANES + tm), 0.0)
    row = lax.broadcasted_iota(jnp.int32, (tm, MIX_W), 0)
    z_m1 = jnp.where(row == 0, zp, pltpu.roll(z, 1, axis=0))
    z_p1 = jnp.where(row == tm - 1, zn, pltpu.roll(z, tm - 1, axis=0))
    ya = a_b * (cw_ref[0:1, :] * z_m1 + cw_ref[1:2, :] * z + cw_ref[2:3, :] * z_p1)

    u = _gelu(b_u)
    v = _rms(_gelu(b_v), vg_ref[...]).astype(BF16)
    lane = lax.broadcasted_iota(jnp.int32, (CHUNK, CHUNK), 1)
    group_w = MIX_W // B_GROUPS
    rows = []
    for n in range(tm // CHUNK):
        cols = []
        for j in range(MIX_W // CHUNK):
            vv = v[n * CHUNK:(n + 1) * CHUNK, j * CHUNK:(j + 1) * CHUNK]
            s0 = _dot(ws_ref[2 * j], vv)
            s1 = _dot(ws_ref[2 * j + 1], vv)
            cols.append(jnp.where(lane < group_w, s0, s1))
        rows.append(jnp.concatenate(cols, axis=1) + bsb_ref[...])
    s = jnp.concatenate(rows, axis=0)
    yb = u * s

    y = jnp.concatenate([ya, yb], axis=1).astype(BF16)
    o_ref[0] = x + gate * _dot(y, wout_ref[...])


def _even_mixer(x, ada, g1, win, wout, cw, vg, ws, bsb, *, tm):
    bsz, L, _ = x.shape
    n_tiles = L // tm
    hb = tm // 8
    const = lambda *shape: pl.BlockSpec(shape, lambda b, t: (0,) * len(shape))
    return pl.pallas_call(
        functools.partial(_even_kernel, n_tiles=n_tiles),
        grid=(bsz, n_tiles),
        in_specs=[pl.BlockSpec((1, tm, D_MODEL), lambda b, t: (b, t, 0)),
                  pl.BlockSpec((1, 8, D_MODEL), lambda b, t: (b, jnp.maximum(t * hb - 1, 0), 0)),
                  pl.BlockSpec((1, 8, D_MODEL),
                               lambda b, t: (b, jnp.minimum((t + 1) * hb, L // 8 - 1), 0)),
                  pl.BlockSpec((1, N_ADA, D_MODEL), lambda b, t: (b, 0, 0)),
                  const(1, D_MODEL), const(D_MODEL, IN_W), const(2 * MIX_W, D_MODEL),
                  const(A_CONV, MIX_W), const(1, MIX_W), const(B_GROUPS, CHUNK, CHUNK),
                  const(CHUNK, MIX_W)],
        out_specs=pl.BlockSpec((1, tm, D_MODEL), lambda b, t: (b, t, 0)),
        out_shape=jax.ShapeDtypeStruct(x.shape, F32),
        compiler_params=pltpu.CompilerParams(
            dimension_semantics=("parallel", "parallel"), vmem_limit_bytes=VMEM_LIMIT),
        name="even_mixer",
    )(x, x, x, ada, g1, win, wout, cw, vg, ws, bsb)


def _head_norm(t, gain, seg_ref):
    hi, lo = _split(t * t)
    ss = _dot(hi, seg_ref[...]) + _dot(lo, seg_ref[...])
    return t * lax.rsqrt(ss * (1.0 / D_HEAD_DIM) + EPS) * gain


def _odd_in_kernel(x_ref, xp_ref, xn_ref, ada_ref, g1_ref, win_ref, cw_ref, cb_ref, cg_ref,
                   qg_ref, kg_ref, cos_ref, sin_ref, seg_ref,
                   conf_ref, q_ref, kt_ref, v_ref, *, n_tiles):
    t = pl.program_id(1)
    tm = x_ref.shape[1]
    rows_all = tm + 2 * CONV_HALO
    shift, scale = ada_ref[0, 0:1, :], ada_ref[0, 1:2, :]
    g1 = g1_ref[...]

    xs = jnp.concatenate([xp_ref[0], x_ref[0], xn_ref[0]], axis=0)
    h = _norm_mod(xs, g1, shift, scale).astype(BF16)
    proj_all = _dot(h, win_ref[...])
    proj = proj_all[CONV_HALO:CONV_HALO + tm]

    glu = proj_all[:, 0:MIX_W] * _sigmoid(proj_all[:, MIX_W:2 * MIX_W])
    row = lax.broadcasted_iota(jnp.int32, (rows_all, MIX_W), 0)
    outside = jnp.logical_or(jnp.logical_and(row < CONV_HALO, t == 0),
                             jnp.logical_and(row >= CONV_HALO + tm, t == n_tiles - 1))
    glu = jnp.where(outside, 0.0, glu)
    first = CONV_HALO - C_CONV // 2
    acc = jnp.zeros((tm, MIX_W), F32) + cb_ref[...]
    for b in range(SUBLANES):
        shifted = glu if b == 0 else pltpu.roll(glu, rows_all - b, axis=0)
        for a in range(rows_all // SUBLANES):
            k = SUBLANES * a + b - first
            if 0 <= k < C_CONV:
                acc = acc + cw_ref[k:k + 1, :] * shifted[SUBLANES * a:SUBLANES * a + tm]
    yn = _rms(acc, cg_ref[...])
    conf_ref[0] = (yn * _sigmoid(yn)).astype(BF16)

    lane = lax.broadcasted_iota(jnp.int32, (tm, MIX_W), 1)
    even_lane = (lane % 2) == 0

    def rope(tn):
        partner = jnp.where(even_lane, pltpu.roll(tn, MIX_W - 1, axis=1), pltpu.roll(tn, 1, axis=1))
        return tn * cos_ref[...] + partner * sin_ref[...]

    q = rope(_head_norm(proj[:, 2 * MIX_W:3 * MIX_W], qg_ref[...], seg_ref))
    k = rope(_head_norm(proj[:, 3 * MIX_W:4 * MIX_W], kg_ref[...], seg_ref))
    q_ref[0] = (q * (D_HEAD_DIM ** -0.5 * LOG2E)).astype(BF16)
    kt_ref[0] = k.T.astype(BF16)
    v_ref[0] = proj[:, 4 * MIX_W:5 * MIX_W].astype(BF16)


def _odd_in(x, ada, g1, win, cw, cb, cg, qg, kg, cos_t, sin_t, seg, *, tm):
    bsz, L, _ = x.shape
    n_tiles = L // tm
    hb = tm // CONV_HALO
    const = lambda *shape: pl.BlockSpec(shape, lambda b, t: (0,) * len(shape))
    tok = lambda w: pl.BlockSpec((1, tm, w), lambda b, t: (b, t, 0))
    return pl.pallas_call(
        functools.partial(_odd_in_kernel, n_tiles=n_tiles),
        grid=(bsz, n_tiles),
        in_specs=[tok(D_MODEL),
                  pl.BlockSpec((1, CONV_HALO, D_MODEL),
                               lambda b, t: (b, jnp.maximum(t * hb - 1, 0), 0)),
                  pl.BlockSpec((1, CONV_HALO, D_MODEL),
                               lambda b, t: (b, jnp.minimum((t + 1) * hb, L // CONV_HALO - 1), 0)),
                  pl.BlockSpec((1, N_ADA, D_MODEL), lambda b, t: (b, 0, 0)),
                  const(1, D_MODEL), const(D_MODEL, IN_W), const(C_CONV, MIX_W),
                  const(1, MIX_W), const(1, MIX_W), const(1, MIX_W), const(1, MIX_W),
                  pl.BlockSpec((tm, MIX_W), lambda b, t: (t, 0)),
                  pl.BlockSpec((tm, MIX_W), lambda b, t: (t, 0)),
                  const(MIX_W, MIX_W)],
        out_specs=[tok(MIX_W), tok(MIX_W),
                   pl.BlockSpec((1, MIX_W, tm), lambda b, t: (b, 0, t)),
                   tok(MIX_W)],
        out_shape=[jax.ShapeDtypeStruct((bsz, L, MIX_W), BF16),
                   jax.ShapeDtypeStruct((bsz, L, MIX_W), BF16),
                   jax.ShapeDtypeStruct((bsz, MIX_W, L), BF16),
                   jax.ShapeDtypeStruct((bsz, L, MIX_W), BF16)],
        compiler_params=pltpu.CompilerParams(
            dimension_semantics=("parallel", "parallel"), vmem_limit_bytes=VMEM_LIMIT),
        name="odd_in",
    )(x, x, x, ada, g1, win, cw, cb, cg, qg, kg, cos_t, sin_t, seg)


def _ctx_kv_kernel(x_ref, ada_ref, g1_ref, wkv_ref, kg_ref, seg_ref, kt_ref, v_ref):
    shift, scale = ada_ref[0, 0:1, :], ada_ref[0, 1:2, :]
    h = _norm_mod(x_ref[0], g1_ref[...], shift, scale).astype(BF16)
    proj = _dot(h, wkv_ref[...])
    k = _head_norm(proj[:, 0:MIX_W], kg_ref[...], seg_ref)
    kt_ref[0] = k.T.astype(BF16)
    v_ref[0] = proj[:, MIX_W:2 * MIX_W].astype(BF16)


def _ctx_kv(ctx, ada, g1, wkv, kg, seg):
    bsz, L, _ = ctx.shape
    const = lambda *shape: pl.BlockSpec(shape, lambda b: (0,) * len(shape))
    return pl.pallas_call(
        _ctx_kv_kernel,
        grid=(bsz,),
        in_specs=[pl.BlockSpec((1, L, D_MODEL), lambda b: (b, 0, 0)),
                  pl.BlockSpec((1, N_ADA, D_MODEL), lambda b: (b, 0, 0)),
                  const(1, D_MODEL), const(D_MODEL, 2 * MIX_W), const(1, MIX_W),
                  const(MIX_W, MIX_W)],
        out_specs=[pl.BlockSpec((1, MIX_W, L), lambda b: (b, 0, 0)),
                   pl.BlockSpec((1, L, MIX_W), lambda b: (b, 0, 0))],
        out_shape=[jax.ShapeDtypeStruct((bsz, MIX_W, L), BF16),
                   jax.ShapeDtypeStruct((bsz, L, MIX_W), BF16)],
        compiler_params=pltpu.CompilerParams(
            dimension_semantics=("parallel",), vmem_limit_bytes=VMEM_LIMIT),
        name="ctx_kv",
    )(ctx, ada, g1, wkv, kg, seg)


def _attn_kernel(q_ref, ktc_ref, ktx_ref, vc_ref, vx_ref, conf_ref, x_ref, ada_ref, wout_ref,
                 lam_ref, sg_ref, o_ref, *, lam_init):
    tq = q_ref.shape[1]
    vw = 2 * D_HEAD_DIM
    lam_p = lam_ref[...]
    lam = (jnp.exp(jnp.sum(lam_p[0:1] * lam_p[1:2], axis=-1, keepdims=True))
           - jnp.exp(jnp.sum(lam_p[2:3] * lam_p[3:4], axis=-1, keepdims=True)) + lam_init)
    first_map = lax.broadcasted_iota(jnp.int32, (tq, vw), 1) < D_HEAD_DIM
    outs = []
    for hd in range(D_HEADS):
        sl = slice(hd * vw, (hd + 1) * vw)
        qh = q_ref[0, :, sl]
        maps = []
        for m in range(2):
            qm = jnp.where(first_map if m == 0 else jnp.logical_not(first_map), qh, jnp.zeros_like(qh))
            sc = _dot(qm, ktc_ref[0, sl, :])
            sx = _dot(qm, ktx_ref[0, sl, :])
            mx = jnp.maximum(jnp.max(sc, axis=-1, keepdims=True), jnp.max(sx, axis=-1, keepdims=True))
            pc, px = jnp.exp2(sc - mx), jnp.exp2(sx - mx)
            inv = 1.0 / (jnp.sum(pc, axis=-1, keepdims=True) + jnp.sum(px, axis=-1, keepdims=True))
            pv = _dot(pc.astype(BF16), vc_ref[0, :, sl]) + _dot(px.astype(BF16), vx_ref[0, :, sl])
            maps.append(pv * inv)
        o = maps[0] - lam * maps[1]
        outs.append(_rms(o, sg_ref[...]) * (1.0 - lam_init))
    y = jnp.concatenate([conf_ref[0]] + [o.astype(BF16) for o in outs], axis=1)
    o_ref[0] = x_ref[0] + ada_ref[0, 2:3, :] * _dot(y, wout_ref[...])


def _attn(q, ktc, ktx, vc, vx, conf, x, ada, wout, lam_p, sg, *, lam_init, tq):
    bsz, L, _ = x.shape
    lc = vc.shape[1]
    const = lambda *shape: pl.BlockSpec(shape, lambda b, t: (0,) * len(shape))
    tok = lambda w: pl.BlockSpec((1, tq, w), lambda b, t: (b, t, 0))
    return pl.pallas_call(
        functools.partial(_attn_kernel, lam_init=lam_init),
        grid=(bsz, L // tq),
        in_specs=[tok(MIX_W),
                  pl.BlockSpec((1, MIX_W, lc), lambda b, t: (b, 0, 0)),
                  pl.BlockSpec((1, MIX_W, L), lambda b, t: (b, 0, 0)),
                  pl.BlockSpec((1, lc, MIX_W), lambda b, t: (b, 0, 0)),
                  pl.BlockSpec((1, L, MIX_W), lambda b, t: (b, 0, 0)),
                  tok(MIX_W), tok(D_MODEL),
                  pl.BlockSpec((1, N_ADA, D_MODEL), lambda b, t: (b, 0, 0)),
                  const(2 * MIX_W, D_MODEL), const(4, D_HEAD_DIM), const(1, 2 * D_HEAD_DIM)],
        out_specs=tok(D_MODEL),
        out_shape=jax.ShapeDtypeStruct(x.shape, F32),
        compiler_params=pltpu.CompilerParams(
            dimension_semantics=("parallel", "parallel"), vmem_limit_bytes=VMEM_LIMIT),
        name="diff_attn",
    )(q, ktc, ktx, vc, vx, conf, x, ada, wout, lam_p, sg)


def _exact_ranks(s, label):
    work = s
    rank = jnp.full(s.shape, float(PEER_TOPK), F32)
    for r in range(PEER_TOPK):
        m = jnp.max(work, axis=0, keepdims=True)
        first = jnp.min(jnp.where(work == m, label, 1e9), axis=0, keepdims=True)
        sel = label == first
        rank = jnp.where(sel, float(r), rank)
        work = jnp.where(sel, -jnp.inf, work)
    return rank


def _batcher_network(n):
    def merge(lo, hi, r):
        step = 2 * r
        if step < hi - lo:
            yield from merge(lo, hi, step)
            yield from merge(lo + r, hi, step)
            yield from [(i, i + r) for i in range(lo + r, hi - r, step)]
        else:
            yield (lo, lo + r)

    def sort(lo, hi):
        if hi > lo:
            mid = lo + (hi - lo) // 2
            yield from sort(lo, mid)
            yield from sort(mid + 1, hi)
            yield from merge(lo, hi, 1)

    return tuple(sort(0, n - 1))


_SORT16 = _batcher_network(PEER_TOPK)


def _sublane_allreduce(x, op):
    for shift in (4, 2, 1):
        x = op(x, pltpu.roll(x, shift, axis=0))
    return x


def _sorted_top16(rows):
    b = list(rows)
    for i, j in _SORT16:
        b[i], b[j] = jnp.maximum(b[i], b[j]), jnp.minimum(b[i], b[j])
    for shift in (4, 2, 1):
        m = [jnp.maximum(b[i], pltpu.roll(b[PEER_TOPK - 1 - i], shift, axis=0))
             for i in range(PEER_TOPK)]
        d = PEER_TOPK // 2
        while d:
            for i in range(PEER_TOPK):
                if not i & d:
                    m[i], m[i + d] = jnp.maximum(m[i], m[i + d]), jnp.minimum(m[i], m[i + d])
            d //= 2
        b = m
    return b


def _lookup_by_rank(rows, top, values):
    out = []
    for a in rows:
        r = jnp.zeros_like(a) + values[0]
        for k in range(PEER_TOPK):
            r = jnp.where(top[k] > a, values[k + 1], r)
        out.append(r)
    return jnp.concatenate(out, axis=0)


def _tie_flag(rows, top):
    flag = jnp.zeros_like(top[0])
    for k in range(PEER_TOPK - 1):
        flag = jnp.where(top[k] == top[k + 1], 1.0, flag)
    n_ge = None
    for a in rows:
        hit = jnp.where(a >= top[PEER_TOPK - 1], 1.0, 0.0)
        n_ge = hit if n_ge is None else n_ge + hit
    n_ge = _sublane_allreduce(n_ge, jnp.add)
    return jnp.where(n_ge != float(PEER_TOPK), 1.0, flag)


def _dup_bf16(v):
    bits = pltpu.bitcast(v.astype(BF16).astype(F32), jnp.uint32)
    return bits | (bits >> 16)


def _pack_bf16(v):
    n = v.shape[0] // 2
    bits = pltpu.bitcast(v.astype(BF16).astype(F32), jnp.uint32)
    return (bits[:n] >> 16) | bits[n:]


def _peer_query_kernel(x_ref, ada_ref, g2_ref, wq_ref, skh_ref, skl_ref,
                       h2t_ref, r2_ref, a2_ref, n1_ref, b1_ref, qt_ref):
    tb = x_ref.shape[0]
    shift, scale = ada_ref[0, 3:4, :], ada_ref[0, 4:5, :]
    h2 = _norm_mod(x_ref[...], g2_ref[...], shift, scale)
    h2t = h2.T
    hh = h2t.astype(BF16)
    h2t_ref[...] = hh
    qt_ref[...] = _dot(wq_ref[...], hh)

    pos = lax.broadcasted_iota(jnp.int32, (PEER_KEYS, tb), 0)
    label1 = pos.astype(F32)
    label2 = jnp.where(pos < PEER_KEYS // 2, 2 * pos, 2 * pos - (PEER_KEYS - 1)).astype(F32)
    sub = lax.broadcasted_iota(jnp.int32, (SUBLANES, tb), 0)
    subf = sub.astype(F32)
    cand_label = ([subf, subf + 8.0] + [subf + 16.0 * k1 for k1, _ in _STAIR]
                  + [16.0 * (subf + 8.0)])
    cand_valid = [None, None] + [None if nv == SUBLANES else sub < nv for _, nv in _STAIR] + [None]

    def spread(vals):
        out = vals[0]
        for u in range(1, SUBLANES):
            out = jnp.where(sub == u, vals[u], out)
        return out

    def head(hd, carry):
        def scores(p):
            qp = qt_ref[pl.ds(pl.multiple_of((2 * hd + p) * PEER_KEYS, PEER_KEYS), PEER_KEYS), :]
            qh, ql = _split(qp)
            kh, kl = skh_ref[2 * hd + p], skl_ref[2 * hd + p]
            return _dot(kh, qh) + (_dot(kh, ql) + _dot(kl, qh))

        s1, s2 = scores(0), scores(1)
        rows1 = [s1[SUBLANES * r:SUBLANES * (r + 1)] for r in range(PEER_KEYS // SUBLANES)]
        rows2 = [s2[SUBLANES * r:SUBLANES * (r + 1)] for r in range(PEER_KEYS // SUBLANES)]
        top1, top2 = _sorted_top16(rows1), _sorted_top16(rows2)

        t2a, t2b, t1b = spread(top2[:SUBLANES]), spread(top2[SUBLANES:]), spread(top1[SUBLANES:])
        cand = ([t2a + top1[0], t2b + top1[0]] + [t2a + top1[k1] for k1, _ in _STAIR]
                + [t1b + top2[0]])
        cand = [c if v is None else jnp.where(v, c, -jnp.inf) for c, v in zip(cand, cand_valid)]
        pad = [jnp.full((SUBLANES, tb), -jnp.inf, F32)] * (PEER_TOPK - len(cand))
        best = _sorted_top16(cand + pad)
        z = functools.reduce(jnp.add, [jnp.exp(b - best[0]) for b in best])
        above = [jnp.where(c >= best[PEER_TOPK - 1], 1.0, 0.0) for c in cand]
        n_above = _sublane_allreduce(functools.reduce(jnp.add, above), jnp.add)

        def take_in_index_order():
            left = cand
            for r in range(PEER_TOPK):
                m = _sublane_allreduce(functools.reduce(jnp.maximum, left), jnp.maximum)
                first = functools.reduce(
                    jnp.minimum, [jnp.where(c == m, lab, 1e9) for c, lab in zip(left, cand_label)])
                first = _sublane_allreduce(first, jnp.minimum)
                left = [jnp.where(lab == first, -jnp.inf, c) for c, lab in zip(left, cand_label)]
            return [jnp.where(jnp.logical_and(c0 > -jnp.inf, c == -jnp.inf), 1.0, 0.0)
                    for c0, c in zip(cand, left)]

        taken = lax.cond(jnp.max(jnp.abs(n_above - float(PEER_TOPK))) > 0.0,
                         take_in_index_order, lambda: above)
        counts = [_sublane_allreduce(taken[0] + taken[1], jnp.add)]
        counts += [_sublane_allreduce(t, jnp.add) for t in taken[2:-1]]
        counts += [_sublane_allreduce(jnp.where(sub == u, taken[-1], 0.0), jnp.add)
                   for u in range(SUBLANES)]
        counts.append(0.0)
        ranks = [float(k) for k in range(PEER_TOPK + 1)]

        def fast():
            return _lookup_by_rank(rows2, top2, ranks), _lookup_by_rank(rows1, top1, counts)

        def exact():
            rank1 = _exact_ranks(s1, label1)
            n1 = jnp.zeros_like(s1)
            for k1 in range(PEER_TOPK):
                n1 = jnp.where(rank1 == float(k1), jnp.tile(counts[k1], (PEER_KEYS // SUBLANES, 1)), n1)
            return _exact_ranks(s2, label2), n1

        ties = jnp.max(jnp.maximum(_tie_flag(rows1, top1), _tie_flag(rows2, top2)))
        rank2, n1 = lax.cond(ties > 0.0, exact, fast)
        tile = lambda v: jnp.tile(v, (PEER_KEYS // SUBLANES, 1))
        r2_ref[hd] = _pack_bf16(rank2)
        a2_ref[hd] = _pack_bf16(jnp.exp(s2 - tile(top2[0])))
        n1_ref[hd] = _dup_bf16(n1)
        wrows = [pltpu.roll(jnp.exp(a - top1[0]) * (1.0 / z), B1_ROW_SHIFT, axis=0) for a in rows1]
        b1_ref[hd] = _dup_bf16(jnp.concatenate(wrows, axis=0))
        return carry

    lax.fori_loop(0, PEER_HEADS, head, 0)


def _peer_query(x, ada, g2, wqt, skh, skl, *, tb, rows_per_ada):
    T = x.shape[0]
    qd = wqt.shape[0]
    const = lambda *shape: pl.BlockSpec(shape, lambda j: (0,) * len(shape))
    tab = lambda rows: pl.BlockSpec((PEER_HEADS, rows, tb), lambda j: (0, 0, j))
    tab_shape = lambda rows: jax.ShapeDtypeStruct((PEER_HEADS, rows, T), jnp.uint32)
    half = PEER_KEYS // 2
    return pl.pallas_call(
        _peer_query_kernel,
        grid=(T // tb,),
        in_specs=[pl.BlockSpec((tb, D_MODEL), lambda j: (j, 0)),
                  pl.BlockSpec((1, N_ADA, D_MODEL), lambda j: ((j * tb) // rows_per_ada, 0, 0)),
                  const(1, D_MODEL), const(qd, D_MODEL),
                  const(2 * PEER_HEADS, PEER_KEYS, PEER_KEYS),
                  const(2 * PEER_HEADS, PEER_KEYS, PEER_KEYS)],
        out_specs=[pl.BlockSpec((D_MODEL, tb), lambda j: (0, j)), tab(half), tab(half),
                   tab(PEER_KEYS), tab(PEER_KEYS)],
        out_shape=[jax.ShapeDtypeStruct((D_MODEL, T), BF16), tab_shape(half), tab_shape(half),
                   tab_shape(PEER_KEYS), tab_shape(PEER_KEYS)],
        scratch_shapes=[pltpu.VMEM((qd, tb), F32)],
        compiler_params=pltpu.CompilerParams(
            dimension_semantics=("parallel",), vmem_limit_bytes=VMEM_LIMIT),
        name="peer_query",
    )(x, ada, g2, wqt, skh, skl)


def _peer_dense_kernel(h2t_ref, u_ref, vt_ref, r2_ref, a2_ref, n1_ref, b1_ref, x_ref, ada_ref,
                       o_ref, acc_ref, p_ref, act_ref, *, n_et):
    i = pl.program_id(1)
    te, tb = p_ref.shape

    @pl.when(i == 0)
    def _():
        acc_ref[...] = jnp.zeros_like(acc_ref)
        p_ref[...] = jnp.zeros_like(p_ref)

    def packed_row(ref, hd, j, cs):
        word = jnp.broadcast_to(ref[hd, j:j + 1, cs], (SUBLANES, LANES))
        return pltpu.bitcast(word, BF16)

    def contract_previous():
        acc_ref[...] += _dot(vt_ref[...], p_ref[...])

    @pl.when(i < n_et)
    def _():
        act_ref[...] = _gelu(_dot(u_ref[...], h2t_ref[...])).astype(BF16)
        contract_previous()
        slabs = PEER_KEYS // BF16_ROWS
        for j in range(te // PEER_KEYS):
            for c0 in range(0, tb // LANES, GATE_CHUNKS):
                for k0 in range(0, slabs, GATE_SLABS):
                    g = [[None] * GATE_SLABS for _ in range(GATE_CHUNKS)]
                    for hd in range(PEER_HEADS):
                        for cc in range(GATE_CHUNKS):
                            cs = slice((c0 + cc) * LANES, (c0 + cc + 1) * LANES)
                            cnt = packed_row(n1_ref, hd, j, cs)
                            wgt = packed_row(b1_ref, hd, (j + B1_ROW_SHIFT) % SUBLANES, cs)
                            for kk in range(GATE_SLABS):
                                words = slice((k0 + kk) * SUBLANES, (k0 + kk + 1) * SUBLANES)
                                rank = pltpu.bitcast(r2_ref[hd, words, cs], BF16)
                                val = pltpu.bitcast(a2_ref[hd, words, cs], BF16)
                                term = jnp.where(rank < cnt, val, jnp.zeros_like(val)) * wgt
                                g[cc][kk] = term if hd == 0 else g[cc][kk] + term
                    for cc in range(GATE_CHUNKS):
                        cs = slice((c0 + cc) * LANES, (c0 + cc + 1) * LANES)
                        for kk in range(GATE_SLABS):
                            r0 = j * PEER_KEYS + (k0 + kk) * BF16_ROWS
                            p_ref[r0:r0 + BF16_ROWS, cs] = g[cc][kk] * act_ref[r0:r0 + BF16_ROWS, cs]

    @pl.when(i == n_et)
    def _():
        contract_previous()
        o_ref[...] = x_ref[...] + ada_ref[0, 5:6, :] * acc_ref[...].T


def _peer_dense(x, h2t, tabs, u, vt, ada, *, tb, te, rows_per_ada):
    T = x.shape[0]
    n_exp = u.shape[0]
    n_et = n_exp // te
    n1 = te // PEER_KEYS
    assert T % tb == 0 and rows_per_ada % tb == 0 and n_exp % te == 0 and n1 == SUBLANES
    r2, a2, n1w, b1w = tabs
    tab = pl.BlockSpec((PEER_HEADS, PEER_KEYS // 2, tb), lambda j, i: (0, 0, j))
    tab1 = pl.BlockSpec((PEER_HEADS, n1, tb), lambda j, i: (0, jnp.minimum(i, n_et - 1), j))
    return pl.pallas_call(
        functools.partial(_peer_dense_kernel, n_et=n_et),
        grid=(T // tb, n_et + 1),
        in_specs=[pl.BlockSpec((D_MODEL, tb), lambda j, i: (0, j)),
                  pl.BlockSpec((te, D_MODEL), lambda j, i: (jnp.minimum(i, n_et - 1), 0)),
                  pl.BlockSpec((D_MODEL, te), lambda j, i: (0, jnp.maximum(i - 1, 0))),
                  tab, tab, tab1, tab1,
                  pl.BlockSpec((tb, D_MODEL), lambda j, i: (j, 0)),
                  pl.BlockSpec((1, N_ADA, D_MODEL), lambda j, i: ((j * tb) // rows_per_ada, 0, 0))],
        out_specs=pl.BlockSpec((tb, D_MODEL), lambda j, i: (j, 0)),
        out_shape=jax.ShapeDtypeStruct(x.shape, F32),
        scratch_shapes=[pltpu.VMEM((D_MODEL, tb), F32), pltpu.VMEM((te, tb), BF16),
                        pltpu.VMEM((te, tb), BF16)],
        compiler_params=pltpu.CompilerParams(
            dimension_semantics=("parallel", "arbitrary"), vmem_limit_bytes=VMEM_LIMIT),
        name="peer_dense",
    )(h2t, u, vt, r2, a2, n1w, b1w, x, ada)


def _prep_tables_kernel(u_ref, v_ref, ub_ref, vt_ref):
    ub_ref[...] = u_ref[0].astype(BF16)
    vt_ref[...] = v_ref[0].T.astype(BF16)


def _prep_tables(peer_u, peer_v, layer, *, te):
    _, n_exp, d = peer_u.shape
    rows = pl.BlockSpec((1, te, d), lambda i: (layer, i, 0))
    return pl.pallas_call(
        _prep_tables_kernel,
        grid=(n_exp // te,),
        in_specs=[rows, rows],
        out_specs=[pl.BlockSpec((te, d), lambda i: (i, 0)), pl.BlockSpec((d, te), lambda i: (0, i))],
        out_shape=[jax.ShapeDtypeStruct((n_exp, d), BF16), jax.ShapeDtypeStruct((d, n_exp), BF16)],
        compiler_params=pltpu.CompilerParams(
            dimension_semantics=("parallel",), vmem_limit_bytes=VMEM_LIMIT),
        name="prep_tables",
    )(peer_u, peer_v)


def _peer(x, ada, g2, wq, subkeys, u_tab, vt_tab, *, rows_per_ada):
    wqt = wq.T.astype(BF16)
    sk = jnp.stack([subkeys[:, 0], jnp.concatenate([subkeys[:, 1, 0::2], subkeys[:, 1, 1::2]], axis=1)],
                   axis=1)
    skh, skl = _split(sk.reshape(2 * PEER_HEADS, PEER_KEYS, -1))
    h2t, *tabs = _peer_query(x, ada, g2, wqt, skh, skl, tb=256, rows_per_ada=rows_per_ada)
    return _peer_dense(x, h2t, tabs, u_tab, vt_tab, ada, tb=1024, te=1024, rows_per_ada=rows_per_ada)


def _rope_tables(L):
    pos = jnp.arange(L, dtype=jnp.int32)
    r = (pos // GRID_W).astype(F32)
    col = (pos % GRID_W).astype(F32)
    n_freq = D_HEAD_DIM // 4
    inv = ROPE_THETA ** (-jnp.arange(n_freq, dtype=F32) / n_freq)
    ang = jnp.concatenate([r[:, None] * inv, col[:, None] * inv], axis=-1)
    cos = jnp.repeat(jnp.cos(ang), 2, axis=-1)
    sin = jnp.repeat(jnp.sin(ang), 2, axis=-1) * jnp.tile(jnp.array([-1.0, 1.0], F32), D_HEAD_DIM // 2)
    reps = MIX_W // D_HEAD_DIM
    return jnp.tile(cos, (1, reps)), jnp.tile(sin, (1, reps))


def kernel(x, c, ctx, c_ctx, w_ada, b_ada, norm1_g, norm2_g, w_in, w_out, a_conv_w, b_vnorm_g,
           b_spatial_w, b_spatial_b, c_conv_w, c_conv_b, c_norm_g, d_qk_norm_g, d_lambda,
           d_subln_g, peer_wq, peer_subkeys, peer_u, peer_v):
    bsz, L, _ = x.shape
    lc = ctx.shape[1]
    depth = w_ada.shape[0]
    assert depth == 2 and bsz + 1 <= ADA_ROWS

    cc = jnp.zeros((ADA_ROWS, D_MODEL), F32).at[:bsz].set(c).at[bsz].set(c_ctx)
    ada = _ada(cc, w_ada, b_ada).reshape(depth, ADA_ROWS, N_ADA, D_MODEL)
    row = lambda v: v.reshape(1, -1)

    ada_x, ada_c = ada[0, :bsz], ada[0, bsz:bsz + 1]
    bsb = jnp.repeat(b_spatial_b[0].T, MIX_W // B_GROUPS, axis=1)
    even = functools.partial(
        _even_mixer, g1=row(norm1_g[0]), win=w_in[0].astype(BF16), wout=w_out[0].astype(BF16),
        cw=a_conv_w[0], vg=row(b_vnorm_g[0]), ws=b_spatial_w[0].astype(BF16), bsb=bsb)
    u_tab, vt_tab = _prep_tables(peer_u, peer_v, 0, te=512)
    peer0 = functools.partial(_peer, g2=row(norm2_g[0]), wq=peer_wq[0], subkeys=peer_subkeys[0],
                              u_tab=u_tab, vt_tab=vt_tab)
    x = even(x, ada_x, tm=512)
    x = peer0(x.reshape(bsz * L, D_MODEL), ada_x, rows_per_ada=L).reshape(bsz, L, D_MODEL)
    ctx = even(ctx, jnp.broadcast_to(ada_c, (bsz, N_ADA, D_MODEL)), tm=lc)
    ctx = peer0(ctx.reshape(bsz * lc, D_MODEL), ada_c, rows_per_ada=bsz * lc).reshape(bsz, lc, D_MODEL)

    ada_x, ada_c = ada[1, :bsz], ada[1, bsz:bsz + 1]
    lam_init = 0.8 - 0.6 * math.exp(-0.3 * 1)
    win = w_in[1].astype(BF16)
    cos_t, sin_t = _rope_tables(L)
    seg_id = jnp.arange(MIX_W, dtype=jnp.int32) // D_HEAD_DIM
    seg = (seg_id[:, None] == seg_id[None, :]).astype(BF16)
    reps = MIX_W // D_HEAD_DIM
    qg, kg = row(jnp.tile(d_qk_norm_g[0, 0], reps)), row(jnp.tile(d_qk_norm_g[0, 1], reps))
    conf, q, ktx, vx = _odd_in(x, ada_x, row(norm1_g[1]), win, c_conv_w[0], row(c_conv_b[0]),
                               row(c_norm_g[0]), qg, kg, cos_t, sin_t, seg, tm=512)
    ktc, vc = _ctx_kv(ctx, jnp.broadcast_to(ada_c, (bsz, N_ADA, D_MODEL)), row(norm1_g[1]),
                      win[:, 3 * MIX_W:], kg, seg)
    x = _attn(q, ktc, ktx, vc, vx, conf, x, ada_x, w_out[1].astype(BF16), d_lambda[0],
              row(d_subln_g[0]), lam_init=lam_init, tq=512)
    u_tab, vt_tab = _prep_tables(peer_u, peer_v, 1, te=512)
    x = _peer(x.reshape(bsz * L, D_MODEL), ada_x, row(norm2_g[1]), peer_wq[1], peer_subkeys[1],
              u_tab, vt_tab, rows_per_ada=L)
    return x.reshape(bsz, L, D_MODEL)
```

```python
import functools
import math

import jax
import jax.numpy as jnp
from jax import lax
from jax.experimental import pallas as pl
from jax.experimental.pallas import tpu as pltpu

F32 = jnp.float32
BF16 = jnp.bfloat16

D_MODEL = 1024
MIX_W = 512
IN_W = 5 * MIX_W
GRID_W = 64
CHUNK = 128
B_GROUPS = 8
A_CONV = 3
C_CONV = 31
D_HEADS = 4
D_HEAD_DIM = 64
ROPE_THETA = 10000.0
N_ADA = 6
PEER_HEADS = 8
PEER_KEYS = 128
PEER_TOPK = 16
EPS = 1e-6

SUBLANES, LANES = 8, 128
BF16_ROWS = 2 * SUBLANES
GATE_CHUNKS = 4
GATE_SLABS = 4
B1_ROW_SHIFT = 4

ADA_ROWS = 16
CONV_HALO = 16
VMEM_LIMIT = 56 * 1024 * 1024

_STAIR = ((1, 8), (2, 5), (3, 4), (4, 3), (5, 2), (6, 2), (7, 2))


LOG2E = math.log2(math.e)
_GELU_C0 = -2.0 * math.sqrt(2.0 / math.pi) * LOG2E
_GELU_C1 = 0.044715 * _GELU_C0


def _gelu(x):
    return x / (1.0 + jnp.exp2(x * (_GELU_C0 + _GELU_C1 * (x * x))))


def _sigmoid(x):
    return 1.0 / (1.0 + jnp.exp(-x))


def _rms(x, g):
    return x * lax.rsqrt(jnp.mean(x * x, axis=-1, keepdims=True) + EPS) * g


def _norm_mod(x, g, shift, scale):
    return _rms(x, g) * (1.0 + scale) + shift


def _split(x):
    hi = x.astype(BF16)
    lo = (x - hi.astype(F32)).astype(BF16)
    return hi, lo


def _dot(a, b):
    return jnp.dot(a, b, preferred_element_type=F32)


def _ada_kernel(c_ref, w_ref, b_ref, o_ref):
    cc = c_ref[...]
    s = cc * _sigmoid(cc)
    o_ref[0] = jnp.dot(s, w_ref[0], precision=lax.Precision.HIGHEST,
                       preferred_element_type=F32) + b_ref[0]


def _ada(cc, w_ada, b_ada):
    depth, _, n = w_ada.shape
    tn = 1536
    return pl.pallas_call(
        _ada_kernel,
        grid=(depth, n // tn),
        in_specs=[pl.BlockSpec((ADA_ROWS, D_MODEL), lambda l, j: (0, 0)),
                  pl.BlockSpec((1, D_MODEL, tn), lambda l, j: (l, 0, j)),
                  pl.BlockSpec((1, 1, tn), lambda l, j: (l, 0, j))],
        out_specs=pl.BlockSpec((1, ADA_ROWS, tn), lambda l, j: (l, 0, j)),
        out_shape=jax.ShapeDtypeStruct((depth, ADA_ROWS, n), F32),
        compiler_params=pltpu.CompilerParams(vmem_limit_bytes=VMEM_LIMIT),
        name="ada",
    )(cc, w_ada, b_ada.reshape(depth, 1, n))


def _even_kernel(x_ref, xp_ref, xn_ref, ada_ref, g1_ref, win_ref, wout_ref, cw_ref, vg_ref,
                 ws_ref, bsb_ref, o_ref, *, n_tiles):
    t = pl.program_id(1)
    tm = x_ref.shape[1]
    shift, scale, gate = ada_ref[0, 0:1, :], ada_ref[0, 1:2, :], ada_ref[0, 2:3, :]
    g1 = g1_ref[...]

    x = x_ref[0]
    xs = jnp.concatenate([xp_ref[0], x, xn_ref[0]], axis=0)
    h = _norm_mod(xs, g1, shift, scale).astype(BF16)
    proj_all = _dot(h, win_ref[...])
    proj = proj_all[SUBLANES:SUBLANES + tm]
    a_h, a_c, a_b = proj[:, 0:MIX_W], proj[:, MIX_W:2 * MIX_W], proj[:, 2 * MIX_W:3 * MIX_W]
    b_u, b_v = proj[:, 3 * MIX_W:4 * MIX_W], proj[:, 4 * MIX_W:5 * MIX_W]

    z = a_c * a_h

    def edge_z(r):
        return proj_all[r:r + 1, 0:MIX_W] * proj_all[r:r + 1, MIX_W:2 * MIX_W]

    zp = jnp.where(t > 0, edge_z(SUBLANES - 1), 0.0)
    zn = jnp.where(t < n_tiles - 1, edge_z(SUBLANES + tm), 0.0)
    row = lax.broadcasted_iota(jnp.int32, (tm, MIX_W), 0)
    z_m1 = jnp.where(row == 0, zp, pltpu.roll(z, 1, axis=0))
    z_p1 = jnp.where(row == tm - 1, zn, pltpu.roll(z, tm - 1, axis=0))
    ya = a_b * (cw_ref[0:1, :] * z_m1 + cw_ref[1:2, :] * z + cw_ref[2:3, :] * z_p1)

    u = _gelu(b_u)
    v = _rms(_gelu(b_v), vg_ref[...]).astype(BF16)
    lane = lax.broadcasted_iota(jnp.int32, (CHUNK, CHUNK), 1)
    group_w = MIX_W // B_GROUPS
    rows = []
    for n in range(tm // CHUNK):
        cols = []
        for j in range(MIX_W // CHUNK):
            vv = v[n * CHUNK:(n + 1) * CHUNK, j * CHUNK:(j + 1) * CHUNK]
            s0 = _dot(ws_ref[2 * j], vv)
            s1 = _dot(ws_ref[2 * j + 1], vv)
            cols.append(jnp.where(lane < group_w, s0, s1))
        rows.append(jnp.concatenate(cols, axis=1) + bsb_ref[...])
    s = jnp.concatenate(rows, axis=0)
    yb = u * s

    y = jnp.concatenate([ya, yb], axis=1).astype(BF16)
    o_ref[0] = x + gate * _dot(y, wout_ref[...])


def _even_mixer(x, ada, g1, win, wout, cw, vg, ws, bsb, *, tm):
    bsz, L, _ = x.shape
    n_tiles = L // tm
    hb = tm // 8
    const = lambda *shape: pl.BlockSpec(shape, lambda b, t: (0,) * len(shape))
    return pl.pallas_call(
        functools.partial(_even_kernel, n_tiles=n_tiles),
        grid=(bsz, n_tiles),
        in_specs=[pl.BlockSpec((1, tm, D_MODEL), lambda b, t: (b, t, 0)),
                  pl.BlockSpec((1, 8, D_MODEL), lambda b, t: (b, jnp.maximum(t * hb - 1, 0), 0)),
                  pl.BlockSpec((1, 8, D_MODEL),
                               lambda b, t: (b, jnp.minimum((t + 1) * hb, L // 8 - 1), 0)),
                  pl.BlockSpec((1, N_ADA, D_MODEL), lambda b, t: (b, 0, 0)),
                  const(1, D_MODEL), const(D_MODEL, IN_W), const(2 * MIX_W, D_MODEL),
                  const(A_CONV, MIX_W), const(1, MIX_W), const(B_GROUPS, CHUNK, CHUNK),
                  const(CHUNK, MIX_W)],
        out_specs=pl.BlockSpec((1, tm, D_MODEL), lambda b, t: (b, t, 0)),
        out_shape=jax.ShapeDtypeStruct(x.shape, F32),
        compiler_params=pltpu.CompilerParams(
            dimension_semantics=("parallel", "parallel"), vmem_limit_bytes=VMEM_LIMIT),
        name="even_mixer",
    )(x, x, x, ada, g1, win, wout, cw, vg, ws, bsb)


def _head_norm(t, gain, seg_ref):
    hi, lo = _split(t * t)
    ss = _dot(hi, seg_ref[...]) + _dot(lo, seg_ref[...])
    return t * lax.rsqrt(ss * (1.0 / D_HEAD_DIM) + EPS) * gain


def _odd_in_kernel(x_ref, xp_ref, xn_ref, ada_ref, g1_ref, win_ref, cw_ref, cb_ref, cg_ref,
                   qg_ref, kg_ref, cos_ref, sin_ref, seg_ref,
                   conf_ref, q_ref, kt_ref, v_ref, *, n_tiles):
    t = pl.program_id(1)
    tm = x_ref.shape[1]
    rows_all = tm + 2 * CONV_HALO
    shift, scale = ada_ref[0, 0:1, :], ada_ref[0, 1:2, :]
    g1 = g1_ref[...]

    xs = jnp.concatenate([xp_ref[0], x_ref[0], xn_ref[0]], axis=0)
    h = _norm_mod(xs, g1, shift, scale).astype(BF16)
    proj_all = _dot(h, win_ref[...])
    proj = proj_all[CONV_HALO:CONV_HALO + tm]

    glu = proj_all[:, 0:MIX_W] * _sigmoid(proj_all[:, MIX_W:2 * MIX_W])
    row = lax.broadcasted_iota(jnp.int32, (rows_all, MIX_W), 0)
    outside = jnp.logical_or(jnp.logical_and(row < CONV_HALO, t == 0),
                             jnp.logical_and(row >= CONV_HALO + tm, t == n_tiles - 1))
    glu = jnp.where(outside, 0.0, glu)
    first = CONV_HALO - C_CONV // 2
    acc = jnp.zeros((tm, MIX_W), F32) + cb_ref[...]
    for b in range(SUBLANES):
        shifted = glu if b == 0 else pltpu.roll(glu, rows_all - b, axis=0)
        for a in range(rows_all // SUBLANES):
            k = SUBLANES * a + b - first
            if 0 <= k < C_CONV:
                acc = acc + cw_ref[k:k + 1, :] * shifted[SUBLANES * a:SUBLANES * a + tm]
    yn = _rms(acc, cg_ref[...])
    conf_ref[0] = (yn * _sigmoid(yn)).astype(BF16)

    lane = lax.broadcasted_iota(jnp.int32, (tm, MIX_W), 1)
    even_lane = (lane % 2) == 0

    def rope(tn):
        partner = jnp.where(even_lane, pltpu.roll(tn, MIX_W - 1, axis=1), pltpu.roll(tn, 1, axis=1))
        return tn * cos_ref[...] + partner * sin_ref[...]

    q = rope(_head_norm(proj[:, 2 * MIX_W:3 * MIX_W], qg_ref[...], seg_ref))
    k = rope(_head_norm(proj[:, 3 * MIX_W:4 * MIX_W], kg_ref[...], seg_ref))
    q_ref[0] = (q * (D_HEAD_DIM ** -0.5 * LOG2E)).astype(BF16)
    kt_ref[0] = k.T.astype(BF16)
    v_ref[0] = proj[:, 4 * MIX_W:5 * MIX_W].astype(BF16)


def _odd_in(x, ada, g1, win, cw, cb, cg, qg, kg, cos_t, sin_t, seg, *, tm):
    bsz, L, _ = x.shape
    n_tiles = L // tm
    hb = tm // CONV_HALO
    const = lambda *shape: pl.BlockSpec(shape, lambda b, t: (0,) * len(shape))
    tok = lambda w: pl.BlockSpec((1, tm, w), lambda b, t: (b, t, 0))
    return pl.pallas_call(
        functools.partial(_odd_in_kernel, n_tiles=n_tiles),
        grid=(bsz, n_tiles),
        in_specs=[tok(D_MODEL),
                  pl.BlockSpec((1, CONV_HALO, D_MODEL),
                               lambda b, t: (b, jnp.maximum(t * hb - 1, 0), 0)),
                  pl.BlockSpec((1, CONV_HALO, D_MODEL),
                               lambda b, t: (b, jnp.minimum((t + 1) * hb, L // CONV_HALO - 1), 0)),
                  pl.BlockSpec((1, N_ADA, D_MODEL), lambda b, t: (b, 0, 0)),
                  const(1, D_MODEL), const(D_MODEL, IN_W), const(C_CONV, MIX_W),
                  const(1, MIX_W), const(1, MIX_W), const(1, MIX_W), const(1, MIX_W),
                  pl.BlockSpec((tm, MIX_W), lambda b, t: (t, 0)),
                  pl.BlockSpec((tm, MIX_W), lambda b, t: (t, 0)),
                  const(MIX_W, MIX_W)],
        out_specs=[tok(MIX_W), tok(MIX_W),
                   pl.BlockSpec((1, MIX_W, tm), lambda b, t: (b, 0, t)),
                   tok(MIX_W)],
        out_shape=[jax.ShapeDtypeStruct((bsz, L, MIX_W), BF16),
                   jax.ShapeDtypeStruct((bsz, L, MIX_W), BF16),
                   jax.ShapeDtypeStruct((bsz, MIX_W, L), BF16),
                   jax.ShapeDtypeStruct((bsz, L, MIX_W), BF16)],
        compiler_params=pltpu.CompilerParams(
            dimension_semantics=("parallel", "parallel"), vmem_limit_bytes=VMEM_LIMIT),
        name="odd_in",
    )(x, x, x, ada, g1, win, cw, cb, cg, qg, kg, cos_t, sin_t, seg)


def _ctx_kv_kernel(x_ref, ada_ref, g1_ref, wkv_ref, kg_ref, seg_ref, kt_ref, v_ref):
    shift, scale = ada_ref[0, 0:1, :], ada_ref[0, 1:2, :]
    h = _norm_mod(x_ref[0], g1_ref[...], shift, scale).astype(BF16)
    proj = _dot(h, wkv_ref[...])
    k = _head_norm(proj[:, 0:MIX_W], kg_ref[...], seg_ref)
    kt_ref[0] = k.T.astype(BF16)
    v_ref[0] = proj[:, MIX_W:2 * MIX_W].astype(BF16)


def _ctx_kv(ctx, ada, g1, wkv, kg, seg):
    bsz, L, _ = ctx.shape
    const = lambda *shape: pl.BlockSpec(shape, lambda b: (0,) * len(shape))
    return pl.pallas_call(
        _ctx_kv_kernel,
        grid=(bsz,),
        in_specs=[pl.BlockSpec((1, L, D_MODEL), lambda b: (b, 0, 0)),
                  pl.BlockSpec((1, N_ADA, D_MODEL), lambda b: (b, 0, 0)),
                  const(1, D_MODEL), const(D_MODEL, 2 * MIX_W), const(1, MIX_W),
                  const(MIX_W, MIX_W)],
        out_specs=[pl.BlockSpec((1, MIX_W, L), lambda b: (b, 0, 0)),
                   pl.BlockSpec((1, L, MIX_W), lambda b: (b, 0, 0))],
        out_shape=[jax.ShapeDtypeStruct((bsz, MIX_W, L), BF16),
                   jax.ShapeDtypeStruct((bsz, L, MIX_W), BF16)],
        compiler_params=pltpu.CompilerParams(
            dimension_semantics=("parallel",), vmem_limit_bytes=VMEM_LIMIT),
        name="ctx_kv",
    )(ctx, ada, g1, wkv, kg, seg)


def _attn_kernel(q_ref, ktc_ref, ktx_ref, vc_ref, vx_ref, conf_ref, x_ref, ada_ref, wout_ref,
                 lam_ref, sg_ref, o_ref, *, lam_init):
    tq = q_ref.shape[1]
    vw = 2 * D_HEAD_DIM
    lam_p = lam_ref[...]
    lam = (jnp.exp(jnp.sum(lam_p[0:1] * lam_p[1:2], axis=-1, keepdims=True))
           - jnp.exp(jnp.sum(lam_p[2:3] * lam_p[3:4], axis=-1, keepdims=True)) + lam_init)
    first_map = lax.broadcasted_iota(jnp.int32, (tq, vw), 1) < D_HEAD_DIM
    outs = []
    for hd in range(D_HEADS):
        sl = slice(hd * vw, (hd + 1) * vw)
        qh = q_ref[0, :, sl]
        maps = []
        for m in range(2):
            qm = jnp.where(first_map if m == 0 else jnp.logical_not(first_map), qh, jnp.zeros_like(qh))
            sc = _dot(qm, ktc_ref[0, sl, :])
            sx = _dot(qm, ktx_ref[0, sl, :])
            mx = jnp.maximum(jnp.max(sc, axis=-1, keepdims=True), jnp.max(sx, axis=-1, keepdims=True))
            pc, px = jnp.exp2(sc - mx), jnp.exp2(sx - mx)
            inv = 1.0 / (jnp.sum(pc, axis=-1, keepdims=True) + jnp.sum(px, axis=-1, keepdims=True))
            pv = _dot(pc.astype(BF16), vc_ref[0, :, sl]) + _dot(px.astype(BF16), vx_ref[0, :, sl])
            maps.append(pv * inv)
        o = maps[0] - lam * maps[1]
        outs.append(_rms(o, sg_ref[...]) * (1.0 - lam_init))
    y = jnp.concatenate([conf_ref[0]] + [o.astype(BF16) for o in outs], axis=1)
    o_ref[0] = x_ref[0] + ada_ref[0, 2:3, :] * _dot(y, wout_ref[...])


def _attn(q, ktc, ktx, vc, vx, conf, x, ada, wout, lam_p, sg, *, lam_init, tq):
    bsz, L, _ = x.shape
    lc = vc.shape[1]
    const = lambda *shape: pl.BlockSpec(shape, lambda b, t: (0,) * len(shape))
    tok = lambda w: pl.BlockSpec((1, tq, w), lambda b, t: (b, t, 0))
    return pl.pallas_call(
        functools.partial(_attn_kernel, lam_init=lam_init),
        grid=(bsz, L // tq),
        in_specs=[tok(MIX_W),
                  pl.BlockSpec((1, MIX_W, lc), lambda b, t: (b, 0, 0)),
                  pl.BlockSpec((1, MIX_W, L), lambda b, t: (b, 0, 0)),
                  pl.BlockSpec((1, lc, MIX_W), lambda b, t: (b, 0, 0)),
                  pl.BlockSpec((1, L, MIX_W), lambda b, t: (b, 0, 0)),
                  tok(MIX_W), tok(D_MODEL),
                  pl.BlockSpec((1, N_ADA, D_MODEL), lambda b, t: (b, 0, 0)),
                  const(2 * MIX_W, D_MODEL), const(4, D_HEAD_DIM), const(1, 2 * D_HEAD_DIM)],
        out_specs=tok(D_MODEL),
        out_shape=jax.ShapeDtypeStruct(x.shape, F32),
        compiler_params=pltpu.CompilerParams(
            dimension_semantics=("parallel", "parallel"), vmem_limit_bytes=VMEM_LIMIT),
        name="diff_attn",
    )(q, ktc, ktx, vc, vx, conf, x, ada, wout, lam_p, sg)


def _exact_ranks(s, label):
    work = s
    rank = jnp.full(s.shape, float(PEER_TOPK), F32)
    for r in range(PEER_TOPK):
        m = jnp.max(work, axis=0, keepdims=True)
        first = jnp.min(jnp.where(work == m, label, 1e9), axis=0, keepdims=True)
        sel = label == first
        rank = jnp.where(sel, float(r), rank)
        work = jnp.where(sel, -jnp.inf, work)
    return rank


def _batcher_network(n):
    def merge(lo, hi, r):
        step = 2 * r
        if step < hi - lo:
            yield from merge(lo, hi, step)
            yield from merge(lo + r, hi, step)
            yield from [(i, i + r) for i in range(lo + r, hi - r, step)]
        else:
            yield (lo, lo + r)

    def sort(lo, hi):
        if hi > lo:
            mid = lo + (hi - lo) // 2
            yield from sort(lo, mid)
            yield from sort(mid + 1, hi)
            yield from merge(lo, hi, 1)

    return tuple(sort(0, n - 1))


_SORT16 = _batcher_network(PEER_TOPK)


def _sublane_allreduce(x, op):
    for shift in (4, 2, 1):
        x = op(x, pltpu.roll(x, shift, axis=0))
    return x


def _sorted_top16(rows):
    b = list(rows)
    for i, j in _SORT16:
        b[i], b[j] = jnp.maximum(b[i], b[j]), jnp.minimum(b[i], b[j])
    for shift in (4, 2, 1):
        m = [jnp.maximum(b[i], pltpu.roll(b[PEER_TOPK - 1 - i], shift, axis=0))
             for i in range(PEER_TOPK)]
        d = PEER_TOPK // 2
        while d:
            for i in range(PEER_TOPK):
                if not i & d:
                    m[i], m[i + d] = jnp.maximum(m[i], m[i + d]), jnp.minimum(m[i], m[i + d])
            d //= 2
        b = m
    return b


def _lookup_by_rank(rows, top, values):
    out = []
    for a in rows:
        r = jnp.zeros_like(a) + values[0]
        for k in range(PEER_TOPK):
            r = jnp.where(top[k] > a, values[k + 1], r)
        out.append(r)
    return jnp.concatenate(out, axis=0)


def _tie_flag(rows, top):
    flag = jnp.zeros_like(top[0])
    for k in range(PEER_TOPK - 1):
        flag = jnp.where(top[k] == top[k + 1], 1.0, flag)
    n_ge = None
    for a in rows:
        hit = jnp.where(a >= top[PEER_TOPK - 1], 1.0, 0.0)
        n_ge = hit if n_ge is None else n_ge + hit
    n_ge = _sublane_allreduce(n_ge, jnp.add)
    return jnp.where(n_ge != float(PEER_TOPK), 1.0, flag)


def _dup_bf16(v):
    bits = pltpu.bitcast(v.astype(BF16).astype(F32), jnp.uint32)
    return bits | (bits >> 16)


def _pack_bf16(v):
    n = v.shape[0] // 2
    bits = pltpu.bitcast(v.astype(BF16).astype(F32), jnp.uint32)
    return (bits[:n] >> 16) | bits[n:]


def _peer_query_kernel(x_ref, ada_ref, g2_ref, wq_ref, skh_ref, skl_ref,
                       h2t_ref, r2_ref, a2_ref, n1_ref, b1_ref, qt_ref):
    tb = x_ref.shape[0]
    shift, scale = ada_ref[0, 3:4, :], ada_ref[0, 4:5, :]
    h2 = _norm_mod(x_ref[...], g2_ref[...], shift, scale)
    h2t = h2.T
    hh = h2t.astype(BF16)
    h2t_ref[...] = hh
    qt_ref[...] = _dot(wq_ref[...], hh)

    pos = lax.broadcasted_iota(jnp.int32, (PEER_KEYS, tb), 0)
    label1 = pos.astype(F32)
    label2 = jnp.where(pos < PEER_KEYS // 2, 2 * pos, 2 * pos - (PEER_KEYS - 1)).astype(F32)
    sub = lax.broadcasted_iota(jnp.int32, (SUBLANES, tb), 0)
    subf = sub.astype(F32)
    cand_label = ([subf, subf + 8.0] + [subf + 16.0 * k1 for k1, _ in _STAIR]
                  + [16.0 * (subf + 8.0)])
    cand_valid = [None, None] + [None if nv == SUBLANES else sub < nv for _, nv in _STAIR] + [None]

    def spread(vals):
        out = vals[0]
        for u in range(1, SUBLANES):
            out = jnp.where(sub == u, vals[u], out)
        return out

    def head(hd, carry):
        def scores(p):
            qp = qt_ref[pl.ds(pl.multiple_of((2 * hd + p) * PEER_KEYS, PEER_KEYS), PEER_KEYS), :]
            qh, ql = _split(qp)
            kh, kl = skh_ref[2 * hd + p], skl_ref[2 * hd + p]
            return _dot(kh, qh) + (_dot(kh, ql) + _dot(kl, qh))

        s1, s2 = scores(0), scores(1)
        rows1 = [s1[SUBLANES * r:SUBLANES * (r + 1)] for r in range(PEER_KEYS // SUBLANES)]
        rows2 = [s2[SUBLANES * r:SUBLANES * (r + 1)] for r in range(PEER_KEYS // SUBLANES)]
        top1, top2 = _sorted_top16(rows1), _sorted_top16(rows2)

        t2a, t2b, t1b = spread(top2[:SUBLANES]), spread(top2[SUBLANES:]), spread(top1[SUBLANES:])
        cand = ([t2a + top1[0], t2b + top1[0]] + [t2a + top1[k1] for k1, _ in _STAIR]
                + [t1b + top2[0]])
        cand = [c if v is None else jnp.where(v, c, -jnp.inf) for c, v in zip(cand, cand_valid)]
        pad = [jnp.full((SUBLANES, tb), -jnp.inf, F32)] * (PEER_TOPK - len(cand))
        best = _sorted_top16(cand + pad)
        z = functools.reduce(jnp.add, [jnp.exp(b - best[0]) for b in best])
        above = [jnp.where(c >= best[PEER_TOPK - 1], 1.0, 0.0) for c in cand]
        n_above = _sublane_allreduce(functools.reduce(jnp.add, above), jnp.add)

        def take_in_index_order():
            left = cand
            for r in range(PEER_TOPK):
                m = _sublane_allreduce(functools.reduce(jnp.maximum, left), jnp.maximum)
                first = functools.reduce(
                    jnp.minimum, [jnp.where(c == m, lab, 1e9) for c, lab in zip(left, cand_label)])
                first = _sublane_allreduce(first, jnp.minimum)
                left = [jnp.where(lab == first, -jnp.inf, c) for c, lab in zip(left, cand_label)]
            return [jnp.where(jnp.logical_and(c0 > -jnp.inf, c == -jnp.inf), 1.0, 0.0)
                    for c0, c in zip(cand, left)]

        taken = lax.cond(jnp.max(jnp.abs(n_above - float(PEER_TOPK))) > 0.0,
                         take_in_index_order, lambda: above)
        counts = [_sublane_allreduce(taken[0] + taken[1], jnp.add)]
        counts += [_sublane_allreduce(t, jnp.add) for t in taken[2:-1]]
        counts += [_sublane_allreduce(jnp.where(sub == u, taken[-1], 0.0), jnp.add)
                   for u in range(SUBLANES)]
        counts.append(0.0)
        ranks = [float(k) for k in range(PEER_TOPK + 1)]

        def fast():
            return _lookup_by_rank(rows2, top2, ranks), _lookup_by_rank(rows1, top1, counts)

        def exact():
            rank1 = _exact_ranks(s1, label1)
            n1 = jnp.zeros_like(s1)
            for k1 in range(PEER_TOPK):
                n1 = jnp.where(rank1 == float(k1), jnp.tile(counts[k1], (PEER_KEYS // SUBLANES, 1)), n1)
            return _exact_ranks(s2, label2), n1

        ties = jnp.max(jnp.maximum(_tie_flag(rows1, top1), _tie_flag(rows2, top2)))
        rank2, n1 = lax.cond(ties > 0.0, exact, fast)
        tile = lambda v: jnp.tile(v, (PEER_KEYS // SUBLANES, 1))
        r2_ref[hd] = _pack_bf16(rank2)
        a2_ref[hd] = _pack_bf16(jnp.exp(s2 - tile(top2[0])))
        n1_ref[hd] = _dup_bf16(n1)
        wrows = [pltpu.roll(jnp.exp(a - top1[0]) * (1.0 / z), B1_ROW_SHIFT, axis=0) for a in rows1]
        b1_ref[hd] = _dup_bf16(jnp.concatenate(wrows, axis=0))
        return carry

    lax.fori_loop(0, PEER_HEADS, head, 0)


def _peer_query(x, ada, g2, wqt, skh, skl, *, tb, rows_per_ada):
    T = x.shape[0]
    qd = wqt.shape[0]
    const = lambda *shape: pl.BlockSpec(shape, lambda j: (0,) * len(shape))
    tab = lambda rows: pl.BlockSpec((PEER_HEADS, rows, tb), lambda j: (0, 0, j))
    tab_shape = lambda rows: jax.ShapeDtypeStruct((PEER_HEADS, rows, T), jnp.uint32)
    half = PEER_KEYS // 2
    return pl.pallas_call(
        _peer_query_kernel,
        grid=(T // tb,),
        in_specs=[pl.BlockSpec((tb, D_MODEL), lambda j: (j, 0)),
                  pl.BlockSpec((1, N_ADA, D_MODEL), lambda j: ((j * tb) // rows_per_ada, 0, 0)),
                  const(1, D_MODEL), const(qd, D_MODEL),
                  const(2 * PEER_HEADS, PEER_KEYS, PEER_KEYS),
                  const(2 * PEER_HEADS, PEER_KEYS, PEER_KEYS)],
        out_specs=[pl.BlockSpec((D_MODEL, tb), lambda j: (0, j)), tab(half), tab(half),
                   tab(PEER_KEYS), tab(PEER_KEYS)],
        out_shape=[jax.ShapeDtypeStruct((D_MODEL, T), BF16), tab_shape(half), tab_shape(half),
                   tab_shape(PEER_KEYS), tab_shape(PEER_KEYS)],
        scratch_shapes=[pltpu.VMEM((qd, tb), F32)],
        compiler_params=pltpu.CompilerParams(
            dimension_semantics=("parallel",), vmem_limit_bytes=VMEM_LIMIT),
        name="peer_query",
    )(x, ada, g2, wqt, skh, skl)


def _peer_dense_kernel(h2t_ref, u_ref, vt_ref, r2_ref, a2_ref, n1_ref, b1_ref, x_ref, ada_ref,
                       o_ref, acc_ref, p_ref, act_ref, *, n_et):
    i = pl.program_id(1)
    te, tb = p_ref.shape

    @pl.when(i == 0)
    def _():
        acc_ref[...] = jnp.zeros_like(acc_ref)
        p_ref[...] = jnp.zeros_like(p_ref)

    def packed_row(ref, hd, j, cs):
        word = jnp.broadcast_to(ref[hd, j:j + 1, cs], (SUBLANES, LANES))
        return pltpu.bitcast(word, BF16)

    def contract_previous():
        acc_ref[...] += _dot(vt_ref[...], p_ref[...])

    @pl.when(i < n_et)
    def _():
        act_ref[...] = _gelu(_dot(u_ref[...], h2t_ref[...])).astype(BF16)
        contract_previous()
        slabs = PEER_KEYS // BF16_ROWS
        for j in range(te // PEER_KEYS):
            for c0 in range(0, tb // LANES, GATE_CHUNKS):
                for k0 in range(0, slabs, GATE_SLABS):
                    g = [[None] * GATE_SLABS for _ in range(GATE_CHUNKS)]
                    for hd in range(PEER_HEADS):
                        for cc in range(GATE_CHUNKS):
                            cs = slice((c0 + cc) * LANES, (c0 + cc + 1) * LANES)
                            cnt = packed_row(n1_ref, hd, j, cs)
                            wgt = packed_row(b1_ref, hd, (j + B1_ROW_SHIFT) % SUBLANES, cs)
                            for kk in range(GATE_SLABS):
                                words = slice((k0 + kk) * SUBLANES, (k0 + kk + 1) * SUBLANES)
                                rank = pltpu.bitcast(r2_ref[hd, words, cs], BF16)
                                val = pltpu.bitcast(a2_ref[hd, words, cs], BF16)
                                term = jnp.where(rank < cnt, val, jnp.zeros_like(val)) * wgt
                                g[cc][kk] = term if hd == 0 else g[cc][kk] + term
                    for cc in range(GATE_CHUNKS):
                        cs = slice((c0 + cc) * LANES, (c0 + cc + 1) * LANES)
                        for kk in range(GATE_SLABS):
                            r0 = j * PEER_KEYS + (k0 + kk) * BF16_ROWS
                            p_ref[r0:r0 + BF16_ROWS, cs] = g[cc][kk] * act_ref[r0:r0 + BF16_ROWS, cs]

    @pl.when(i == n_et)
    def _():
        contract_previous()
        o_ref[...] = x_ref[...] + ada_ref[0, 5:6, :] * acc_ref[...].T


def _peer_dense(x, h2t, tabs, u, vt, ada, *, tb, te, rows_per_ada):
    T = x.shape[0]
    n_exp = u.shape[0]
    n_et = n_exp // te
    n1 = te // PEER_KEYS
    assert T % tb == 0 and rows_per_ada % tb == 0 and n_exp % te == 0 and n1 == SUBLANES
    r2, a2, n1w, b1w = tabs
    tab = pl.BlockSpec((PEER_HEADS, PEER_KEYS // 2, tb), lambda j, i: (0, 0, j))
    tab1 = pl.BlockSpec((PEER_HEADS, n1, tb), lambda j, i: (0, jnp.minimum(i, n_et - 1), j))
    return pl.pallas_call(
        functools.partial(_peer_dense_kernel, n_et=n_et),
        grid=(T // tb, n_et + 1),
        in_specs=[pl.BlockSpec((D_MODEL, tb), lambda j, i: (0, j)),
                  pl.BlockSpec((te, D_MODEL), lambda j, i: (jnp.minimum(i, n_et - 1), 0)),
                  pl.BlockSpec((D_MODEL, te), lambda j, i: (0, jnp.maximum(i - 1, 0))),
                  tab, tab, tab1, tab1,
                  pl.BlockSpec((tb, D_MODEL), lambda j, i: (j, 0)),
                  pl.BlockSpec((1, N_ADA, D_MODEL), lambda j, i: ((j * tb) // rows_per_ada, 0, 0))],
        out_specs=pl.BlockSpec((tb, D_MODEL), lambda j, i: (j, 0)),
        out_shape=jax.ShapeDtypeStruct(x.shape, F32),
        scratch_shapes=[pltpu.VMEM((D_MODEL, tb), F32), pltpu.VMEM((te, tb), BF16),
                        pltpu.VMEM((te, tb), BF16)],
        compiler_params=pltpu.CompilerParams(
            dimension_semantics=("parallel", "arbitrary"), vmem_limit_bytes=VMEM_LIMIT),
        name="peer_dense",
    )(h2t, u, vt, r2, a2, n1w, b1w, x, ada)


def _prep_tables_kernel(u_ref, v_ref, ub_ref, vt_ref):
    ub_ref[...] = u_ref[0].astype(BF16)
    vt_ref[...] = v_ref[0].T.astype(BF16)


def _prep_tables(peer_u, peer_v, layer, *, te):
    _, n_exp, d = peer_u.shape
    rows = pl.BlockSpec((1, te, d), lambda i: (layer, i, 0))
    return pl.pallas_call(
        _prep_tables_kernel,
        grid=(n_exp // te,),
        in_specs=[rows, rows],
        out_specs=[pl.BlockSpec((te, d), lambda i: (i, 0)), pl.BlockSpec((d, te), lambda i: (0, i))],
        out_shape=[jax.ShapeDtypeStruct((n_exp, d), BF16), jax.ShapeDtypeStruct((d, n_exp), BF16)],
        compiler_params=pltpu.CompilerParams(
            dimension_semantics=("parallel",), vmem_limit_bytes=VMEM_LIMIT),
        name="prep_tables",
    )(peer_u, peer_v)


def _peer(x, ada, g2, wq, subkeys, u_tab, vt_tab, *, rows_per_ada):
    wqt = wq.T.astype(BF16)
    sk = jnp.stack([subkeys[:, 0], jnp.concatenate([subkeys[:, 1, 0::2], subkeys[:, 1, 1::2]], axis=1)],
                   axis=1)
    skh, skl = _split(sk.reshape(2 * PEER_HEADS, PEER_KEYS, -1))
    h2t, *tabs = _peer_query(x, ada, g2, wqt, skh, skl, tb=512, rows_per_ada=rows_per_ada)
    return _peer_dense(x, h2t, tabs, u_tab, vt_tab, ada, tb=1024, te=1024, rows_per_ada=rows_per_ada)


def _rope_tables(L):
    pos = jnp.arange(L, dtype=jnp.int32)
    r = (pos // GRID_W).astype(F32)
    col = (pos % GRID_W).astype(F32)
    n_freq = D_HEAD_DIM // 4
    inv = ROPE_THETA ** (-jnp.arange(n_freq, dtype=F32) / n_freq)
    ang = jnp.concatenate([r[:, None] * inv, col[:, None] * inv], axis=-1)
    cos = jnp.repeat(jnp.cos(ang), 2, axis=-1)
    sin = jnp.repeat(jnp.sin(ang), 2, axis=-1) * jnp.tile(jnp.array([-1.0, 1.0], F32), D_HEAD_DIM // 2)
    reps = MIX_W // D_HEAD_DIM
    return jnp.tile(cos, (1, reps)), jnp.tile(sin, (1, reps))


def kernel(x, c, ctx, c_ctx, w_ada, b_ada, norm1_g, norm2_g, w_in, w_out, a_conv_w, b_vnorm_g,
           b_spatial_w, b_spatial_b, c_conv_w, c_conv_b, c_norm_g, d_qk_norm_g, d_lambda,
           d_subln_g, peer_wq, peer_subkeys, peer_u, peer_v):
    bsz, L, _ = x.shape
    lc = ctx.shape[1]
    depth = w_ada.shape[0]
    assert depth == 2 and bsz + 1 <= ADA_ROWS

    cc = jnp.zeros((ADA_ROWS, D_MODEL), F32).at[:bsz].set(c).at[bsz].set(c_ctx)
    ada = _ada(cc, w_ada, b_ada).reshape(depth, ADA_ROWS, N_ADA, D_MODEL)
    row = lambda v: v.reshape(1, -1)

    ada_x, ada_c = ada[0, :bsz], ada[0, bsz:bsz + 1]
    bsb = jnp.repeat(b_spatial_b[0].T, MIX_W // B_GROUPS, axis=1)
    even = functools.partial(
        _even_mixer, g1=row(norm1_g[0]), win=w_in[0].astype(BF16), wout=w_out[0].astype(BF16),
        cw=a_conv_w[0], vg=row(b_vnorm_g[0]), ws=b_spatial_w[0].astype(BF16), bsb=bsb)
    u_tab, vt_tab = _prep_tables(peer_u, peer_v, 0, te=512)
    peer0 = functools.partial(_peer, g2=row(norm2_g[0]), wq=peer_wq[0], subkeys=peer_subkeys[0],
                              u_tab=u_tab, vt_tab=vt_tab)
    x = even(x, ada_x, tm=512)
    x = peer0(x.reshape(bsz * L, D_MODEL), ada_x, rows_per_ada=L).reshape(bsz, L, D_MODEL)
    ctx = even(ctx, jnp.broadcast_to(ada_c, (bsz, N_ADA, D_MODEL)), tm=lc)
    ctx = peer0(ctx.reshape(bsz * lc, D_MODEL), ada_c, rows_per_ada=bsz * lc).reshape(bsz, lc, D_MODEL)

    ada_x, ada_c = ada[1, :bsz], ada[1, bsz:bsz + 1]
    lam_init = 0.8 - 0.6 * math.exp(-0.3 * 1)
    win = w_in[1].astype(BF16)
    cos_t, sin_t = _rope_tables(L)
    seg_id = jnp.arange(MIX_W, dtype=jnp.int32) // D_HEAD_DIM
    seg = (seg_id[:, None] == seg_id[None, :]).astype(BF16)
    reps = MIX_W // D_HEAD_DIM
    qg, kg = row(jnp.tile(d_qk_norm_g[0, 0], reps)), row(jnp.tile(d_qk_norm_g[0, 1], reps))
    conf, q, ktx, vx = _odd_in(x, ada_x, row(norm1_g[1]), win, c_conv_w[0], row(c_conv_b[0]),
                               row(c_norm_g[0]), qg, kg, cos_t, sin_t, seg, tm=512)
    ktc, vc = _ctx_kv(ctx, jnp.broadcast_to(ada_c, (bsz, N_ADA, D_MODEL)), row(norm1_g[1]),
                      win[:, 3 * MIX_W:], kg, seg)
    x = _attn(q, ktc, ktx, vc, vx, conf, x, ada_x, w_out[1].astype(BF16), d_lambda[0],
              row(d_subln_g[0]), lam_init=lam_init, tq=512)
    u_tab, vt_tab = _prep_tables(peer_u, peer_v, 1, te=512)
    x = _peer(x.reshape(bsz * L, D_MODEL), ada_x, row(norm2_g[1]), peer_wq[1], peer_subkeys[1],
              u_tab, vt_tab, rows_per_ada=L)
    return x.reshape(bsz, L, D_MODEL)
```
